```python
import math
import jax, jax.numpy as jnp
from jax import lax
import numpy as np

D_MODEL = 1024
BATCH = 4
SEQ = 4096
DEPTH = 2

HEAD_DIM = 64
N_HEADS_DIFF = 8
N_HEADS_MOBA = 8
DIFF_HALF = HEAD_DIM // 2
MOBA_BLOCK = 256
MOBA_TOPK = 3
MOBA_Q_CHUNK = 64
ATTN_Q_BLOCK = 128
GMLP_CHUNK = 128
GMLP_GROUPS = 8
GMLP_WIDTH = 512
CONV_CH = 512
CONV_KERNEL = 31
D_FF = -(-8 * D_MODEL // (3 * 256)) * 256
ROPE_THETA = 10000.0
EPS = 1e-6
N_EVEN = (DEPTH + 1) // 2
N_ODD = DEPTH // 2
EVEN_IN = 3 * N_HEADS_DIFF * HEAD_DIM + 3 * N_HEADS_MOBA * HEAD_DIM
EVEN_OUT = (N_HEADS_DIFF + N_HEADS_MOBA) * HEAD_DIM
ODD_IN = 2 * GMLP_WIDTH + 2 * CONV_CH
ODD_OUT = GMLP_WIDTH + CONV_CH

kernel_name = "hybrid_diffattn_moba_gmlp_conformer_block"


def rms_norm(x, g):
    xf = x.astype(jnp.float32)
    y = xf * lax.rsqrt(jnp.mean(xf * xf, axis=-1, keepdims=True) + EPS)
    return (y * g.astype(jnp.float32)).astype(x.dtype)


def layer_norm(x, g, b):
    xf = x.astype(jnp.float32)
    mu = jnp.mean(xf, axis=-1, keepdims=True)
    xc = xf - mu
    y = xc * lax.rsqrt(jnp.mean(xc * xc, axis=-1, keepdims=True) + EPS)
    return (y * g.astype(jnp.float32) + b.astype(jnp.float32)).astype(x.dtype)


def rope_tables(dim):
    inv = 1.0 / (ROPE_THETA ** (jnp.arange(0, dim, 2, dtype=jnp.float32) / dim))
    ang = jnp.arange(SEQ, dtype=jnp.float32)[:, None] * inv[None, :]
    return jnp.cos(ang), jnp.sin(ang)


def apply_rope(x, cos, sin):
    x1, x2 = jnp.split(x, 2, axis=-1)
    c = cos.astype(x.dtype)
    s = sin.astype(x.dtype)
    return jnp.concatenate([x1 * c - x2 * s, x2 * c + x1 * s], axis=-1)


def diff_attention(q, k, v, lam):
    B, H = q.shape[0], q.shape[1]
    nq = SEQ // ATTN_Q_BLOCK
    scale = DIFF_HALF ** -0.5
    qb = jnp.moveaxis(q.reshape(B, H, 2, nq, ATTN_Q_BLOCK, DIFF_HALF), 3, 0)
    kpos = jnp.arange(SEQ)

    def block(args):
        q_blk, i = args
        s = jnp.einsum('bhmqd,bhmkd->bhmqk', q_blk, k).astype(jnp.float32) * scale
        qpos = i * ATTN_Q_BLOCK + jnp.arange(ATTN_Q_BLOCK)
        s = jnp.where(kpos[None, :] <= qpos[:, None], s, -jnp.inf)
        p = jax.nn.softmax(s, axis=-1)
        w = p[:, :, 0] - lam * p[:, :, 1]
        return jnp.einsum('bhqk,bhkd->bhqd', w.astype(v.dtype), v)

    out = lax.map(block, (qb, jnp.arange(nq)))
    return jnp.moveaxis(out, 0, 2).reshape(B, H, SEQ, HEAD_DIM)


def moba_attention(q, k, v):
    B, H, S, D = q.shape
    nb = -(-S // MOBA_BLOCK)
    pad = nb * MOBA_BLOCK - S
    kp = jnp.pad(k, ((0, 0), (0, 0), (0, pad), (0, 0)))
    vp = jnp.pad(v, ((0, 0), (0, 0), (0, pad), (0, 0)))
    kb = kp.reshape(B, H, nb, MOBA_BLOCK, D)
    vb = vp.reshape(B, H, nb, MOBA_BLOCK, D)
    kbar = jnp.mean(kb.astype(jnp.float32), axis=3)
    qblk = jnp.arange(S) // MOBA_BLOCK
    gate = jnp.einsum('bhsd,bhnd->bhsn', q.astype(jnp.float32), kbar)
    past = jnp.arange(nb)[None, :] < qblk[:, None]
    gate = jnp.where(past, gate, -jnp.inf)
    kk = min(MOBA_TOPK, nb)
    _, idx = lax.top_k(gate, kk)

    nc = S // MOBA_Q_CHUNK
    qc = jnp.moveaxis(q.reshape(B, H, nc, MOBA_Q_CHUNK, D), 2, 0)
    ic = jnp.moveaxis(idx.reshape(B, H, nc, MOBA_Q_CHUNK, kk), 2, 0)
    bi = jnp.arange(B)[:, None, None, None]
    hi = jnp.arange(H)[None, :, None, None]
    scale = D ** -0.5
    offs = jnp.arange(MOBA_BLOCK)

    def chunk(args):
        q_c, idx_c, c = args
        q0 = c * MOBA_Q_CHUNK
        j = q0 // MOBA_BLOCK
        k_sel = kb[bi, hi, idx_c]
        v_sel = vb[bi, hi, idx_c]
        s_sel = jnp.einsum('bhqd,bhqnkd->bhqnk', q_c, k_sel).astype(jnp.float32) * scale
        valid = jnp.arange(kk) < j
        s_sel = jnp.where(valid[None, None, None, :, None], s_sel, -jnp.inf)
        k_own = lax.dynamic_slice_in_dim(kp, j * MOBA_BLOCK, MOBA_BLOCK, axis=2)
        v_own = lax.dynamic_slice_in_dim(vp, j * MOBA_BLOCK, MOBA_BLOCK, axis=2)
        s_own = jnp.einsum('bhqd,bhkd->bhqk', q_c, k_own).astype(jnp.float32) * scale
        qpos = q0 + jnp.arange(MOBA_Q_CHUNK)
        kpos = j * MOBA_BLOCK + offs
        s_own = jnp.where(kpos[None, :] <= qpos[:, None], s_own, -jnp.inf)
        s_all = jnp.concatenate([s_sel.reshape(B, H, MOBA_Q_CHUNK, kk * MOBA_BLOCK), s_own], axis=-1)
        p = jax.nn.softmax(s_all, axis=-1).astype(v.dtype)
        p_sel = p[..., :kk * MOBA_BLOCK].reshape(B, H, MOBA_Q_CHUNK, kk, MOBA_BLOCK)
        p_own = p[..., kk * MOBA_BLOCK:]
        return (jnp.einsum('bhqnk,bhqnkd->bhqd', p_sel, v_sel)
                + jnp.einsum('bhqk,bhkd->bhqd', p_own, v_own))

    out = lax.map(chunk, (qc, ic, jnp.arange(nc)))
    return jnp.moveaxis(out, 0, 2).reshape(B, H, S, D)


def even_mixer(h, w_in, w_out, qn_g, kn_g, lq1, lk1, lq2, lk2, subln_g, mqn_g, mkn_g, lambda_init):
    B, S, _ = h.shape
    nd = N_HEADS_DIFF * HEAD_DIM
    nm = N_HEADS_MOBA * HEAD_DIM
    proj = h @ w_in
    dq, dk, dv, mq, mk, mv = jnp.split(proj, [nd, 2 * nd, 3 * nd, 3 * nd + nm, 3 * nd + 2 * nm], axis=-1)
    cos32, sin32 = rope_tables(DIFF_HALF)
    dq = dq.reshape(B, S, N_HEADS_DIFF, 2, DIFF_HALF).transpose(0, 2, 3, 1, 4)
    dk = dk.reshape(B, S, N_HEADS_DIFF, 2, DIFF_HALF).transpose(0, 2, 3, 1, 4)
    dq = apply_rope(rms_norm(dq, qn_g), cos32, sin32)
    dk = apply_rope(rms_norm(dk, kn_g), cos32, sin32)
    dv = dv.reshape(B, S, N_HEADS_DIFF, HEAD_DIM).transpose(0, 2, 1, 3)
    f32 = jnp.float32
    lam = (jnp.exp(jnp.sum(lq1.astype(f32) * lk1.astype(f32)))
           - jnp.exp(jnp.sum(lq2.astype(f32) * lk2.astype(f32))) + lambda_init)
    a = diff_attention(dq, dk, dv, lam)
    a = rms_norm(a, subln_g) * (1.0 - lambda_init)
    cos64, sin64 = rope_tables(HEAD_DIM)
    mq = mq.reshape(B, S, N_HEADS_MOBA, HEAD_DIM).transpose(0, 2, 1, 3)
    mk = mk.reshape(B, S, N_HEADS_MOBA, HEAD_DIM).transpose(0, 2, 1, 3)
    mv = mv.reshape(B, S, N_HEADS_MOBA, HEAD_DIM).transpose(0, 2, 1, 3)
    mq = apply_rope(rms_norm(mq, mqn_g), cos64, sin64)
    mk = apply_rope(rms_norm(mk, mkn_g), cos64, sin64)
    m = moba_attention(mq, mk, mv)
    out = jnp.concatenate([a, m], axis=1)
    out = out.transpose(0, 2, 1, 3).reshape(B, S, EVEN_OUT)
    return out @ w_out


def odd_mixer(h, w_in, b_in, w_out, ln_g, ln_b, w_s, b_s, conv_w, conv_b, cln_g, cln_b):
    B, S, _ = h.shape
    proj = h @ w_in + b_in
    gz, ca, cg = jnp.split(proj, [2 * GMLP_WIDTH, 2 * GMLP_WIDTH + CONV_CH], axis=-1)
    gu, gv = jnp.split(jax.nn.gelu(gz), 2, axis=-1)
    gv = layer_norm(gv, ln_g, ln_b)
    nc = S // GMLP_CHUNK
    gvc = gv.reshape(B, nc, GMLP_CHUNK, GMLP_GROUPS, GMLP_WIDTH // GMLP_GROUPS)
    tri = jnp.tril(jnp.ones((GMLP_CHUNK, GMLP_CHUNK), dtype=bool))
    wm = jnp.where(tri[None], w_s, jnp.zeros_like(w_s))
    sg = jnp.einsum('gts,bcsgd->bctgd', wm, gvc) + b_s.T[None, None, :, :, None]
    c_out = gu * sg.reshape(B, S, GMLP_WIDTH)
    c = ca * jax.nn.sigmoid(cg)
    c = lax.conv_general_dilated(c, conv_w[:, None, :], window_strides=(1,),
                                 padding=[(CONV_KERNEL - 1, 0)],
                                 dimension_numbers=('NWC', 'WIO', 'NWC'),
                                 feature_group_count=CONV_CH) + conv_b
    d_out = jax.nn.silu(layer_norm(c, cln_g, cln_b))
    return jnp.concatenate([c_out, d_out], axis=-1) @ w_out


def swiglu(h, w_in, w_out):
    g, u = jnp.split(h @ w_in, 2, axis=-1)
    return (jax.nn.silu(g) * u) @ w_out


def setup_inputs(seed: int = 0) -> dict:
    key = jax.random.key(seed)
    ks = jax.random.split(key, 32)

    def nrm(k, shape, scale):
        return jax.random.normal(k, shape, jnp.float32) * scale

    return {
        "x": nrm(ks[0], (BATCH, SEQ, D_MODEL), 1.0),
        "attn_norm_g": 1.0 + nrm(ks[1], (DEPTH, D_MODEL), 0.02),
        "ffn_norm_g": 1.0 + nrm(ks[2], (DEPTH, D_MODEL), 0.02),
        "ffn_w_in": nrm(ks[3], (DEPTH, D_MODEL, 2 * D_FF), D_MODEL ** -0.5),
        "ffn_w_out": nrm(ks[4], (DEPTH, D_FF, D_MODEL), D_FF ** -0.5),
        "even_w_in": nrm(ks[5], (N_EVEN, D_MODEL, EVEN_IN), D_MODEL ** -0.5),
        "even_w_out": nrm(ks[6], (N_EVEN, EVEN_OUT, D_MODEL), EVEN_OUT ** -0.5),
        "diff_q_norm_g": 1.0 + nrm(ks[7], (N_EVEN, DIFF_HALF), 0.02),
        "diff_k_norm_g": 1.0 + nrm(ks[8], (N_EVEN, DIFF_HALF), 0.02),
        "diff_lambda_q1": nrm(ks[9], (N_EVEN, DIFF_HALF), 0.1),
        "diff_lambda_k1": nrm(ks[10], (N_EVEN, DIFF_HALF), 0.1),
        "diff_lambda_q2": nrm(ks[11], (N_EVEN, DIFF_HALF), 0.1),
        "diff_lambda_k2": nrm(ks[12], (N_EVEN, DIFF_HALF), 0.1),
        "diff_subln_g": 1.0 + nrm(ks[13], (N_EVEN, HEAD_DIM), 0.02),
        "moba_q_norm_g": 1.0 + nrm(ks[14], (N_EVEN, HEAD_DIM), 0.02),
        "moba_k_norm_g": 1.0 + nrm(ks[15], (N_EVEN, HEAD_DIM), 0.02),
        "odd_w_in": nrm(ks[16], (N_ODD, D_MODEL, ODD_IN), D_MODEL ** -0.5),
        "odd_b_in": nrm(ks[17], (N_ODD, ODD_IN), 0.02),
        "odd_w_out": nrm(ks[18], (N_ODD, ODD_OUT, D_MODEL), ODD_OUT ** -0.5),
        "gmlp_ln_g": 1.0 + nrm(ks[19], (N_ODD, GMLP_WIDTH), 0.02),
        "gmlp_ln_b": nrm(ks[20], (N_ODD, GMLP_WIDTH), 0.02),
        "gmlp_w_s": nrm(ks[21], (N_ODD, GMLP_GROUPS, GMLP_CHUNK, GMLP_CHUNK), GMLP_CHUNK ** -0.5),
        "gmlp_b_s": 1.0 + nrm(ks[22], (N_ODD, GMLP_GROUPS, GMLP_CHUNK), 0.02),
        "conv_w": nrm(ks[23], (N_ODD, CONV_KERNEL, CONV_CH), CONV_KERNEL ** -0.5),
        "conv_b": nrm(ks[24], (N_ODD, CONV_CH), 0.02),
        "conv_ln_g": 1.0 + nrm(ks[25], (N_ODD, CONV_CH), 0.02),
        "conv_ln_b": nrm(ks[26], (N_ODD, CONV_CH), 0.02),
    }


def reference(x, attn_norm_g, ffn_norm_g, ffn_w_in, ffn_w_out, even_w_in, even_w_out,
              diff_q_norm_g, diff_k_norm_g, diff_lambda_q1, diff_lambda_k1,
              diff_lambda_q2, diff_lambda_k2, diff_subln_g, moba_q_norm_g, moba_k_norm_g,
              odd_w_in, odd_b_in, odd_w_out, gmlp_ln_g, gmlp_ln_b, gmlp_w_s, gmlp_b_s,
              conv_w, conv_b, conv_ln_g, conv_ln_b):
    for i in range(DEPTH):
        h = rms_norm(x, attn_norm_g[i])
        if i % 2 == 0:
            e = i // 2
            lambda_init = 0.8 - 0.6 * math.exp(-0.3 * i)
            mix = even_mixer(h, even_w_in[e], even_w_out[e], diff_q_norm_g[e], diff_k_norm_g[e],
                             diff_lambda_q1[e], diff_lambda_k1[e], diff_lambda_q2[e], diff_lambda_k2[e],
                             diff_subln_g[e], moba_q_norm_g[e], moba_k_norm_g[e], lambda_init)
        else:
            o = i // 2
            mix = odd_mixer(h, odd_w_in[o], odd_b_in[o], odd_w_out[o], gmlp_ln_g[o], gmlp_ln_b[o],
                            gmlp_w_s[o], gmlp_b_s[o], conv_w[o], conv_b[o], conv_ln_g[o], conv_ln_b[o])
        x = x + mix
        x = x + swiglu(rms_norm(x, ffn_norm_g[i]), ffn_w_in[i], ffn_w_out[i])
    return x
```

```python
import functools
import math

import jax
import jax.numpy as jnp
from jax import lax
from jax.experimental import pallas as pl
from jax.experimental.pallas import tpu as pltpu

F32 = jnp.float32
BF16 = jnp.bfloat16

LANES = 128
VMEM_LIMIT_BYTES = 56 * 1024 * 1024

HEAD_DIM = 64
DIFF_HALF = 32
N_HEADS_DIFF = 8
N_HEADS_MOBA = 8
MOBA_BLOCK = 256
MOBA_TOPK = 3
GMLP_CHUNK = 128
GMLP_GROUPS = 8
GMLP_WIDTH = 512
CONV_CH = 512
CONV_KERNEL = 31
ROPE_THETA = 10000.0
EPS = 1e-6

SECTION = 512
GATE_SEG = LANES // N_HEADS_MOBA
CONV_HALO = 32
MASKED_BIAS = -1e30
M_INIT = -1e29

_NT = (((1,), (1,)), ((), ()))


def _rms(x, g):
    return x * lax.rsqrt(jnp.mean(x * x, axis=-1, keepdims=True) + EPS) * g


def _layer_norm(x, g, b):
    mu = jnp.mean(x, axis=-1, keepdims=True)
    xc = x - mu
    return xc * lax.rsqrt(jnp.mean(xc * xc, axis=-1, keepdims=True) + EPS) * g + b


def _seg_allreduce(x, op, seg, lane):
    s = 1
    while s < seg:
        up = pltpu.roll(x, LANES - s, 1)
        dn = pltpu.roll(x, s, 1)
        x = op(x, jnp.where((lane & s) == 0, up, dn))
        s *= 2
    return x


def _even_proj_kernel(x_ref, g_ref, w_ref, bdd_ref, bdm_ref, gain_ref,
                      cosd_ref, sind_ref, cosm_ref, sinm_ref,
                      qkv_ref, bias_ref, kbt_ref, *, tm):
    j = pl.program_id(1)
    h = _rms(x_ref[0], g_ref[...]).astype(BF16)
    lane = lax.broadcasted_iota(jnp.int32, (tm, LANES), 1)

    def qk_section(sec, bd_ref, seg, cos_ref, sin_ref):
        y = jnp.dot(h, w_ref[:, sec * SECTION:(sec + 1) * SECTION], preferred_element_type=F32)
        ss = jnp.dot((y * y).astype(BF16), bd_ref[...], preferred_element_type=F32)
        yn = y * lax.rsqrt(ss * (1.0 / seg) + EPS) * gain_ref[sec_row[sec]:sec_row[sec] + 1, :]
        half = seg // 2
        lower = (lane & (seg - 1)) < half
        cos = cos_ref[...]
        sin = sin_ref[...]
        out = []
        for c in range(SECTION // LANES):
            yc = yn[:, c * LANES:(c + 1) * LANES]
            rot = jnp.where(lower, pltpu.roll(yc, LANES - half, 1), pltpu.roll(yc, half, 1))
            out.append(yc * cos + rot * sin)
        return out

    def store(sec, chunks, scale):
        for c, yc in enumerate(chunks):
            lo = sec * SECTION + c * LANES
            qkv_ref[0, :, lo:lo + LANES] = (yc * scale).astype(BF16)

    sec_row = {0: 0, 1: 1, 3: 2, 4: 3}
    store(0, qk_section(0, bdd_ref, DIFF_HALF, cosd_ref, sind_ref), DIFF_HALF ** -0.5)
    store(1, qk_section(1, bdd_ref, DIFF_HALF, cosd_ref, sind_ref), 1.0)
    mq = qk_section(3, bdm_ref, HEAD_DIM, cosm_ref, sinm_ref)
    store(3, mq, HEAD_DIM ** -0.5)
    mk = qk_section(4, bdm_ref, HEAD_DIM, cosm_ref, sinm_ref)
    store(4, mk, 1.0)
    for sec in (2, 5):
        lo = sec * SECTION
        qkv_ref[0, :, lo:lo + SECTION] = jnp.dot(
            h, w_ref[:, lo:lo + SECTION], preferred_element_type=F32).astype(BF16)

    @pl.when(j == 0)
    def _():
        kbt_ref[...] = jnp.zeros_like(kbt_ref)

    mk_full = jnp.concatenate(mk, axis=1)
    kbar = jnp.mean(mk_full, axis=0, keepdims=True)
    head_of_lane = lax.broadcasted_iota(jnp.int32, (1, SECTION), 1) // HEAD_DIM
    for hh in range(N_HEADS_MOBA):
        kbt_ref[pl.ds(hh * GATE_SEG + j, 1), :] = jnp.where(head_of_lane == hh, kbar, 0.0)

    gate = lax.dot_general(jnp.concatenate(mq, axis=1), kbt_ref[...], _NT,
                           precision=lax.Precision.HIGHEST, preferred_element_type=F32)
    blk = lane & (GATE_SEG - 1)
    blk_f = blk.astype(F32)
    neg_inf = jnp.float32(-jnp.inf)
    avail = jnp.where(blk < j, gate, neg_inf)
    chosen = blk == j
    for _ in range(MOBA_TOPK):
        best = _seg_allreduce(avail, jnp.maximum, GATE_SEG, lane)
        cand = jnp.logical_and(avail == best, avail > neg_inf)
        first = _seg_allreduce(jnp.where(cand, blk_f, float(GATE_SEG)), jnp.minimum, GATE_SEG, lane)
        pick = blk_f == first
        chosen = jnp.logical_or(chosen, pick)
        avail = jnp.where(pick, neg_inf, avail)
    bias_ref[0] = jnp.where(chosen, 0.0, MASKED_BIAS).astype(BF16)


def _even_proj(x, norm_g, w_in, bdd, bdm, gains, cosd, sind, cosm, sinm):
    B, S, D = x.shape
    tm = MOBA_BLOCK
    n_out = w_in.shape[1]
    const = lambda b, t: (0, 0)
    tab = pl.BlockSpec((tm, LANES), lambda b, t: (t, 0))
    return pl.pallas_call(
        functools.partial(_even_proj_kernel, tm=tm),
        grid=(B, S // tm),
        in_specs=[
            pl.BlockSpec((1, tm, D), lambda b, t: (b, t, 0)),
            pl.BlockSpec((1, D), const),
            pl.BlockSpec((D, n_out), const),
            pl.BlockSpec((SECTION, SECTION), const),
            pl.BlockSpec((SECTION, SECTION), const),
            pl.BlockSpec((4, SECTION), const),
            tab, tab, tab, tab,
        ],
        out_specs=[
            pl.BlockSpec((1, tm, n_out), lambda b, t: (b, t, 0)),
            pl.BlockSpec((1, tm, LANES), lambda b, t: (b, t, 0)),
        ],
        out_shape=[
            jax.ShapeDtypeStruct((B, S, n_out), BF16),
            jax.ShapeDtypeStruct((B, S, LANES), BF16),
        ],
        scratch_shapes=[pltpu.VMEM((LANES, SECTION), F32)],
        compiler_params=pltpu.CompilerParams(
            dimension_semantics=("arbitrary", "arbitrary"),
            vmem_limit_bytes=VMEM_LIMIT_BYTES),
        name="even_proj",
    )(x, norm_g, w_in, bdd, bdm, gains, cosd, sind, cosm, sinm)


def _flash_rows(q_rows, k_ref, v_ref, qi, t, k_aug_ref=None):
    R = q_rows.shape[0]

    def tile(j, carry, diagonal):
        m, l, acc = carry
        start = pl.multiple_of(j * t, t)
        k = k_ref[0, pl.ds(start, t), :]
        if k_aug_ref is not None:
            k = jnp.concatenate([k, k_aug_ref[pl.ds(start, t), :]], axis=1)
        v = v_ref[0, pl.ds(start, t), :]
        s = lax.dot_general(q_rows, k, _NT, preferred_element_type=F32)
        if diagonal:
            row = lax.broadcasted_iota(jnp.int32, (R, t), 0) & (t - 1)
            col = lax.broadcasted_iota(jnp.int32, (R, t), 1)
            s = jnp.where(col <= row, s, -jnp.inf)
        m_new = jnp.maximum(m, jnp.max(s, axis=1, keepdims=True))
        p = jnp.exp(s - m_new)
        alpha = jnp.exp(m - m_new)
        l = alpha * l + jnp.sum(p, axis=1, keepdims=True)
        acc = alpha * acc + jnp.dot(p.astype(BF16), v, preferred_element_type=F32)
        return m_new, l, acc

    init = (jnp.full((R, 1), M_INIT, F32), jnp.zeros((R, 1), F32), jnp.zeros((R, LANES), F32))
    carry = lax.fori_loop(0, qi, lambda j, c: tile(j, c, False), init)
    _, l, acc = tile(qi, carry, True)
    return acc, l


def _diff_attn_kernel(lam_ref, q_ref, k_ref, v_ref, sg_ref, o_ref, *, t, lambda_init):
    qi = pl.program_id(2)
    lane = lax.broadcasted_iota(jnp.int32, (t, LANES), 1)
    q = q_ref[0]
    seg = lane // DIFF_HALF
    q4 = jnp.concatenate([jnp.where(seg == i, q, jnp.zeros_like(q)) for i in range(4)], axis=0)
    acc, l = _flash_rows(q4, k_ref, v_ref, qi, t)
    a = acc * (1.0 / l)
    lam_p = lam_ref[...]
    lam = (jnp.exp(jnp.sum(lam_p[0:1] * lam_p[1:2], axis=1, keepdims=True))
           - jnp.exp(jnp.sum(lam_p[2:3] * lam_p[3:4], axis=1, keepdims=True)) + lambda_init)
    first = lane < HEAD_DIM
    o = jnp.where(first, a[0:t] - lam * a[t:2 * t], a[2 * t:3 * t] - lam * a[3 * t:4 * t])
    o2 = o * o
    ms0 = jnp.sum(jnp.where(first, o2, 0.0), axis=1, keepdims=True)
    ms1 = jnp.sum(jnp.where(first, 0.0, o2), axis=1, keepdims=True)
    ms = jnp.where(first, ms0, ms1) * (1.0 / HEAD_DIM)
    o_ref[0] = (o * lax.rsqrt(ms + EPS) * sg_ref[...] * (1.0 - lambda_init)).astype(BF16)


def _moba_attn_kernel(q_ref, bias_ref, k_ref, v_ref, ind_ref, o_ref, *, t):
    hp = pl.program_id(1)
    qi = pl.program_id(2)
    lane = lax.broadcasted_iota(jnp.int32, (t, LANES), 1)
    q = q_ref[0]
    bias = bias_ref[0]
    rows = []
    for hh in range(2):
        qz = jnp.where(lane // HEAD_DIM == hh, q, jnp.zeros_like(q))
        bz = jnp.where(lane // GATE_SEG == 2 * hp + hh, bias, jnp.zeros_like(bias))
        rows.append(jnp.concatenate([qz, bz], axis=1))
    q2 = jnp.concatenate(rows, axis=0)
    acc, l = _flash_rows(q2, k_ref, v_ref, qi, t, k_aug_ref=ind_ref)
    a = acc * (1.0 / l)
    o_ref[0] = jnp.where(lane < HEAD_DIM, a[0:t], a[t:2 * t]).astype(BF16)


def _attention(qkv, bias, ind, lam_params, subln_tile, lambda_init):
    B, S, _ = qkv.shape
    t = MOBA_BLOCK
    n_pairs = SECTION // LANES
    grid = (B, n_pairs, S // t)
    q_spec = lambda sec: pl.BlockSpec((1, t, LANES), lambda b, p, i: (b, i, sec * n_pairs + p))
    kv_spec = lambda sec: pl.BlockSpec((1, S, LANES), lambda b, p, i: (b, 0, sec * n_pairs + p))
    out_spec = pl.BlockSpec((1, t, LANES), lambda b, p, i: (b, i, p))
    params = pltpu.CompilerParams(
        dimension_semantics=("arbitrary", "arbitrary", "arbitrary"),
        vmem_limit_bytes=VMEM_LIMIT_BYTES)
    out_shape = jax.ShapeDtypeStruct((B, S, SECTION), BF16)

    diff = pl.pallas_call(
        functools.partial(_diff_attn_kernel, t=t, lambda_init=lambda_init),
        grid=grid,
        in_specs=[pl.BlockSpec((4, DIFF_HALF), lambda b, p, i: (0, 0)),
                  q_spec(0), kv_spec(1), kv_spec(2),
                  pl.BlockSpec((1, LANES), lambda b, p, i: (0, 0))],
        out_specs=out_spec, out_shape=out_shape, compiler_params=params,
        name="diff_attn",
    )(lam_params, qkv, qkv, qkv, subln_tile)

    moba = pl.pallas_call(
        functools.partial(_moba_attn_kernel, t=t),
        grid=grid,
        in_specs=[q_spec(3),
                  pl.BlockSpec((1, t, LANES), lambda b, p, i: (b, i, 0)),
                  kv_spec(4), kv_spec(5),
                  pl.BlockSpec((S, LANES), lambda b, p, i: (0, 0))],
        out_specs=out_spec, out_shape=out_shape, compiler_params=params,
        name="moba_attn",
    )(qkv, bias, qkv, qkv, ind)
    return jnp.concatenate([diff, moba], axis=-1)


def _out_ffn_kernel(mix_ref, x_ref, wo_ref, g_ref, wi_ref, w2_ref, o_ref, *, d_ff, chunk):
    x1 = x_ref[0] + jnp.dot(mix_ref[0], wo_ref[...], preferred_element_type=F32)
    hn = _rms(x1, g_ref[...]).astype(BF16)
    acc = x1
    for c in range(d_ff // chunk):
        lo = c * chunk
        g = jnp.dot(hn, wi_ref[:, lo:lo + chunk], preferred_element_type=F32)
        u = jnp.dot(hn, wi_ref[:, d_ff + lo:d_ff + lo + chunk], preferred_element_type=F32)
        a = (g * jax.nn.sigmoid(g) * u).astype(BF16)
        acc = acc + jnp.dot(a, w2_ref[lo:lo + chunk, :], preferred_element_type=F32)
    o_ref[0] = acc


def _out_ffn(mix, x, w_out, norm_g, w_in, w2):
    B, S, D = x.shape
    d_ff = w2.shape[0]
    tm = 512
    const = lambda b, t: (0, 0)
    resident = lambda shape: pl.BlockSpec(shape, const, pipeline_mode=pl.Buffered(1))
    tile = pl.BlockSpec((1, tm, D), lambda b, t: (b, t, 0))
    return pl.pallas_call(
        functools.partial(_out_ffn_kernel, d_ff=d_ff, chunk=256),
        grid=(B, S // tm),
        in_specs=[tile, tile, resident(w_out.shape), pl.BlockSpec((1, D), const),
                  resident(w_in.shape), resident(w2.shape)],
        out_specs=tile,
        out_shape=jax.ShapeDtypeStruct((B, S, D), F32),
        compiler_params=pltpu.CompilerParams(
            dimension_semantics=("arbitrary", "arbitrary"),
            vmem_limit_bytes=VMEM_LIMIT_BYTES),
        name="out_ffn",
    )(mix, x, w_out, norm_g, w_in, w2)


def _odd_mix_kernel(x_ref, g_ref, w_ref, b_ref, lng_ref, lnb_ref, ws_ref, bs_ref,
                    cw_ref, cb_ref, clg_ref, clb_ref, o_ref, ext_ref, *, tm):
    ti = pl.program_id(1)
    h = _rms(x_ref[0], g_ref[...]).astype(BF16)

    def proj(lo, width):
        return jnp.dot(h, w_ref[:, lo:lo + width], preferred_element_type=F32) + b_ref[:, lo:lo + width]

    gu = jax.nn.gelu(proj(0, GMLP_WIDTH), approximate=True)
    gv = jax.nn.gelu(proj(GMLP_WIDTH, GMLP_WIDTH), approximate=True)
    gv = _layer_norm(gv, lng_ref[...], lnb_ref[...]).astype(BF16)
    T = GMLP_CHUNK
    tri = lax.broadcasted_iota(jnp.int32, (T, T), 1) <= lax.broadcasted_iota(jnp.int32, (T, T), 0)
    lane = lax.broadcasted_iota(jnp.int32, (T, LANES), 1)
    first = lane < (GMLP_WIDTH // GMLP_GROUPS)
    zero = jnp.zeros((T, LANES), BF16)
    for lb in range(GMLP_WIDTH // LANES):
        wpair = jnp.concatenate(
            [jnp.where(tri, ws_ref[2 * lb + i], 0.0).astype(BF16) for i in range(2)], axis=1)
        bs = bs_ref[:, lb * LANES:(lb + 1) * LANES]
        for c in range(tm // T):
            vc = gv[c * T:(c + 1) * T, lb * LANES:(lb + 1) * LANES]
            vpair = jnp.concatenate([jnp.where(first, vc, zero), jnp.where(first, zero, vc)], axis=0)
            sg = jnp.dot(wpair, vpair, preferred_element_type=F32) + bs
            o_ref[0, c * T:(c + 1) * T, lb * LANES:(lb + 1) * LANES] = (
                gu[c * T:(c + 1) * T, lb * LANES:(lb + 1) * LANES] * sg).astype(BF16)

    lo = 2 * GMLP_WIDTH
    cval = proj(lo, CONV_CH) * jax.nn.sigmoid(proj(lo + CONV_CH, CONV_CH))

    @pl.when(ti == 0)
    def _():
        ext_ref[0:CONV_HALO, :] = jnp.zeros((CONV_HALO, CONV_CH), F32)

    ext_ref[CONV_HALO:CONV_HALO + tm, :] = cval
    first_tap = CONV_HALO - (CONV_KERNEL - 1)
    span = tm + 8 * ((CONV_KERNEL - 1) // 8)
    conv = jnp.zeros((tm, CONV_CH), F32) + cb_ref[...]
    for r in range(8):
        taps = [k for k in range(CONV_KERNEL) if (first_tap + k) % 8 == r]
        base = first_tap + taps[0]
        rows = ext_ref[base:base + span, :]
        for k in taps:
            off = first_tap + k - base
            conv = conv + rows[off:off + tm, :] * cw_ref[k:k + 1, :]
    ext_ref[0:CONV_HALO, :] = ext_ref[tm:tm + CONV_HALO, :]
    dn = _layer_norm(conv, clg_ref[...], clb_ref[...])
    o_ref[0, :, GMLP_WIDTH:GMLP_WIDTH + CONV_CH] = (dn * jax.nn.sigmoid(dn)).astype(BF16)


def _odd_mix(x, norm_g, w_in, b_in, ln_g, ln_b, w_s, bs_tile, conv_w, conv_b, cln_g, cln_b):
    B, S, D = x.shape
    tm = 512
    const2 = lambda b, t: (0, 0)
    full = lambda a: pl.BlockSpec(a.shape, (lambda b, t: (0,) * a.ndim))
    n_out = GMLP_WIDTH + CONV_CH
    span = tm + 8 * ((CONV_KERNEL - 1) // 8)
    args = (x, norm_g, w_in, b_in, ln_g, ln_b, w_s, bs_tile, conv_w, conv_b, cln_g, cln_b)
    return pl.pallas_call(
        functools.partial(_odd_mix_kernel, tm=tm),
        grid=(B, S // tm),
        in_specs=[pl.BlockSpec((1, tm, D), lambda b, t: (b, t, 0))] + [full(a) for a in args[1:]],
        out_specs=pl.BlockSpec((1, tm, n_out), lambda b, t: (b, t, 0)),
        out_shape=jax.ShapeDtypeStruct((B, S, n_out), BF16),
        scratch_shapes=[pltpu.VMEM((span + 8, CONV_CH), F32)],
        compiler_params=pltpu.CompilerParams(
            dimension_semantics=("arbitrary", "arbitrary"),
            vmem_limit_bytes=VMEM_LIMIT_BYTES),
        name="odd_mix",
    )(*args)


def _rope_tables(seq, dim, seg_per_block):
    inv = 1.0 / (ROPE_THETA ** (jnp.arange(0, dim, 2, dtype=F32) / dim))
    ang = jnp.arange(seq, dtype=F32)[:, None] * inv[None, :]
    cos, sin = jnp.cos(ang), jnp.sin(ang)
    cos = jnp.tile(jnp.concatenate([cos, cos], axis=1), (1, seg_per_block))
    sin = jnp.tile(jnp.concatenate([-sin, sin], axis=1), (1, seg_per_block))
    return cos, sin


def _block_diag_ones(seg):
    idx = jnp.arange(SECTION) // seg
    return (idx[:, None] == idx[None, :]).astype(BF16)


def kernel(x, attn_norm_g, ffn_norm_g, ffn_w_in, ffn_w_out, even_w_in, even_w_out,
           diff_q_norm_g, diff_k_norm_g, diff_lambda_q1, diff_lambda_k1,
           diff_lambda_q2, diff_lambda_k2, diff_subln_g, moba_q_norm_g, moba_k_norm_g,
           odd_w_in, odd_b_in, odd_w_out, gmlp_ln_g, gmlp_ln_b, gmlp_w_s, gmlp_b_s,
           conv_w, conv_b, conv_ln_g, conv_ln_b):
    B, S, D = x.shape
    assert S % 512 == 0 and S // MOBA_BLOCK <= GATE_SEG
    row = lambda v: v.reshape(1, -1).astype(F32)

    lambda_init = 0.8 - 0.6 * math.exp(-0.3 * 0)
    cosd, sind = _rope_tables(S, DIFF_HALF, LANES // DIFF_HALF)
    cosm, sinm = _rope_tables(S, HEAD_DIM, LANES // HEAD_DIM)
    gains = jnp.stack([
        jnp.tile(diff_q_norm_g[0], SECTION // DIFF_HALF),
        jnp.tile(diff_k_norm_g[0], SECTION // DIFF_HALF),
        jnp.tile(moba_q_norm_g[0], SECTION // HEAD_DIM),
        jnp.tile(moba_k_norm_g[0], SECTION // HEAD_DIM)]).astype(F32)
    qkv, bias = _even_proj(x, row(attn_norm_g[0]), even_w_in[0].astype(BF16),
                           _block_diag_ones(DIFF_HALF), _block_diag_ones(HEAD_DIM), gains,
                           cosd, sind, cosm, sinm)
    key_block = jnp.arange(S)[:, None] // MOBA_BLOCK
    ind = (key_block == (jnp.arange(LANES)[None, :] % GATE_SEG)).astype(BF16)
    lam_params = jnp.stack([diff_lambda_q1[0], diff_lambda_k1[0],
                            diff_lambda_q2[0], diff_lambda_k2[0]]).astype(F32)
    subln_tile = jnp.tile(diff_subln_g[0], LANES // HEAD_DIM).reshape(1, LANES).astype(F32)
    mix = _attention(qkv, bias, ind, lam_params, subln_tile, lambda_init)
    x = _out_ffn(mix, x, even_w_out[0].astype(BF16), row(ffn_norm_g[0]),
                 ffn_w_in[0].astype(BF16), ffn_w_out[0].astype(BF16))

    bs_tile = jnp.repeat(gmlp_b_s[0].T, GMLP_WIDTH // GMLP_GROUPS, axis=1).astype(F32)
    mix = _odd_mix(x, row(attn_norm_g[1]), odd_w_in[0].astype(BF16), row(odd_b_in[0]),
                   row(gmlp_ln_g[0]), row(gmlp_ln_b[0]), gmlp_w_s[0].astype(F32), bs_tile,
                   conv_w[0].astype(F32), row(conv_b[0]), row(conv_ln_g[0]), row(conv_ln_b[0]))
    x = _out_ffn(mix, x, odd_w_out[0].astype(BF16), row(ffn_norm_g[1]),
                 ffn_w_in[1].astype(BF16), ffn_w_out[1].astype(BF16))
    return x
```

```python
import functools
import math

import jax
import jax.numpy as jnp
from jax import lax
from jax.experimental import pallas as pl
from jax.experimental.pallas import tpu as pltpu

F32 = jnp.float32
BF16 = jnp.bfloat16

LANES = 128
VMEM_LIMIT_BYTES = 56 * 1024 * 1024

HEAD_DIM = 64
DIFF_HALF = 32
N_HEADS_DIFF = 8
N_HEADS_MOBA = 8
MOBA_BLOCK = 256
MOBA_TOPK = 3
GMLP_CHUNK = 128
GMLP_GROUPS = 8
GMLP_WIDTH = 512
CONV_CH = 512
CONV_KERNEL = 31
ROPE_THETA = 10000.0
EPS = 1e-6

SECTION = 512
GATE_SEG = LANES // N_HEADS_MOBA
CONV_HALO = 32
MASKED_BIAS = -1e30
M_INIT = -1e29
LOG2E = math.log2(math.e)

_NT = (((1,), (1,)), ((), ()))


def _rms(x, g):
    return x * lax.rsqrt(jnp.mean(x * x, axis=-1, keepdims=True) + EPS) * g


def _layer_norm(x, g, b):
    mu = jnp.mean(x, axis=-1, keepdims=True)
    xc = x - mu
    return xc * lax.rsqrt(jnp.mean(xc * xc, axis=-1, keepdims=True) + EPS) * g + b


def _seg_allreduce(x, op, seg, lane):
    s = 1
    while s < seg:
        up = pltpu.roll(x, LANES - s, 1)
        dn = pltpu.roll(x, s, 1)
        x = op(x, jnp.where((lane & s) == 0, up, dn))
        s *= 2
    return x


def _even_proj_kernel(x_ref, g_ref, w_ref, bdd_ref, bdm_ref, gain_ref,
                      cosd_ref, sind_ref, cosm_ref, sinm_ref,
                      qk_ref, vt_ref, bias_ref, kbt_ref, *, tm):
    j = pl.program_id(1)
    h = _rms(x_ref[0], g_ref[...]).astype(BF16)
    lane = lax.broadcasted_iota(jnp.int32, (tm, LANES), 1)

    def qk_section(sec, bd_ref, seg, cos_ref, sin_ref):
        y = jnp.dot(h, w_ref[:, sec * SECTION:(sec + 1) * SECTION], preferred_element_type=F32)
        ss = jnp.dot((y * y).astype(BF16), bd_ref[...], preferred_element_type=F32)
        yn = y * lax.rsqrt(ss * (1.0 / seg) + EPS) * gain_ref[sec_row[sec]:sec_row[sec] + 1, :]
        half = seg // 2
        lower = (lane & (seg - 1)) < half
        cos = cos_ref[...]
        sin = sin_ref[...]
        out = []
        for c in range(SECTION // LANES):
            yc = yn[:, c * LANES:(c + 1) * LANES]
            rot = jnp.where(lower, pltpu.roll(yc, LANES - half, 1), pltpu.roll(yc, half, 1))
            out.append(yc * cos + rot * sin)
        return out

    def store(out_sec, chunks, scale):
        for c, yc in enumerate(chunks):
            lo = out_sec * SECTION + c * LANES
            qk_ref[0, :, lo:lo + LANES] = (yc * scale).astype(BF16)

    sec_row = {0: 0, 1: 1, 3: 2, 4: 3}
    store(0, qk_section(0, bdd_ref, DIFF_HALF, cosd_ref, sind_ref), DIFF_HALF ** -0.5 * LOG2E)
    store(1, qk_section(1, bdd_ref, DIFF_HALF, cosd_ref, sind_ref), 1.0)
    mq = qk_section(3, bdm_ref, HEAD_DIM, cosm_ref, sinm_ref)
    store(2, mq, HEAD_DIM ** -0.5 * LOG2E)
    mk = qk_section(4, bdm_ref, HEAD_DIM, cosm_ref, sinm_ref)
    store(3, mk, 1.0)
    for out_sec, sec in enumerate((2, 5)):
        lo = sec * SECTION
        v = jnp.dot(h, w_ref[:, lo:lo + SECTION], preferred_element_type=F32)
        vt_ref[0, 0, out_sec * SECTION:(out_sec + 1) * SECTION, :] = v.T.astype(BF16)

    @pl.when(j == 0)
    def _():
        kbt_ref[...] = jnp.zeros_like(kbt_ref)

    mk_full = jnp.concatenate(mk, axis=1)
    kbar = jnp.mean(mk_full, axis=0, keepdims=True)
    head_of_lane = lax.broadcasted_iota(jnp.int32, (1, SECTION), 1) // HEAD_DIM
    for hh in range(N_HEADS_MOBA):
        kbt_ref[pl.ds(hh * GATE_SEG + j, 1), :] = jnp.where(head_of_lane == hh, kbar, 0.0)

    gate = lax.dot_general(jnp.concatenate(mq, axis=1), kbt_ref[...], _NT,
                           precision=lax.Precision.HIGHEST, preferred_element_type=F32)
    blk = lane & (GATE_SEG - 1)
    blk_f = blk.astype(F32)
    neg_inf = jnp.float32(-jnp.inf)
    avail = jnp.where(blk < j, gate, neg_inf)
    chosen = blk == j
    for _ in range(MOBA_TOPK):
        best = _seg_allreduce(avail, jnp.maximum, GATE_SEG, lane)
        cand = jnp.logical_and(avail == best, avail > neg_inf)
        first = _seg_allreduce(jnp.where(cand, blk_f, float(GATE_SEG)), jnp.minimum, GATE_SEG, lane)
        pick = blk_f == first
        chosen = jnp.logical_or(chosen, pick)
        avail = jnp.where(pick, neg_inf, avail)
    bias_ref[0] = jnp.where(chosen, 0.0, MASKED_BIAS).astype(BF16)


def _even_proj(x, norm_g, w_in, bdd, bdm, gains, cosd, sind, cosm, sinm):
    B, S, D = x.shape
    tm = MOBA_BLOCK
    n_out = w_in.shape[1]
    n_qk = 4 * SECTION
    n_v = 2 * SECTION
    const = lambda b, t: (0, 0)
    tab = pl.BlockSpec((tm, LANES), lambda b, t: (t, 0))
    return pl.pallas_call(
        functools.partial(_even_proj_kernel, tm=tm),
        grid=(B, S // tm),
        in_specs=[
            pl.BlockSpec((1, tm, D), lambda b, t: (b, t, 0)),
            pl.BlockSpec((1, D), const),
            pl.BlockSpec((D, n_out), const),
            pl.BlockSpec((SECTION, SECTION), const),
            pl.BlockSpec((SECTION, SECTION), const),
            pl.BlockSpec((4, SECTION), const),
            tab, tab, tab, tab,
        ],
        out_specs=[
            pl.BlockSpec((1, tm, n_qk), lambda b, t: (b, t, 0)),
            pl.BlockSpec((1, 1, n_v, tm), lambda b, t: (b, t, 0, 0)),
            pl.BlockSpec((1, tm, LANES), lambda b, t: (b, t, 0)),
        ],
        out_shape=[
            jax.ShapeDtypeStruct((B, S, n_qk), BF16),
            jax.ShapeDtypeStruct((B, S // tm, n_v, tm), BF16),
            jax.ShapeDtypeStruct((B, S, LANES), BF16),
        ],
        scratch_shapes=[pltpu.VMEM((LANES, SECTION), F32)],
        compiler_params=pltpu.CompilerParams(
            dimension_semantics=("arbitrary", "arbitrary"),
            vmem_limit_bytes=VMEM_LIMIT_BYTES),
        name="even_proj",
    )(x, norm_g, w_in, bdd, bdm, gains, cosd, sind, cosm, sinm)


def _flash_cols(q_strips, v_rows, k_ref, vt_ref, qi, t, k_aug_ref=None):
    n = len(q_strips)
    q_all = jnp.concatenate(q_strips, axis=0)

    def tile(j, carry, diagonal):
        start = pl.multiple_of(j * t, t)
        k = k_ref[0, pl.ds(start, t), :]
        if k_aug_ref is not None:
            k = jnp.concatenate([k, k_aug_ref[pl.ds(start, t), :]], axis=1)
        vt = vt_ref[0, j]
        s_all = lax.dot_general(k, q_all, _NT, preferred_element_type=F32)
        out = []
        for i in range(n):
            m, l, acc = carry[i]
            s = s_all[:, i * t:(i + 1) * t]
            if diagonal:
                key = lax.broadcasted_iota(jnp.int32, (t, t), 0)
                qry = lax.broadcasted_iota(jnp.int32, (t, t), 1)
                s = jnp.where(key <= qry, s, -jnp.inf)
            m_new = jnp.maximum(m, jnp.max(s, axis=0, keepdims=True))
            p = jnp.exp2(s - m_new)
            alpha = jnp.exp2(m - m_new)
            l = alpha * l + jnp.sum(p, axis=0, keepdims=True)
            lo, hi = v_rows[i]
            acc = alpha * acc + jnp.dot(vt[lo:hi], p.astype(BF16), preferred_element_type=F32)
            out.append((m_new, l, acc))
        return tuple(out)

    init = tuple((jnp.full((1, t), M_INIT, F32), jnp.zeros((1, t), F32),
                  jnp.zeros((v_rows[i][1] - v_rows[i][0], t), F32)) for i in range(n))
    carry = lax.fori_loop(0, qi, lambda j, c: tile(j, c, False), init)
    return [(acc, l) for _, l, acc in tile(qi, carry, True)]


def _diff_attn_kernel(lam_ref, q_ref, k_ref, vt_ref, sg_ref, o_ref, *, t, lambda_init):
    qi = pl.program_id(2)
    lane = lax.broadcasted_iota(jnp.int32, (t, LANES), 1)
    q = q_ref[0]
    seg = lane // DIFF_HALF
    strips = [jnp.where(seg == i, q, jnp.zeros_like(q)) for i in range(4)]
    v_rows = [(HEAD_DIM * (i // 2), HEAD_DIM * (i // 2 + 1)) for i in range(4)]
    a = [acc * (1.0 / l) for acc, l in _flash_cols(strips, v_rows, k_ref, vt_ref, qi, t)]
    lam_p = lam_ref[...]
    lam = (jnp.exp(jnp.sum(lam_p[0:1] * lam_p[1:2], axis=1, keepdims=True))
           - jnp.exp(jnp.sum(lam_p[2:3] * lam_p[3:4], axis=1, keepdims=True)) + lambda_init)
    heads = []
    for hh in range(2):
        o = a[2 * hh] - lam * a[2 * hh + 1]
        heads.append(o * lax.rsqrt(jnp.mean(o * o, axis=0, keepdims=True) + EPS))
    o = jnp.concatenate(heads, axis=0).T
    o_ref[0] = (o * sg_ref[...] * (1.0 - lambda_init)).astype(BF16)


def _moba_attn_kernel(q_ref, bias_ref, k_ref, vt_ref, ind_ref, o_ref, *, t):
    hp = pl.program_id(1)
    qi = pl.program_id(2)
    lane = lax.broadcasted_iota(jnp.int32, (t, LANES), 1)
    q = q_ref[0]
    bias = bias_ref[0]
    strips = []
    for hh in range(2):
        qz = jnp.where(lane // HEAD_DIM == hh, q, jnp.zeros_like(q))
        bz = jnp.where(lane // GATE_SEG == 2 * hp + hh, bias, jnp.zeros_like(bias))
        strips.append(jnp.concatenate([qz, bz], axis=1))
    v_rows = [(0, HEAD_DIM), (HEAD_DIM, 2 * HEAD_DIM)]
    res = _flash_cols(strips, v_rows, k_ref, vt_ref, qi, t, k_aug_ref=ind_ref)
    o = jnp.concatenate([acc * (1.0 / l) for acc, l in res], axis=0)
    o_ref[0] = o.T.astype(BF16)


def _attention(qk, vt, bias, ind, lam_params, subln_tile, lambda_init):
    B, S, _ = qk.shape
    t = MOBA_BLOCK
    n_pairs = SECTION // LANES
    grid = (B, n_pairs, S // t)
    q_spec = lambda sec: pl.BlockSpec((1, t, LANES), lambda b, p, i: (b, i, sec * n_pairs + p))
    k_spec = lambda sec: pl.BlockSpec((1, S, LANES), lambda b, p, i: (b, 0, sec * n_pairs + p))
    vt_spec = lambda sec: pl.BlockSpec((1, S // t, LANES, t), lambda b, p, i: (b, 0, sec * n_pairs + p, 0))
    out_spec = pl.BlockSpec((1, t, LANES), lambda b, p, i: (b, i, p))
    params = pltpu.CompilerParams(
        dimension_semantics=("arbitrary", "arbitrary", "arbitrary"),
        vmem_limit_bytes=VMEM_LIMIT_BYTES)
    out_shape = jax.ShapeDtypeStruct((B, S, SECTION), BF16)

    diff = pl.pallas_call(
        functools.partial(_diff_attn_kernel, t=t, lambda_init=lambda_init),
        grid=grid,
        in_specs=[pl.BlockSpec((4, DIFF_HALF), lambda b, p, i: (0, 0)),
                  q_spec(0), k_spec(1), vt_spec(0),
                  pl.BlockSpec((1, LANES), lambda b, p, i: (0, 0))],
        out_specs=out_spec, out_shape=out_shape, compiler_params=params,
        name="diff_attn",
    )(lam_params, qk, qk, vt, subln_tile)

    moba = pl.pallas_call(
        functools.partial(_moba_attn_kernel, t=t),
        grid=grid,
        in_specs=[q_spec(2),
                  pl.BlockSpec((1, t, LANES), lambda b, p, i: (b, i, 0)),
                  k_spec(3), vt_spec(1),
                  pl.BlockSpec((S, LANES), lambda b, p, i: (0, 0))],
        out_specs=out_spec, out_shape=out_shape, compiler_params=params,
        name="moba_attn",
    )(qk, bias, qk, vt, ind)
    return jnp.concatenate([diff, moba], axis=-1)


def _out_ffn_kernel(mix_ref, x_ref, wo_ref, g_ref, wi_ref, w2_ref, o_ref, *, d_ff, chunk):
    x1 = x_ref[0] + jnp.dot(mix_ref[0], wo_ref[...], preferred_element_type=F32)
    hn = _rms(x1, g_ref[...]).astype(BF16)
    acc = x1
    for c in range(d_ff // chunk):
        lo = c * chunk
        g = jnp.dot(hn, wi_ref[:, lo:lo + chunk], preferred_element_type=F32)
        u = jnp.dot(hn, wi_ref[:, d_ff + lo:d_ff + lo + chunk], preferred_element_type=F32)
        a = (g * jax.nn.sigmoid(g) * u).astype(BF16)
        acc = acc + jnp.dot(a, w2_ref[lo:lo + chunk, :], preferred_element_type=F32)
    o_ref[0] = acc


def _out_ffn(mix, x, w_out, norm_g, w_in, w2):
    B, S, D = x.shape
    d_ff = w2.shape[0]
    tm = 512
    const = lambda b, t: (0, 0)
    resident = lambda shape: pl.BlockSpec(shape, const, pipeline_mode=pl.Buffered(1))
    tile = pl.BlockSpec((1, tm, D), lambda b, t: (b, t, 0))
    return pl.pallas_call(
        functools.partial(_out_ffn_kernel, d_ff=d_ff, chunk=256),
        grid=(B, S // tm),
        in_specs=[tile, tile, resident(w_out.shape), pl.BlockSpec((1, D), const),
                  resident(w_in.shape), resident(w2.shape)],
        out_specs=tile,
        out_shape=jax.ShapeDtypeStruct((B, S, D), F32),
        compiler_params=pltpu.CompilerParams(
            dimension_semantics=("arbitrary", "arbitrary"),
            vmem_limit_bytes=VMEM_LIMIT_BYTES),
        name="out_ffn",
    )(mix, x, w_out, norm_g, w_in, w2)


def _odd_mix_kernel(x_ref, g_ref, w_ref, b_ref, lng_ref, lnb_ref, ws_ref, bs_ref,
                    cw_ref, cb_ref, clg_ref, clb_ref, o_ref, ext_ref, *, tm):
    ti = pl.program_id(1)
    h = _rms(x_ref[0], g_ref[...]).astype(BF16)

    def proj(lo, width):
        return jnp.dot(h, w_ref[:, lo:lo + width], preferred_element_type=F32) + b_ref[:, lo:lo + width]

    gu = jax.nn.gelu(proj(0, GMLP_WIDTH), approximate=True)
    gv = jax.nn.gelu(proj(GMLP_WIDTH, GMLP_WIDTH), approximate=True)
    gv = _layer_norm(gv, lng_ref[...], lnb_ref[...]).astype(BF16)
    T = GMLP_CHUNK
    tri = lax.broadcasted_iota(jnp.int32, (T, T), 1) <= lax.broadcasted_iota(jnp.int32, (T, T), 0)
    lane = lax.broadcasted_iota(jnp.int32, (T, LANES), 1)
    first = lane < (GMLP_WIDTH // GMLP_GROUPS)
    zero = jnp.zeros((T, LANES), BF16)
    for lb in range(GMLP_WIDTH // LANES):
        wpair = jnp.concatenate(
            [jnp.where(tri, ws_ref[2 * lb + i], 0.0).astype(BF16) for i in range(2)], axis=1)
        bs = bs_ref[:, lb * LANES:(lb + 1) * LANES]
        for c in range(tm // T):
            vc = gv[c * T:(c + 1) * T, lb * LANES:(lb + 1) * LANES]
            vpair = jnp.concatenate([jnp.where(first, vc, zero), jnp.where(first, zero, vc)], axis=0)
            sg = jnp.dot(wpair, vpair, preferred_element_type=F32) + bs
            o_ref[0, c * T:(c + 1) * T, lb * LANES:(lb + 1) * LANES] = (
                gu[c * T:(c + 1) * T, lb * LANES:(lb + 1) * LANES] * sg).astype(BF16)

    lo = 2 * GMLP_WIDTH
    cval = proj(lo, CONV_CH) * jax.nn.sigmoid(proj(lo + CONV_CH, CONV_CH))

    @pl.when(ti == 0)
    def _():
        ext_ref[0:CONV_HALO, :] = jnp.zeros((CONV_HALO, CONV_CH), F32)

    ext_ref[CONV_HALO:CONV_HALO + tm, :] = cval
    first_tap = CONV_HALO - (CONV_KERNEL - 1)
    span = tm + 8 * ((CONV_KERNEL - 1) // 8)
    conv = jnp.zeros((tm, CONV_CH), F32) + cb_ref[...]
    for r in range(8):
        taps = [k for k in range(CONV_KERNEL) if (first_tap + k) % 8 == r]
        base = first_tap + taps[0]
        rows = ext_ref[base:base + span, :]
        for k in taps:
            off = first_tap + k - base
            conv = conv + rows[off:off + tm, :] * cw_ref[k:k + 1, :]
    ext_ref[0:CONV_HALO, :] = ext_ref[tm:tm + CONV_HALO, :]
    dn = _layer_norm(conv, clg_ref[...], clb_ref[...])
    o_ref[0, :, GMLP_WIDTH:GMLP_WIDTH + CONV_CH] = (dn * jax.nn.sigmoid(dn)).astype(BF16)


def _odd_mix(x, norm_g, w_in, b_in, ln_g, ln_b, w_s, bs_tile, conv_w, conv_b, cln_g, cln_b):
    B, S, D = x.shape
    tm = 512
    const2 = lambda b, t: (0, 0)
    full = lambda a: pl.BlockSpec(a.shape, (lambda b, t: (0,) * a.ndim))
    n_out = GMLP_WIDTH + CONV_CH
    span = tm + 8 * ((CONV_KERNEL - 1) // 8)
    args = (x, norm_g, w_in, b_in, ln_g, ln_b, w_s, bs_tile, conv_w, conv_b, cln_g, cln_b)
    return pl.pallas_call(
        functools.partial(_odd_mix_kernel, tm=tm),
        grid=(B, S // tm),
        in_specs=[pl.BlockSpec((1, tm, D), lambda b, t: (b, t, 0))] + [full(a) for a in args[1:]],
        out_specs=pl.BlockSpec((1, tm, n_out), lambda b, t: (b, t, 0)),
        out_shape=jax.ShapeDtypeStruct((B, S, n_out), BF16),
        scratch_shapes=[pltpu.VMEM((span + 8, CONV_CH), F32)],
        compiler_params=pltpu.CompilerParams(
            dimension_semantics=("arbitrary", "arbitrary"),
            vmem_limit_bytes=VMEM_LIMIT_BYTES),
        name="odd_mix",
    )(*args)


def _rope_tables(seq, dim, seg_per_block):
    inv = 1.0 / (ROPE_THETA ** (jnp.arange(0, dim, 2, dtype=F32) / dim))
    ang = jnp.arange(seq, dtype=F32)[:, None] * inv[None, :]
    cos, sin = jnp.cos(ang), jnp.sin(ang)
    cos = jnp.tile(jnp.concatenate([cos, cos], axis=1), (1, seg_per_block))
    sin = jnp.tile(jnp.concatenate([-sin, sin], axis=1), (1, seg_per_block))
    return cos, sin


def _block_diag_ones(seg):
    idx = jnp.arange(SECTION) // seg
    return (idx[:, None] == idx[None, :]).astype(BF16)


def kernel(x, attn_norm_g, ffn_norm_g, ffn_w_in, ffn_w_out, even_w_in, even_w_out,
           diff_q_norm_g, diff_k_norm_g, diff_lambda_q1, diff_lambda_k1,
           diff_lambda_q2, diff_lambda_k2, diff_subln_g, moba_q_norm_g, moba_k_norm_g,
           odd_w_in, odd_b_in, odd_w_out, gmlp_ln_g, gmlp_ln_b, gmlp_w_s, gmlp_b_s,
           conv_w, conv_b, conv_ln_g, conv_ln_b):
    B, S, D = x.shape
    assert S % 512 == 0 and S // MOBA_BLOCK <= GATE_SEG
    row = lambda v: v.reshape(1, -1).astype(F32)

    lambda_init = 0.8 - 0.6 * math.exp(-0.3 * 0)
    cosd, sind = _rope_tables(S, DIFF_HALF, LANES // DIFF_HALF)
    cosm, sinm = _rope_tables(S, HEAD_DIM, LANES // HEAD_DIM)
    gains = jnp.stack([
        jnp.tile(diff_q_norm_g[0], SECTION // DIFF_HALF),
        jnp.tile(diff_k_norm_g[0], SECTION // DIFF_HALF),
        jnp.tile(moba_q_norm_g[0], SECTION // HEAD_DIM),
        jnp.tile(moba_k_norm_g[0], SECTION // HEAD_DIM)]).astype(F32)
    qk, vt, bias = _even_proj(x, row(attn_norm_g[0]), even_w_in[0].astype(BF16),
                              _block_diag_ones(DIFF_HALF), _block_diag_ones(HEAD_DIM), gains,
                              cosd, sind, cosm, sinm)
    key_block = jnp.arange(S)[:, None] // MOBA_BLOCK
    ind = (key_block == (jnp.arange(LANES)[None, :] % GATE_SEG)).astype(BF16)
    lam_params = jnp.stack([diff_lambda_q1[0], diff_lambda_k1[0],
                            diff_lambda_q2[0], diff_lambda_k2[0]]).astype(F32)
    subln_tile = jnp.tile(diff_subln_g[0], LANES // HEAD_DIM).reshape(1, LANES).astype(F32)
    mix = _attention(qk, vt, bias, ind, lam_params, subln_tile, lambda_init)
    x = _out_ffn(mix, x, even_w_out[0].astype(BF16), row(ffn_norm_g[0]),
                 ffn_w_in[0].astype(BF16), ffn_w_out[0].astype(BF16))

    bs_tile = jnp.repeat(gmlp_b_s[0].T, GMLP_WIDTH // GMLP_GROUPS, axis=1).astype(F32)
    mix = _odd_mix(x, row(attn_norm_g[1]), odd_w_in[0].astype(BF16), row(odd_b_in[0]),
                   row(gmlp_ln_g[0]), row(gmlp_ln_b[0]), gmlp_w_s[0].astype(F32), bs_tile,
                   conv_w[0].astype(F32), row(conv_b[0]), row(conv_ln_g[0]), row(conv_ln_b[0]))
    x = _out_ffn(mix, x, odd_w_out[0].astype(BF16), row(ffn_norm_g[1]),
                 ffn_w_in[1].astype(BF16), ffn_w_out[1].astype(BF16))
    return x
```

```python
import functools
import math

import jax
import jax.numpy as jnp
from jax import lax
from jax.experimental import pallas as pl
from jax.experimental.pallas import tpu as pltpu

F32 = jnp.float32
BF16 = jnp.bfloat16

LANES = 128
VMEM_LIMIT_BYTES = 56 * 1024 * 1024

HEAD_DIM = 64
DIFF_HALF = 32
N_HEADS_DIFF = 8
N_HEADS_MOBA = 8
MOBA_BLOCK = 256
MOBA_TOPK = 3
GMLP_CHUNK = 128
GMLP_GROUPS = 8
GMLP_WIDTH = 512
CONV_CH = 512
CONV_KERNEL = 31
ROPE_THETA = 10000.0
EPS = 1e-6

SECTION = 512
GATE_SEG = LANES // N_HEADS_MOBA
CONV_HALO = 32
MASKED_BIAS = -1e30
M_INIT = -1e29
LOG2E = math.log2(math.e)

_NT = (((1,), (1,)), ((), ()))


def _rms(x, g):
    return x * lax.rsqrt(jnp.mean(x * x, axis=-1, keepdims=True) + EPS) * g


def _layer_norm(x, g, b):
    mu = jnp.mean(x, axis=-1, keepdims=True)
    xc = x - mu
    return xc * lax.rsqrt(jnp.mean(xc * xc, axis=-1, keepdims=True) + EPS) * g + b


def _seg_allreduce(x, op, seg, lane):
    s = 1
    while s < seg:
        up = pltpu.roll(x, LANES - s, 1)
        dn = pltpu.roll(x, s, 1)
        x = op(x, jnp.where((lane & s) == 0, up, dn))
        s *= 2
    return x


def _even_proj_kernel(x_ref, g_ref, w_ref, bdd_ref, bdm_ref, gain_ref,
                      cosd_ref, sind_ref, cosm_ref, sinm_ref,
                      qk_ref, vt_ref, bias_ref, kbt_ref, *, tm):
    j = pl.program_id(1)
    h = _rms(x_ref[0], g_ref[...]).astype(BF16)
    lane = lax.broadcasted_iota(jnp.int32, (tm, LANES), 1)

    def qk_section(sec, bd_ref, seg, cos_ref, sin_ref):
        y = jnp.dot(h, w_ref[:, sec * SECTION:(sec + 1) * SECTION], preferred_element_type=F32)
        ss = jnp.dot((y * y).astype(BF16), bd_ref[...], preferred_element_type=F32)
        yn = y * lax.rsqrt(ss * (1.0 / seg) + EPS) * gain_ref[sec_row[sec]:sec_row[sec] + 1, :]
        half = seg // 2
        lower = (lane & (seg - 1)) < half
        cos = cos_ref[...]
        sin = sin_ref[...]
        out = []
        for c in range(SECTION // LANES):
            yc = yn[:, c * LANES:(c + 1) * LANES]
            rot = jnp.where(lower, pltpu.roll(yc, LANES - half, 1), pltpu.roll(yc, half, 1))
            out.append(yc * cos + rot * sin)
        return out

    def store(out_sec, chunks, scale):
        for c, yc in enumerate(chunks):
            lo = out_sec * SECTION + c * LANES
            qk_ref[0, :, lo:lo + LANES] = (yc * scale).astype(BF16)

    sec_row = {0: 0, 1: 1, 3: 2, 4: 3}
    store(0, qk_section(0, bdd_ref, DIFF_HALF, cosd_ref, sind_ref), DIFF_HALF ** -0.5 * LOG2E)
    store(1, qk_section(1, bdd_ref, DIFF_HALF, cosd_ref, sind_ref), 1.0)
    mq = qk_section(3, bdm_ref, HEAD_DIM, cosm_ref, sinm_ref)
    store(2, mq, HEAD_DIM ** -0.5 * LOG2E)
    mk = qk_section(4, bdm_ref, HEAD_DIM, cosm_ref, sinm_ref)
    store(3, mk, 1.0)
    for out_sec, sec in enumerate((2, 5)):
        lo = sec * SECTION
        v = jnp.dot(h, w_ref[:, lo:lo + SECTION], preferred_element_type=F32)
        vt_ref[0, 0, out_sec * SECTION:(out_sec + 1) * SECTION, :] = v.T.astype(BF16)

    @pl.when(j == 0)
    def _():
        kbt_ref[...] = jnp.zeros_like(kbt_ref)

    mk_full = jnp.concatenate(mk, axis=1)
    kbar = jnp.mean(mk_full, axis=0, keepdims=True)
    head_of_lane = lax.broadcasted_iota(jnp.int32, (1, SECTION), 1) // HEAD_DIM
    for hh in range(N_HEADS_MOBA):
        kbt_ref[pl.ds(hh * GATE_SEG + j, 1), :] = jnp.where(head_of_lane == hh, kbar, 0.0)

    gate = lax.dot_general(jnp.concatenate(mq, axis=1), kbt_ref[...], _NT,
                           precision=lax.Precision.HIGHEST, preferred_element_type=F32)
    blk = lane & (GATE_SEG - 1)
    blk_f = blk.astype(F32)
    neg_inf = jnp.float32(-jnp.inf)
    avail = jnp.where(blk < j, gate, neg_inf)
    chosen = blk == j
    for _ in range(MOBA_TOPK):
        best = _seg_allreduce(avail, jnp.maximum, GATE_SEG, lane)
        cand = jnp.logical_and(avail == best, avail > neg_inf)
        first = _seg_allreduce(jnp.where(cand, blk_f, float(GATE_SEG)), jnp.minimum, GATE_SEG, lane)
        pick = blk_f == first
        chosen = jnp.logical_or(chosen, pick)
        avail = jnp.where(pick, neg_inf, avail)
    bias_ref[0] = jnp.where(chosen, 0.0, MASKED_BIAS).astype(BF16)


def _even_proj(x, norm_g, w_in, bdd, bdm, gains, cosd, sind, cosm, sinm):
    B, S, D = x.shape
    tm = MOBA_BLOCK
    n_out = w_in.shape[1]
    n_qk = 4 * SECTION
    n_v = 2 * SECTION
    const = lambda b, t: (0, 0)
    tab = pl.BlockSpec((tm, LANES), lambda b, t: (t, 0))
    return pl.pallas_call(
        functools.partial(_even_proj_kernel, tm=tm),
        grid=(B, S // tm),
        in_specs=[
            pl.BlockSpec((1, tm, D), lambda b, t: (b, t, 0)),
            pl.BlockSpec((1, D), const),
            pl.BlockSpec((D, n_out), const),
            pl.BlockSpec((SECTION, SECTION), const),
            pl.BlockSpec((SECTION, SECTION), const),
            pl.BlockSpec((4, SECTION), const),
            tab, tab, tab, tab,
        ],
        out_specs=[
            pl.BlockSpec((1, tm, n_qk), lambda b, t: (b, t, 0)),
            pl.BlockSpec((1, 1, n_v, tm), lambda b, t: (b, t, 0, 0)),
            pl.BlockSpec((1, tm, LANES), lambda b, t: (b, t, 0)),
        ],
        out_shape=[
            jax.ShapeDtypeStruct((B, S, n_qk), BF16),
            jax.ShapeDtypeStruct((B, S // tm, n_v, tm), BF16),
            jax.ShapeDtypeStruct((B, S, LANES), BF16),
        ],
        scratch_shapes=[pltpu.VMEM((LANES, SECTION), F32)],
        compiler_params=pltpu.CompilerParams(
            dimension_semantics=("arbitrary", "arbitrary"),
            vmem_limit_bytes=VMEM_LIMIT_BYTES),
        name="even_proj",
    )(x, norm_g, w_in, bdd, bdm, gains, cosd, sind, cosm, sinm)


def _flash_cols(q_strips, v_rows, k_ref, vt_ref, qi, t, s_ref, p_ref, k_aug_ref=None):
    n = len(q_strips)
    q_all = jnp.concatenate(q_strips, axis=0)

    def scores(j):
        start = pl.multiple_of(j * t, t)
        k = k_ref[0, pl.ds(start, t), :]
        if k_aug_ref is not None:
            k = jnp.concatenate([k, k_aug_ref[pl.ds(start, t), :]], axis=1)
        return lax.dot_general(k, q_all, _NT, preferred_element_type=F32)

    def softmax(s_all, stats, diagonal):
        ps, new_stats, alphas = [], [], []
        for i in range(n):
            m, l = stats[i]
            s = s_all[:, i * t:(i + 1) * t]
            if diagonal:
                key = lax.broadcasted_iota(jnp.int32, (t, t), 0)
                qry = lax.broadcasted_iota(jnp.int32, (t, t), 1)
                s = jnp.where(key <= qry, s, -jnp.inf)
            m_new = jnp.maximum(m, jnp.max(s, axis=0, keepdims=True))
            p = jnp.exp2(s - m_new)
            alpha = jnp.exp2(m - m_new)
            ps.append(p.astype(BF16))
            new_stats.append((m_new, alpha * l + jnp.sum(p, axis=0, keepdims=True)))
            alphas.append(alpha)
        return tuple(ps), tuple(new_stats), alphas

    def weighted_values(j, ps):
        vt = vt_ref[0, j]
        return [jnp.dot(vt[lo:hi], p, preferred_element_type=F32) for (lo, hi), p in zip(v_rows, ps)]

    def step(j, cur, carry):
        stats, accs = carry
        s_ref[1 - cur] = scores(j + 1)
        p_prev = tuple(p_ref[1 - cur, i] for i in range(n))
        pv = weighted_values(jnp.maximum(j - 1, 0), p_prev)
        p_cur, stats, alphas = softmax(s_ref[cur], stats, False)
        for i in range(n):
            p_ref[cur, i] = p_cur[i]
        accs = tuple(a * (acc + x) for a, acc, x in zip(alphas, accs, pv))
        return stats, accs

    s_ref[0] = scores(0)
    p_ref[1] = jnp.zeros(p_ref.shape[1:], BF16)
    init = (tuple((jnp.full((1, t), M_INIT, F32), jnp.zeros((1, t), F32)) for _ in range(n)),
            tuple(jnp.zeros((hi - lo, t), F32) for lo, hi in v_rows))
    carry = lax.fori_loop(0, qi // 2, lambda i, c: step(2 * i + 1, 1, step(2 * i, 0, c)), init)
    odd = (qi & 1) == 1
    stats, accs = lax.cond(odd, lambda c: step(qi - 1, 0, c), lambda c: c, carry)
    cur = qi & 1
    p_prev = tuple(p_ref[1 - cur, i] for i in range(n))
    pv_prev = weighted_values(jnp.maximum(qi - 1, 0), p_prev)
    p_cur, stats, alphas = softmax(s_ref[cur], stats, True)
    pv_cur = weighted_values(qi, p_cur)
    return [(a * (acc + x) + y, l)
            for a, acc, x, y, (_, l) in zip(alphas, accs, pv_prev, pv_cur, stats)]


def _diff_attn_kernel(lam_ref, q_ref, k_ref, vt_ref, sg_ref, o_ref, s_ref, p_ref, *, t, lambda_init):
    qi = pl.program_id(2)
    lane = lax.broadcasted_iota(jnp.int32, (t, LANES), 1)
    q = q_ref[0]
    seg = lane // DIFF_HALF
    strips = [jnp.where(seg == i, q, jnp.zeros_like(q)) for i in range(4)]
    v_rows = [(HEAD_DIM * (i // 2), HEAD_DIM * (i // 2 + 1)) for i in range(4)]
    res = _flash_cols(strips, v_rows, k_ref, vt_ref, qi, t, s_ref, p_ref)
    a = [acc * (1.0 / l) for acc, l in res]
    lam_p = lam_ref[...]
    lam = (jnp.exp(jnp.sum(lam_p[0:1] * lam_p[1:2], axis=1, keepdims=True))
           - jnp.exp(jnp.sum(lam_p[2:3] * lam_p[3:4], axis=1, keepdims=True)) + lambda_init)
    heads = []
    for hh in range(2):
        o = a[2 * hh] - lam * a[2 * hh + 1]
        heads.append(o * lax.rsqrt(jnp.mean(o * o, axis=0, keepdims=True) + EPS))
    o = jnp.concatenate(heads, axis=0).T
    o_ref[0] = (o * sg_ref[...] * (1.0 - lambda_init)).astype(BF16)


def _moba_attn_kernel(q_ref, bias_ref, k_ref, vt_ref, ind_ref, o_ref, s_ref, p_ref, *, t):
    hp = pl.program_id(1)
    qi = pl.program_id(2)
    lane = lax.broadcasted_iota(jnp.int32, (t, LANES), 1)
    q = q_ref[0]
    bias = bias_ref[0]
    strips = []
    for hh in range(2):
        qz = jnp.where(lane // HEAD_DIM == hh, q, jnp.zeros_like(q))
        bz = jnp.where(lane // GATE_SEG == 2 * hp + hh, bias, jnp.zeros_like(bias))
        strips.append(jnp.concatenate([qz, bz], axis=1))
    v_rows = [(0, HEAD_DIM), (HEAD_DIM, 2 * HEAD_DIM)]
    res = _flash_cols(strips, v_rows, k_ref, vt_ref, qi, t, s_ref, p_ref, k_aug_ref=ind_ref)
    o = jnp.concatenate([acc * (1.0 / l) for acc, l in res], axis=0)
    o_ref[0] = o.T.astype(BF16)


def _attention(qk, vt, bias, ind, lam_params, subln_tile, lambda_init):
    B, S, _ = qk.shape
    t = MOBA_BLOCK
    n_pairs = SECTION // LANES
    grid = (B, n_pairs, S // t)
    q_spec = lambda sec: pl.BlockSpec((1, t, LANES), lambda b, p, i: (b, i, sec * n_pairs + p))
    k_spec = lambda sec: pl.BlockSpec((1, S, LANES), lambda b, p, i: (b, 0, sec * n_pairs + p))
    vt_spec = lambda sec: pl.BlockSpec((1, S // t, LANES, t), lambda b, p, i: (b, 0, sec * n_pairs + p, 0))
    out_spec = pl.BlockSpec((1, t, LANES), lambda b, p, i: (b, i, p))
    params = pltpu.CompilerParams(
        dimension_semantics=("arbitrary", "arbitrary", "arbitrary"),
        vmem_limit_bytes=VMEM_LIMIT_BYTES)
    out_shape = jax.ShapeDtypeStruct((B, S, SECTION), BF16)
    pipeline_scratch = lambda n: [pltpu.VMEM((2, t, n * t), F32), pltpu.VMEM((2, n, t, t), BF16)]

    diff = pl.pallas_call(
        functools.partial(_diff_attn_kernel, t=t, lambda_init=lambda_init),
        grid=grid,
        in_specs=[pl.BlockSpec((4, DIFF_HALF), lambda b, p, i: (0, 0)),
                  q_spec(0), k_spec(1), vt_spec(0),
                  pl.BlockSpec((1, LANES), lambda b, p, i: (0, 0))],
        out_specs=out_spec, out_shape=out_shape, compiler_params=params,
        scratch_shapes=pipeline_scratch(4),
        name="diff_attn",
    )(lam_params, qk, qk, vt, subln_tile)

    moba = pl.pallas_call(
        functools.partial(_moba_attn_kernel, t=t),
        grid=grid,
        in_specs=[q_spec(2),
                  pl.BlockSpec((1, t, LANES), lambda b, p, i: (b, i, 0)),
                  k_spec(3), vt_spec(1),
                  pl.BlockSpec((S, LANES), lambda b, p, i: (0, 0))],
        out_specs=out_spec, out_shape=out_shape, compiler_params=params,
        scratch_shapes=pipeline_scratch(2),
        name="moba_attn",
    )(qk, bias, qk, vt, ind)
    return diff, moba


def _out_ffn_kernel(mixa_ref, mixb_ref, x_ref, wo_ref, g_ref, wi_ref, w2_ref, o_ref, *, d_ff, chunk):
    mix = jnp.concatenate([mixa_ref[0], mixb_ref[0]], axis=1)
    x1 = x_ref[0] + jnp.dot(mix, wo_ref[...], preferred_element_type=F32)
    hn = _rms(x1, g_ref[...]).astype(BF16)
    acc = x1
    for c in range(d_ff // chunk):
        lo = c * chunk
        g = jnp.dot(hn, wi_ref[:, lo:lo + chunk], preferred_element_type=F32)
        u = jnp.dot(hn, wi_ref[:, d_ff + lo:d_ff + lo + chunk], preferred_element_type=F32)
        a = (g * jax.nn.sigmoid(g) * u).astype(BF16)
        acc = acc + jnp.dot(a, w2_ref[lo:lo + chunk, :], preferred_element_type=F32)
    o_ref[0] = acc


def _out_ffn(mix_a, col_a, mix_b, col_b, x, w_out, norm_g, w_in, w2):
    B, S, D = x.shape
    d_ff = w2.shape[0]
    tm = 512
    half = w_out.shape[0] // 2
    const = lambda b, t: (0, 0)
    resident = lambda shape: pl.BlockSpec(shape, const, pipeline_mode=pl.Buffered(1))
    tile = pl.BlockSpec((1, tm, D), lambda b, t: (b, t, 0))
    mix_spec = lambda col: pl.BlockSpec((1, tm, half), lambda b, t: (b, t, col))
    return pl.pallas_call(
        functools.partial(_out_ffn_kernel, d_ff=d_ff, chunk=256),
        grid=(B, S // tm),
        in_specs=[mix_spec(col_a), mix_spec(col_b), tile, resident(w_out.shape), pl.BlockSpec((1, D), const),
                  resident(w_in.shape), resident(w2.shape)],
        out_specs=tile,
        out_shape=jax.ShapeDtypeStruct((B, S, D), F32),
        compiler_params=pltpu.CompilerParams(
            dimension_semantics=("arbitrary", "arbitrary"),
            vmem_limit_bytes=VMEM_LIMIT_BYTES),
        name="out_ffn",
    )(mix_a, mix_b, x, w_out, norm_g, w_in, w2)


def _odd_mix_kernel(x_ref, g_ref, w_ref, b_ref, lng_ref, lnb_ref, ws_ref, bs_ref,
                    cw_ref, cb_ref, clg_ref, clb_ref, o_ref, ext_ref, *, tm):
    ti = pl.program_id(1)
    h = _rms(x_ref[0], g_ref[...]).astype(BF16)

    def proj(lo, width):
        return jnp.dot(h, w_ref[:, lo:lo + width], preferred_element_type=F32) + b_ref[:, lo:lo + width]

    gu = jax.nn.gelu(proj(0, GMLP_WIDTH), approximate=True)
    gv = jax.nn.gelu(proj(GMLP_WIDTH, GMLP_WIDTH), approximate=True)
    gv = _layer_norm(gv, lng_ref[...], lnb_ref[...]).astype(BF16)
    T = GMLP_CHUNK
    tri = lax.broadcasted_iota(jnp.int32, (T, T), 1) <= lax.broadcasted_iota(jnp.int32, (T, T), 0)
    lane = lax.broadcasted_iota(jnp.int32, (T, LANES), 1)
    first = lane < (GMLP_WIDTH // GMLP_GROUPS)
    zero = jnp.zeros((T, LANES), BF16)
    for lb in range(GMLP_WIDTH // LANES):
        wpair = jnp.concatenate(
            [jnp.where(tri, ws_ref[2 * lb + i], 0.0).astype(BF16) for i in range(2)], axis=1)
        bs = bs_ref[:, lb * LANES:(lb + 1) * LANES]
        for c in range(tm // T):
            vc = gv[c * T:(c + 1) * T, lb * LANES:(lb + 1) * LANES]
            vpair = jnp.concatenate([jnp.where(first, vc, zero), jnp.where(first, zero, vc)], axis=0)
            sg = jnp.dot(wpair, vpair, preferred_element_type=F32) + bs
            o_ref[0, c * T:(c + 1) * T, lb * LANES:(lb + 1) * LANES] = (
                gu[c * T:(c + 1) * T, lb * LANES:(lb + 1) * LANES] * sg).astype(BF16)

    lo = 2 * GMLP_WIDTH
    cval = proj(lo, CONV_CH) * jax.nn.sigmoid(proj(lo + CONV_CH, CONV_CH))

    @pl.when(ti == 0)
    def _():
        ext_ref[0:CONV_HALO, :] = jnp.zeros((CONV_HALO, CONV_CH), F32)

    ext_ref[CONV_HALO:CONV_HALO + tm, :] = cval
    first_tap = CONV_HALO - (CONV_KERNEL - 1)
    span = tm + 8 * ((CONV_KERNEL - 1) // 8)
    conv = jnp.zeros((tm, CONV_CH), F32) + cb_ref[...]
    for r in range(8):
        taps = [k for k in range(CONV_KERNEL) if (first_tap + k) % 8 == r]
        base = first_tap + taps[0]
        rows = ext_ref[base:base + span, :]
        for k in taps:
            off = first_tap + k - base
            conv = conv + rows[off:off + tm, :] * cw_ref[k:k + 1, :]
    ext_ref[0:CONV_HALO, :] = ext_ref[tm:tm + CONV_HALO, :]
    dn = _layer_norm(conv, clg_ref[...], clb_ref[...])
    o_ref[0, :, GMLP_WIDTH:GMLP_WIDTH + CONV_CH] = (dn * jax.nn.sigmoid(dn)).astype(BF16)


def _odd_mix(x, norm_g, w_in, b_in, ln_g, ln_b, w_s, bs_tile, conv_w, conv_b, cln_g, cln_b):
    B, S, D = x.shape
    tm = 512
    const2 = lambda b, t: (0, 0)
    full = lambda a: pl.BlockSpec(a.shape, (lambda b, t: (0,) * a.ndim))
    n_out = GMLP_WIDTH + CONV_CH
    span = tm + 8 * ((CONV_KERNEL - 1) // 8)
    args = (x, norm_g, w_in, b_in, ln_g, ln_b, w_s, bs_tile, conv_w, conv_b, cln_g, cln_b)
    return pl.pallas_call(
        functools.partial(_odd_mix_kernel, tm=tm),
        grid=(B, S // tm),
        in_specs=[pl.BlockSpec((1, tm, D), lambda b, t: (b, t, 0))] + [full(a) for a in args[1:]],
        out_specs=pl.BlockSpec((1, tm, n_out), lambda b, t: (b, t, 0)),
        out_shape=jax.ShapeDtypeStruct((B, S, n_out), BF16),
        scratch_shapes=[pltpu.VMEM((span + 8, CONV_CH), F32)],
        compiler_params=pltpu.CompilerParams(
            dimension_semantics=("arbitrary", "arbitrary"),
            vmem_limit_bytes=VMEM_LIMIT_BYTES),
        name="odd_mix",
    )(*args)


def _rope_tables(seq, dim, seg_per_block):
    inv = 1.0 / (ROPE_THETA ** (jnp.arange(0, dim, 2, dtype=F32) / dim))
    ang = jnp.arange(seq, dtype=F32)[:, None] * inv[None, :]
    cos, sin = jnp.cos(ang), jnp.sin(ang)
    cos = jnp.tile(jnp.concatenate([cos, cos], axis=1), (1, seg_per_block))
    sin = jnp.tile(jnp.concatenate([-sin, sin], axis=1), (1, seg_per_block))
    return cos, sin


def _block_diag_ones(seg):
    idx = jnp.arange(SECTION) // seg
    return (idx[:, None] == idx[None, :]).astype(BF16)


def kernel(x, attn_norm_g, ffn_norm_g, ffn_w_in, ffn_w_out, even_w_in, even_w_out,
           diff_q_norm_g, diff_k_norm_g, diff_lambda_q1, diff_lambda_k1,
           diff_lambda_q2, diff_lambda_k2, diff_subln_g, moba_q_norm_g, moba_k_norm_g,
           odd_w_in, odd_b_in, odd_w_out, gmlp_ln_g, gmlp_ln_b, gmlp_w_s, gmlp_b_s,
           conv_w, conv_b, conv_ln_g, conv_ln_b):
    B, S, D = x.shape
    assert S % 512 == 0 and S // MOBA_BLOCK <= GATE_SEG
    row = lambda v: v.reshape(1, -1).astype(F32)

    lambda_init = 0.8 - 0.6 * math.exp(-0.3 * 0)
    cosd, sind = _rope_tables(S, DIFF_HALF, LANES // DIFF_HALF)
    cosm, sinm = _rope_tables(S, HEAD_DIM, LANES // HEAD_DIM)
    gains = jnp.stack([
        jnp.tile(diff_q_norm_g[0], SECTION // DIFF_HALF),
        jnp.tile(diff_k_norm_g[0], SECTION // DIFF_HALF),
        jnp.tile(moba_q_norm_g[0], SECTION // HEAD_DIM),
        jnp.tile(moba_k_norm_g[0], SECTION // HEAD_DIM)]).astype(F32)
    qk, vt, bias = _even_proj(x, row(attn_norm_g[0]), even_w_in[0].astype(BF16),
                              _block_diag_ones(DIFF_HALF), _block_diag_ones(HEAD_DIM), gains,
                              cosd, sind, cosm, sinm)
    key_block = jnp.arange(S)[:, None] // MOBA_BLOCK
    ind = (key_block == (jnp.arange(LANES)[None, :] % GATE_SEG)).astype(BF16)
    lam_params = jnp.stack([diff_lambda_q1[0], diff_lambda_k1[0],
                            diff_lambda_q2[0], diff_lambda_k2[0]]).astype(F32)
    subln_tile = jnp.tile(diff_subln_g[0], LANES // HEAD_DIM).reshape(1, LANES).astype(F32)
    diff, moba = _attention(qk, vt, bias, ind, lam_params, subln_tile, lambda_init)
    x = _out_ffn(diff, 0, moba, 0, x, even_w_out[0].astype(BF16), row(ffn_norm_g[0]),
                 ffn_w_in[0].astype(BF16), ffn_w_out[0].astype(BF16))

    bs_tile = jnp.repeat(gmlp_b_s[0].T, GMLP_WIDTH // GMLP_GROUPS, axis=1).astype(F32)
    mix = _odd_mix(x, row(attn_norm_g[1]), odd_w_in[0].astype(BF16), row(odd_b_in[0]),
                   row(gmlp_ln_g[0]), row(gmlp_ln_b[0]), gmlp_w_s[0].astype(F32), bs_tile,
                   conv_w[0].astype(F32), row(conv_b[0]), row(conv_ln_g[0]), row(conv_ln_b[0]))
    x = _out_ffn(mix, 0, mix, 1, x, odd_w_out[0].astype(BF16), row(ffn_norm_g[1]),
                 ffn_w_in[1].astype(BF16), ffn_w_out[1].astype(BF16))
    return x
```

```python
import functools
import math

import jax
import jax.numpy as jnp
from jax import lax
from jax.experimental import pallas as pl
from jax.experimental.pallas import tpu as pltpu

F32 = jnp.float32
BF16 = jnp.bfloat16

LANES = 128
VMEM_LIMIT_BYTES = 56 * 1024 * 1024

HEAD_DIM = 64
DIFF_HALF = 32
N_HEADS_DIFF = 8
N_HEADS_MOBA = 8
MOBA_BLOCK = 256
MOBA_TOPK = 3
GMLP_CHUNK = 128
GMLP_GROUPS = 8
GMLP_WIDTH = 512
CONV_CH = 512
CONV_KERNEL = 31
ROPE_THETA = 10000.0
EPS = 1e-6

SECTION = 512
GATE_SEG = LANES // N_HEADS_MOBA
CONV_HALO = 32
MASKED_BIAS = -1e30
M_INIT = -1e29
LOG2E = math.log2(math.e)

_NT = (((1,), (1,)), ((), ()))


def _rms(x, g):
    return x * lax.rsqrt(jnp.mean(x * x, axis=-1, keepdims=True) + EPS) * g


def _layer_norm(x, g, b):
    mu = jnp.mean(x, axis=-1, keepdims=True)
    xc = x - mu
    return xc * lax.rsqrt(jnp.mean(xc * xc, axis=-1, keepdims=True) + EPS) * g + b


def _seg_allreduce(x, op, seg, lane):
    s = 1
    while s < seg:
        up = pltpu.roll(x, LANES - s, 1)
        dn = pltpu.roll(x, s, 1)
        x = op(x, jnp.where((lane & s) == 0, up, dn))
        s *= 2
    return x


def _even_proj_kernel(x_ref, g_ref, w_ref, bdd_ref, bdm_ref, gain_ref,
                      cosd_ref, sind_ref, cosm_ref, sinm_ref,
                      qk_ref, vt_ref, bias_ref, kbt_ref, *, tm):
    j = pl.program_id(1)
    h = _rms(x_ref[0], g_ref[...]).astype(BF16)
    lane = lax.broadcasted_iota(jnp.int32, (tm, LANES), 1)

    def qk_section(sec, bd_ref, seg, cos_ref, sin_ref):
        y = jnp.dot(h, w_ref[:, sec * SECTION:(sec + 1) * SECTION], preferred_element_type=F32)
        ss = jnp.dot((y * y).astype(BF16), bd_ref[...], preferred_element_type=F32)
        yn = y * lax.rsqrt(ss * (1.0 / seg) + EPS) * gain_ref[sec_row[sec]:sec_row[sec] + 1, :]
        half = seg // 2
        lower = (lane & (seg - 1)) < half
        cos = cos_ref[...]
        sin = sin_ref[...]
        out = []
        for c in range(SECTION // LANES):
            yc = yn[:, c * LANES:(c + 1) * LANES]
            rot = jnp.where(lower, pltpu.roll(yc, LANES - half, 1), pltpu.roll(yc, half, 1))
            out.append(yc * cos + rot * sin)
        return out

    def store(out_sec, chunks, scale):
        for c, yc in enumerate(chunks):
            lo = out_sec * SECTION + c * LANES
            qk_ref[0, :, lo:lo + LANES] = (yc * scale).astype(BF16)

    sec_row = {0: 0, 1: 1, 3: 2, 4: 3}
    store(0, qk_section(0, bdd_ref, DIFF_HALF, cosd_ref, sind_ref), DIFF_HALF ** -0.5 * LOG2E)
    store(1, qk_section(1, bdd_ref, DIFF_HALF, cosd_ref, sind_ref), 1.0)
    mq = qk_section(3, bdm_ref, HEAD_DIM, cosm_ref, sinm_ref)
    store(2, mq, HEAD_DIM ** -0.5 * LOG2E)
    mk = qk_section(4, bdm_ref, HEAD_DIM, cosm_ref, sinm_ref)
    store(3, mk, 1.0)
    for out_sec, sec in enumerate((2, 5)):
        lo = sec * SECTION
        v = jnp.dot(h, w_ref[:, lo:lo + SECTION], preferred_element_type=F32)
        vt_ref[0, 0, out_sec * SECTION:(out_sec + 1) * SECTION, :] = v.T.astype(BF16)

    @pl.when(j == 0)
    def _():
        kbt_ref[...] = jnp.zeros_like(kbt_ref)

    mk_full = jnp.concatenate(mk, axis=1)
    kbar = jnp.mean(mk_full, axis=0, keepdims=True)
    head_of_lane = lax.broadcasted_iota(jnp.int32, (1, SECTION), 1) // HEAD_DIM
    for hh in range(N_HEADS_MOBA):
        kbt_ref[pl.ds(hh * GATE_SEG + j, 1), :] = jnp.where(head_of_lane == hh, kbar, 0.0)

    gate = lax.dot_general(jnp.concatenate(mq, axis=1), kbt_ref[...], _NT,
                           precision=lax.Precision.HIGHEST, preferred_element_type=F32)
    blk = lane & (GATE_SEG - 1)
    blk_f = blk.astype(F32)
    neg_inf = jnp.float32(-jnp.inf)
    avail = jnp.where(blk < j, gate, neg_inf)
    chosen = blk == j
    for _ in range(MOBA_TOPK):
        best = _seg_allreduce(avail, jnp.maximum, GATE_SEG, lane)
        cand = jnp.logical_and(avail == best, avail > neg_inf)
        first = _seg_allreduce(jnp.where(cand, blk_f, float(GATE_SEG)), jnp.minimum, GATE_SEG, lane)
        pick = blk_f == first
        chosen = jnp.logical_or(chosen, pick)
        avail = jnp.where(pick, neg_inf, avail)
    bias_ref[0] = jnp.where(chosen, 0.0, MASKED_BIAS).astype(BF16)


def _even_proj(x, norm_g, w_in, bdd, bdm, gains, cosd, sind, cosm, sinm):
    B, S, D = x.shape
    tm = MOBA_BLOCK
    n_out = w_in.shape[1]
    n_qk = 4 * SECTION
    n_v = 2 * SECTION
    const = lambda b, t: (0, 0)
    tab = pl.BlockSpec((tm, LANES), lambda b, t: (t, 0))
    return pl.pallas_call(
        functools.partial(_even_proj_kernel, tm=tm),
        grid=(B, S // tm),
        in_specs=[
            pl.BlockSpec((1, tm, D), lambda b, t: (b, t, 0)),
            pl.BlockSpec((1, D), const),
            pl.BlockSpec((D, n_out), const),
            pl.BlockSpec((SECTION, SECTION), const),
            pl.BlockSpec((SECTION, SECTION), const),
            pl.BlockSpec((4, SECTION), const),
            tab, tab, tab, tab,
        ],
        out_specs=[
            pl.BlockSpec((1, tm, n_qk), lambda b, t: (b, t, 0)),
            pl.BlockSpec((1, 1, n_v, tm), lambda b, t: (b, t, 0, 0)),
            pl.BlockSpec((1, tm, LANES), lambda b, t: (b, t, 0)),
        ],
        out_shape=[
            jax.ShapeDtypeStruct((B, S, n_qk), BF16),
            jax.ShapeDtypeStruct((B, S // tm, n_v, tm), BF16),
            jax.ShapeDtypeStruct((B, S, LANES), BF16),
        ],
        scratch_shapes=[pltpu.VMEM((LANES, SECTION), F32)],
        compiler_params=pltpu.CompilerParams(
            dimension_semantics=("arbitrary", "arbitrary"),
            vmem_limit_bytes=VMEM_LIMIT_BYTES),
        name="even_proj",
    )(x, norm_g, w_in, bdd, bdm, gains, cosd, sind, cosm, sinm)


def _flash_head_pair(make_q, n, v_rows, k_ref, vt_ref, s_ref, p_ref, nq, t, emit, k_aug_ref=None):
    def scores(q_all, j):
        start = pl.multiple_of(j * t, t)
        k = k_ref[0, pl.ds(start, t), :]
        if k_aug_ref is not None:
            k = jnp.concatenate([k, k_aug_ref[pl.ds(start, t), :]], axis=1)
        return lax.dot_general(k, q_all, _NT, preferred_element_type=F32)

    def softmax(s_all, stats, diagonal):
        ps, new_stats, alphas = [], [], []
        for i in range(n):
            m, l = stats[i]
            s = s_all[:, i * t:(i + 1) * t]
            if diagonal:
                key = lax.broadcasted_iota(jnp.int32, (t, t), 0)
                qry = lax.broadcasted_iota(jnp.int32, (t, t), 1)
                s = jnp.where(key <= qry, s, -jnp.inf)
            m_new = jnp.maximum(m, jnp.max(s, axis=0, keepdims=True))
            p = jnp.exp2(s - m_new)
            alpha = jnp.exp2(m - m_new)
            ps.append(p.astype(BF16))
            new_stats.append((m_new, alpha * l + jnp.sum(p, axis=0, keepdims=True)))
            alphas.append(alpha)
        return tuple(ps), tuple(new_stats), alphas

    def weighted_values(j, ps):
        vt = vt_ref[0, j]
        return [jnp.dot(vt[lo:hi], p, preferred_element_type=F32) for (lo, hi), p in zip(v_rows, ps)]

    def step(q_all, j, cur, nxt, carry):
        stats, accs = carry
        s_ref[nxt] = scores(q_all, j + 1)
        p_prev = tuple(p_ref[nxt, i] for i in range(n))
        pv = weighted_values(jnp.maximum(j - 1, 0), p_prev)
        pv = [jnp.where(j > 0, x, 0.0) for x in pv]
        p_cur, stats, alphas = softmax(s_ref[cur], stats, False)
        for i in range(n):
            p_ref[cur, i] = p_cur[i]
        accs = tuple(a * (acc + x) for a, acc, x in zip(alphas, accs, pv))
        return stats, accs

    def query_tile(qi, first, second, odd, next_first):
        q_all = make_q(qi)
        init = (tuple((jnp.full((1, t), M_INIT, F32), jnp.zeros((1, t), F32)) for _ in range(n)),
                tuple(jnp.zeros((hi - lo, t), F32) for lo, hi in v_rows))
        carry = lax.fori_loop(
            0, qi // 2,
            lambda i, c: step(q_all, 2 * i + 1, second, first, step(q_all, 2 * i, first, second, c)), init)
        if odd:
            carry = step(q_all, qi - 1, first, second, carry)
        diag, prev = (second, first) if odd else (first, second)
        stats, accs = carry
        s_ref[next_first] = scores(make_q(jnp.minimum(qi + 1, nq - 1)), 0)
        pv_prev = weighted_values(jnp.maximum(qi - 1, 0), tuple(p_ref[prev, i] for i in range(n)))
        pv_prev = [jnp.where(qi > 0, x, 0.0) for x in pv_prev]
        p_cur, stats, alphas = softmax(s_ref[diag], stats, True)
        pv_cur = weighted_values(qi, p_cur)
        emit(qi, [(a * (acc + x) + y, l)
                  for a, acc, x, y, (_, l) in zip(alphas, accs, pv_prev, pv_cur, stats)])

    p_ref[...] = jnp.zeros(p_ref.shape, BF16)
    s_ref[0] = scores(make_q(0), 0)

    def pair_of_tiles(u, _):
        query_tile(2 * u, 0, 1, False, 2)
        query_tile(2 * u + 1, 2, 1, True, 0)
        return 0

    lax.fori_loop(0, nq // 2, pair_of_tiles, 0)


def _diff_attn_kernel(lam_ref, q_ref, k_ref, vt_ref, sg_ref, o_ref, s_ref, p_ref, *, t, lambda_init):
    nq = q_ref.shape[1] // t
    seg = lax.broadcasted_iota(jnp.int32, (t, LANES), 1) // DIFF_HALF
    lam_p = lam_ref[...]
    lam = (jnp.exp(jnp.sum(lam_p[0:1] * lam_p[1:2], axis=1, keepdims=True))
           - jnp.exp(jnp.sum(lam_p[2:3] * lam_p[3:4], axis=1, keepdims=True)) + lambda_init)

    def make_q(qi):
        q = q_ref[0, pl.ds(pl.multiple_of(qi * t, t), t), :]
        return jnp.concatenate([jnp.where(seg == i, q, jnp.zeros_like(q)) for i in range(4)], axis=0)

    def emit(qi, res):
        a = [acc * (1.0 / l) for acc, l in res]
        heads = []
        for hh in range(2):
            o = a[2 * hh] - lam * a[2 * hh + 1]
            heads.append(o * lax.rsqrt(jnp.mean(o * o, axis=0, keepdims=True) + EPS))
        o = jnp.concatenate(heads, axis=0).T
        o_ref[0, pl.ds(pl.multiple_of(qi * t, t), t), :] = (
            o * sg_ref[...] * (1.0 - lambda_init)).astype(BF16)

    v_rows = [(HEAD_DIM * (i // 2), HEAD_DIM * (i // 2 + 1)) for i in range(4)]
    _flash_head_pair(make_q, 4, v_rows, k_ref, vt_ref, s_ref, p_ref, nq, t, emit)


def _moba_attn_kernel(q_ref, bias_ref, k_ref, vt_ref, ind_ref, o_ref, s_ref, p_ref, *, t):
    hp = pl.program_id(1)
    nq = q_ref.shape[1] // t
    lane = lax.broadcasted_iota(jnp.int32, (t, LANES), 1)

    def make_q(qi):
        rows = pl.ds(pl.multiple_of(qi * t, t), t)
        q = q_ref[0, rows, :]
        bias = bias_ref[0, rows, :]
        strips = []
        for hh in range(2):
            qz = jnp.where(lane // HEAD_DIM == hh, q, jnp.zeros_like(q))
            bz = jnp.where(lane // GATE_SEG == 2 * hp + hh, bias, jnp.zeros_like(bias))
            strips.append(jnp.concatenate([qz, bz], axis=1))
        return jnp.concatenate(strips, axis=0)

    def emit(qi, res):
        o = jnp.concatenate([acc * (1.0 / l) for acc, l in res], axis=0)
        o_ref[0, pl.ds(pl.multiple_of(qi * t, t), t), :] = o.T.astype(BF16)

    v_rows = [(0, HEAD_DIM), (HEAD_DIM, 2 * HEAD_DIM)]
    _flash_head_pair(make_q, 2, v_rows, k_ref, vt_ref, s_ref, p_ref, nq, t, emit, k_aug_ref=ind_ref)


def _attention(qk, vt, bias, ind, lam_params, subln_tile, lambda_init):
    B, S, _ = qk.shape
    t = MOBA_BLOCK
    n_pairs = SECTION // LANES
    assert (S // t) % 2 == 0
    grid = (B, n_pairs)
    qk_spec = lambda sec: pl.BlockSpec((1, S, LANES), lambda b, p: (b, 0, sec * n_pairs + p))
    vt_spec = lambda sec: pl.BlockSpec((1, S // t, LANES, t), lambda b, p: (b, 0, sec * n_pairs + p, 0))
    out_spec = pl.BlockSpec((1, S, LANES), lambda b, p: (b, 0, p))
    params = pltpu.CompilerParams(
        dimension_semantics=("arbitrary", "arbitrary"),
        vmem_limit_bytes=VMEM_LIMIT_BYTES)
    out_shape = jax.ShapeDtypeStruct((B, S, SECTION), BF16)
    pipeline_scratch = lambda n: [pltpu.VMEM((3, t, n * t), F32), pltpu.VMEM((3, n, t, t), BF16)]

    diff = pl.pallas_call(
        functools.partial(_diff_attn_kernel, t=t, lambda_init=lambda_init),
        grid=grid,
        in_specs=[pl.BlockSpec((4, DIFF_HALF), lambda b, p: (0, 0)),
                  qk_spec(0), qk_spec(1), vt_spec(0),
                  pl.BlockSpec((1, LANES), lambda b, p: (0, 0))],
        out_specs=out_spec, out_shape=out_shape, compiler_params=params,
        scratch_shapes=pipeline_scratch(4),
        name="diff_attn",
    )(lam_params, qk, qk, vt, subln_tile)

    moba = pl.pallas_call(
        functools.partial(_moba_attn_kernel, t=t),
        grid=grid,
        in_specs=[qk_spec(2),
                  pl.BlockSpec((1, S, LANES), lambda b, p: (b, 0, 0)),
                  qk_spec(3), vt_spec(1),
                  pl.BlockSpec((S, LANES), lambda b, p: (0, 0))],
        out_specs=out_spec, out_shape=out_shape, compiler_params=params,
        scratch_shapes=pipeline_scratch(2),
        name="moba_attn",
    )(qk, bias, qk, vt, ind)
    return diff, moba


def _out_ffn_kernel(mixa_ref, mixb_ref, x_ref, wo_ref, g_ref, wi_ref, w2_ref, o_ref, *, d_ff, chunk):
    mix = jnp.concatenate([mixa_ref[0], mixb_ref[0]], axis=1)
    x1 = x_ref[0] + jnp.dot(mix, wo_ref[...], preferred_element_type=F32)
    hn = _rms(x1, g_ref[...]).astype(BF16)
    acc = x1
    for c in range(d_ff // chunk):
        lo = c * chunk
        g = jnp.dot(hn, wi_ref[:, lo:lo + chunk], preferred_element_type=F32)
        u = jnp.dot(hn, wi_ref[:, d_ff + lo:d_ff + lo + chunk], preferred_element_type=F32)
        a = (g * jax.nn.sigmoid(g) * u).astype(BF16)
        acc = acc + jnp.dot(a, w2_ref[lo:lo + chunk, :], preferred_element_type=F32)
    o_ref[0] = acc


def _out_ffn(mix_a, col_a, mix_b, col_b, x, w_out, norm_g, w_in, w2):
    B, S, D = x.shape
    d_ff = w2.shape[0]
    tm = 512
    half = w_out.shape[0] // 2
    const = lambda b, t: (0, 0)
    resident = lambda shape: pl.BlockSpec(shape, const, pipeline_mode=pl.Buffered(1))
    tile = pl.BlockSpec((1, tm, D), lambda b, t: (b, t, 0))
    mix_spec = lambda col: pl.BlockSpec((1, tm, half), lambda b, t: (b, t, col))
    return pl.pallas_call(
        functools.partial(_out_ffn_kernel, d_ff=d_ff, chunk=256),
        grid=(B, S // tm),
        in_specs=[mix_spec(col_a), mix_spec(col_b), tile, resident(w_out.shape), pl.BlockSpec((1, D), const),
                  resident(w_in.shape), resident(w2.shape)],
        out_specs=tile,
        out_shape=jax.ShapeDtypeStruct((B, S, D), F32),
        compiler_params=pltpu.CompilerParams(
            dimension_semantics=("arbitrary", "arbitrary"),
            vmem_limit_bytes=VMEM_LIMIT_BYTES),
        name="out_ffn",
    )(mix_a, mix_b, x, w_out, norm_g, w_in, w2)


def _odd_mix_kernel(x_ref, g_ref, w_ref, b_ref, lng_ref, lnb_ref, ws_ref, bs_ref,
                    cw_ref, cb_ref, clg_ref, clb_ref, o_ref, ext_ref, *, tm):
    ti = pl.program_id(1)
    h = _rms(x_ref[0], g_ref[...]).astype(BF16)

    def proj(lo, width):
        return jnp.dot(h, w_ref[:, lo:lo + width], preferred_element_type=F32) + b_ref[:, lo:lo + width]

    gu = jax.nn.gelu(proj(0, GMLP_WIDTH), approximate=True)
    gv = jax.nn.gelu(proj(GMLP_WIDTH, GMLP_WIDTH), approximate=True)
    gv = _layer_norm(gv, lng_ref[...], lnb_ref[...]).astype(BF16)
    T = GMLP_CHUNK
    tri = lax.broadcasted_iota(jnp.int32, (T, T), 1) <= lax.broadcasted_iota(jnp.int32, (T, T), 0)
    lane = lax.broadcasted_iota(jnp.int32, (T, LANES), 1)
    first = lane < (GMLP_WIDTH // GMLP_GROUPS)
    zero = jnp.zeros((T, LANES), BF16)
    for lb in range(GMLP_WIDTH // LANES):
        wpair = jnp.concatenate(
            [jnp.where(tri, ws_ref[2 * lb + i], 0.0).astype(BF16) for i in range(2)], axis=1)
        bs = bs_ref[:, lb * LANES:(lb + 1) * LANES]
        for c in range(tm // T):
            vc = gv[c * T:(c + 1) * T, lb * LANES:(lb + 1) * LANES]
            vpair = jnp.concatenate([jnp.where(first, vc, zero), jnp.where(first, zero, vc)], axis=0)
            sg = jnp.dot(wpair, vpair, preferred_element_type=F32) + bs
            o_ref[0, c * T:(c + 1) * T, lb * LANES:(lb + 1) * LANES] = (
                gu[c * T:(c + 1) * T, lb * LANES:(lb + 1) * LANES] * sg).astype(BF16)

    lo = 2 * GMLP_WIDTH
    cval = proj(lo, CONV_CH) * jax.nn.sigmoid(proj(lo + CONV_CH, CONV_CH))

    @pl.when(ti == 0)
    def _():
        ext_ref[0:CONV_HALO, :] = jnp.zeros((CONV_HALO, CONV_CH), F32)

    ext_ref[CONV_HALO:CONV_HALO + tm, :] = cval
    first_tap = CONV_HALO - (CONV_KERNEL - 1)
    span = tm + 8 * ((CONV_KERNEL - 1) // 8)
    conv = jnp.zeros((tm, CONV_CH), F32) + cb_ref[...]
    for r in range(8):
        taps = [k for k in range(CONV_KERNEL) if (first_tap + k) % 8 == r]
        base = first_tap + taps[0]
        rows = ext_ref[base:base + span, :]
        for k in taps:
            off = first_tap + k - base
            conv = conv + rows[off:off + tm, :] * cw_ref[k:k + 1, :]
    ext_ref[0:CONV_HALO, :] = ext_ref[tm:tm + CONV_HALO, :]
    dn = _layer_norm(conv, clg_ref[...], clb_ref[...])
    o_ref[0, :, GMLP_WIDTH:GMLP_WIDTH + CONV_CH] = (dn * jax.nn.sigmoid(dn)).astype(BF16)


def _odd_mix(x, norm_g, w_in, b_in, ln_g, ln_b, w_s, bs_tile, conv_w, conv_b, cln_g, cln_b):
    B, S, D = x.shape
    tm = 512
    const2 = lambda b, t: (0, 0)
    full = lambda a: pl.BlockSpec(a.shape, (lambda b, t: (0,) * a.ndim))
    n_out = GMLP_WIDTH + CONV_CH
    span = tm + 8 * ((CONV_KERNEL - 1) // 8)
    args = (x, norm_g, w_in, b_in, ln_g, ln_b, w_s, bs_tile, conv_w, conv_b, cln_g, cln_b)
    return pl.pallas_call(
        functools.partial(_odd_mix_kernel, tm=tm),
        grid=(B, S // tm),
        in_specs=[pl.BlockSpec((1, tm, D), lambda b, t: (b, t, 0))] + [full(a) for a in args[1:]],
        out_specs=pl.BlockSpec((1, tm, n_out), lambda b, t: (b, t, 0)),
        out_shape=jax.ShapeDtypeStruct((B, S, n_out), BF16),
        scratch_shapes=[pltpu.VMEM((span + 8, CONV_CH), F32)],
        compiler_params=pltpu.CompilerParams(
            dimension_semantics=("arbitrary", "arbitrary"),
            vmem_limit_bytes=VMEM_LIMIT_BYTES),
        name="odd_mix",
    )(*args)


def _rope_tables(seq, dim, seg_per_block):
    inv = 1.0 / (ROPE_THETA ** (jnp.arange(0, dim, 2, dtype=F32) / dim))
    ang = jnp.arange(seq, dtype=F32)[:, None] * inv[None, :]
    cos, sin = jnp.cos(ang), jnp.sin(ang)
    cos = jnp.tile(jnp.concatenate([cos, cos], axis=1), (1, seg_per_block))
    sin = jnp.tile(jnp.concatenate([-sin, sin], axis=1), (1, seg_per_block))
    return cos, sin


def _block_diag_ones(seg):
    idx = jnp.arange(SECTION) // seg
    return (idx[:, None] == idx[None, :]).astype(BF16)


def kernel(x, attn_norm_g, ffn_norm_g, ffn_w_in, ffn_w_out, even_w_in, even_w_out,
           diff_q_norm_g, diff_k_norm_g, diff_lambda_q1, diff_lambda_k1,
           diff_lambda_q2, diff_lambda_k2, diff_subln_g, moba_q_norm_g, moba_k_norm_g,
           odd_w_in, odd_b_in, odd_w_out, gmlp_ln_g, gmlp_ln_b, gmlp_w_s, gmlp_b_s,
           conv_w, conv_b, conv_ln_g, conv_ln_b):
    B, S, D = x.shape
    assert S % 512 == 0 and S // MOBA_BLOCK <= GATE_SEG
    row = lambda v: v.reshape(1, -1).astype(F32)

    lambda_init = 0.8 - 0.6 * math.exp(-0.3 * 0)
    cosd, sind = _rope_tables(S, DIFF_HALF, LANES // DIFF_HALF)
    cosm, sinm = _rope_tables(S, HEAD_DIM, LANES // HEAD_DIM)
    gains = jnp.stack([
        jnp.tile(diff_q_norm_g[0], SECTION // DIFF_HALF),
        jnp.tile(diff_k_norm_g[0], SECTION // DIFF_HALF),
        jnp.tile(moba_q_norm_g[0], SECTION // HEAD_DIM),
        jnp.tile(moba_k_norm_g[0], SECTION // HEAD_DIM)]).astype(F32)
    qk, vt, bias = _even_proj(x, row(attn_norm_g[0]), even_w_in[0].astype(BF16),
                              _block_diag_ones(DIFF_HALF), _block_diag_ones(HEAD_DIM), gains,
                              cosd, sind, cosm, sinm)
    key_block = jnp.arange(S)[:, None] // MOBA_BLOCK
    ind = (key_block == (jnp.arange(LANES)[None, :] % GATE_SEG)).astype(BF16)
    lam_params = jnp.stack([diff_lambda_q1[0], diff_lambda_k1[0],
                            diff_lambda_q2[0], diff_lambda_k2[0]]).astype(F32)
    subln_tile = jnp.tile(diff_subln_g[0], LANES // HEAD_DIM).reshape(1, LANES).astype(F32)
    diff, moba = _attention(qk, vt, bias, ind, lam_params, subln_tile, lambda_init)
    x = _out_ffn(diff, 0, moba, 0, x, even_w_out[0].astype(BF16), row(ffn_norm_g[0]),
                 ffn_w_in[0].astype(BF16), ffn_w_out[0].astype(BF16))

    bs_tile = jnp.repeat(gmlp_b_s[0].T, GMLP_WIDTH // GMLP_GROUPS, axis=1).astype(F32)
    mix = _odd_mix(x, row(attn_norm_g[1]), odd_w_in[0].astype(BF16), row(odd_b_in[0]),
                   row(gmlp_ln_g[0]), row(gmlp_ln_b[0]), gmlp_w_s[0].astype(F32), bs_tile,
                   conv_w[0].astype(F32), row(conv_b[0]), row(conv_ln_g[0]), row(conv_ln_b[0]))
    x = _out_ffn(mix, 0, mix, 1, x, odd_w_out[0].astype(BF16), row(ffn_norm_g[1]),
                 ffn_w_in[1].astype(BF16), ffn_w_out[1].astype(BF16))
    return x
```

```python
import functools
import math

import jax
import jax.numpy as jnp
from jax import lax
from jax.experimental import pallas as pl
from jax.experimental.pallas import tpu as pltpu

F32 = jnp.float32
BF16 = jnp.bfloat16

LANES = 128
VMEM_LIMIT_BYTES = 56 * 1024 * 1024

HEAD_DIM = 64
DIFF_HALF = 32
N_HEADS_DIFF = 8
N_HEADS_MOBA = 8
MOBA_BLOCK = 256
MOBA_TOPK = 3
GMLP_CHUNK = 128
GMLP_GROUPS = 8
GMLP_WIDTH = 512
CONV_CH = 512
CONV_KERNEL = 31
ROPE_THETA = 10000.0
EPS = 1e-6

SECTION = 512
BD = 256
EVEN_PROJ_ROWS = 1024
GATE_SEG = LANES // N_HEADS_MOBA
CONV_HALO = 32
MASKED_BIAS = -1e30
M_INIT = -1e29
LOG2E = math.log2(math.e)

_NT = (((1,), (1,)), ((), ()))


def _rms(x, g):
    return x * lax.rsqrt(jnp.mean(x * x, axis=-1, keepdims=True) + EPS) * g


def _layer_norm(x, g, b):
    mu = jnp.mean(x, axis=-1, keepdims=True)
    xc = x - mu
    return xc * lax.rsqrt(jnp.mean(xc * xc, axis=-1, keepdims=True) + EPS) * g + b


def _seg_allreduce(x, op, seg, lane):
    s = 1
    while s < seg:
        up = pltpu.roll(x, LANES - s, 1)
        dn = pltpu.roll(x, s, 1)
        x = op(x, jnp.where((lane & s) == 0, up, dn))
        s *= 2
    return x


def _even_proj_kernel(x_ref, g_ref, w_ref, bdd_ref, bdm_ref, pd_ref, pm_ref, gain_ref,
                      cosd_ref, sind_ref, cosm_ref, sinm_ref,
                      qk_ref, vt_ref, bias_ref, kbt_ref, *, tm):
    blocks = tm // MOBA_BLOCK
    first_blk = pl.program_id(1) * blocks
    h = _rms(x_ref[0], g_ref[...]).astype(BF16)
    lane = lax.broadcasted_iota(jnp.int32, (tm, LANES), 1)

    def tiles_dot(a, m_ref):
        return jnp.concatenate(
            [jnp.dot(a[:, c * BD:(c + 1) * BD], m_ref[...], preferred_element_type=F32)
             for c in range(SECTION // BD)], axis=1)

    def qk_section(sec, bd_ref, perm_ref, seg, cos_ref, sin_ref):
        y = jnp.dot(h, w_ref[:, sec * SECTION:(sec + 1) * SECTION], preferred_element_type=F32)
        ss = tiles_dot((y * y).astype(BF16), bd_ref)
        yn = y * lax.rsqrt(ss * (1.0 / seg) + EPS) * gain_ref[sec_row[sec]:sec_row[sec] + 1, :]
        rot = tiles_dot(yn.astype(BF16), perm_ref)
        cos = cos_ref[...]
        sin = sin_ref[...]
        return [yn[:, c * LANES:(c + 1) * LANES] * cos + rot[:, c * LANES:(c + 1) * LANES] * sin
                for c in range(SECTION // LANES)]

    def store(out_sec, chunks, scale):
        for c, yc in enumerate(chunks):
            lo = out_sec * SECTION + c * LANES
            qk_ref[0, :, lo:lo + LANES] = (yc * scale).astype(BF16)

    sec_row = {0: 0, 1: 1, 3: 2, 4: 3}
    store(0, qk_section(0, bdd_ref, pd_ref, DIFF_HALF, cosd_ref, sind_ref), DIFF_HALF ** -0.5 * LOG2E)
    store(1, qk_section(1, bdd_ref, pd_ref, DIFF_HALF, cosd_ref, sind_ref), 1.0)
    mq = qk_section(3, bdm_ref, pm_ref, HEAD_DIM, cosm_ref, sinm_ref)
    store(2, mq, HEAD_DIM ** -0.5 * LOG2E)
    mk = qk_section(4, bdm_ref, pm_ref, HEAD_DIM, cosm_ref, sinm_ref)
    store(3, mk, 1.0)
    for out_sec, sec in enumerate((2, 5)):
        lo = sec * SECTION
        vt = jnp.dot(h, w_ref[:, lo:lo + SECTION], preferred_element_type=F32).T.astype(BF16)
        for sb in range(blocks):
            vt_ref[0, sb, out_sec * SECTION:(out_sec + 1) * SECTION, :] = (
                vt[:, sb * MOBA_BLOCK:(sb + 1) * MOBA_BLOCK])

    @pl.when(first_blk == 0)
    def _():
        kbt_ref[...] = jnp.zeros_like(kbt_ref)

    mk_full = jnp.concatenate(mk, axis=1)
    head_of_lane = lax.broadcasted_iota(jnp.int32, (1, SECTION), 1) // HEAD_DIM
    for sb in range(blocks):
        kbar = jnp.mean(mk_full[sb * MOBA_BLOCK:(sb + 1) * MOBA_BLOCK], axis=0, keepdims=True)
        for hh in range(N_HEADS_MOBA):
            kbt_ref[pl.ds(hh * GATE_SEG + first_blk + sb, 1), :] = jnp.where(head_of_lane == hh, kbar, 0.0)

    def split(a):
        hi = a.astype(BF16)
        return hi, (a - hi.astype(F32)).astype(BF16)

    q_hi, q_lo = split(jnp.concatenate(mq, axis=1))
    k_hi, k_lo = split(kbt_ref[...])
    gate = lax.dot_general(jnp.concatenate([q_hi, q_lo, q_hi], axis=1),
                           jnp.concatenate([k_hi, k_hi, k_lo], axis=1), _NT,
                           preferred_element_type=F32)
    blk = lane & (GATE_SEG - 1)
    blk_f = blk.astype(F32)
    neg_inf = jnp.float32(-jnp.inf)
    own = first_blk + lax.broadcasted_iota(jnp.int32, (tm, LANES), 0) // MOBA_BLOCK
    avail = jnp.where(blk < own, gate, neg_inf)
    chosen = blk == own
    for _ in range(MOBA_TOPK):
        best = _seg_allreduce(avail, jnp.maximum, GATE_SEG, lane)
        cand = jnp.logical_and(avail == best, avail > neg_inf)
        first = _seg_allreduce(jnp.where(cand, blk_f, float(GATE_SEG)), jnp.minimum, GATE_SEG, lane)
        pick = blk_f == first
        chosen = jnp.logical_or(chosen, pick)
        avail = jnp.where(pick, neg_inf, avail)
    bias_ref[0] = jnp.where(chosen, 0.0, MASKED_BIAS).astype(BF16)


def _even_proj(x, norm_g, w_in, bdd, bdm, pd, pm, gains, cosd, sind, cosm, sinm):
    B, S, D = x.shape
    tm = EVEN_PROJ_ROWS
    blocks = tm // MOBA_BLOCK
    n_out = w_in.shape[1]
    n_qk = 4 * SECTION
    n_v = 2 * SECTION
    const = lambda b, t: (0, 0)
    tab = pl.BlockSpec((tm, LANES), lambda b, t: (t, 0))
    return pl.pallas_call(
        functools.partial(_even_proj_kernel, tm=tm),
        grid=(B, S // tm),
        in_specs=[
            pl.BlockSpec((1, tm, D), lambda b, t: (b, t, 0)),
            pl.BlockSpec((1, D), const),
            pl.BlockSpec((D, n_out), const, pipeline_mode=pl.Buffered(1)),
            pl.BlockSpec((BD, BD), const),
            pl.BlockSpec((BD, BD), const),
            pl.BlockSpec((BD, BD), const),
            pl.BlockSpec((BD, BD), const),
            pl.BlockSpec((4, SECTION), const),
            tab, tab, tab, tab,
        ],
        out_specs=[
            pl.BlockSpec((1, tm, n_qk), lambda b, t: (b, t, 0)),
            pl.BlockSpec((1, blocks, n_v, MOBA_BLOCK), lambda b, t: (b, t, 0, 0)),
            pl.BlockSpec((1, tm, LANES), lambda b, t: (b, t, 0)),
        ],
        out_shape=[
            jax.ShapeDtypeStruct((B, S, n_qk), BF16),
            jax.ShapeDtypeStruct((B, S // MOBA_BLOCK, n_v, MOBA_BLOCK), BF16),
            jax.ShapeDtypeStruct((B, S, LANES), BF16),
        ],
        scratch_shapes=[pltpu.VMEM((LANES, SECTION), F32)],
        compiler_params=pltpu.CompilerParams(
            dimension_semantics=("arbitrary", "arbitrary"),
            vmem_limit_bytes=VMEM_LIMIT_BYTES),
        name="even_proj",
    )(x, norm_g, w_in, bdd, bdm, pd, pm, gains, cosd, sind, cosm, sinm)


def _flash_head_pair(make_q, n, v_rows, k_ref, vt_ref, s_ref, p_ref, nq, t, emit, k_aug_ref=None):
    def scores(q_all, j):
        start = pl.multiple_of(j * t, t)
        k = k_ref[0, pl.ds(start, t), :]
        if k_aug_ref is not None:
            k = jnp.concatenate([k, k_aug_ref[pl.ds(start, t), :]], axis=1)
        return lax.dot_general(k, q_all, _NT, preferred_element_type=F32)

    def softmax(s_all, stats, diagonal):
        ps, new_stats, alphas = [], [], []
        for i in range(n):
            m, l = stats[i]
            s = s_all[:, i * t:(i + 1) * t]
            if diagonal:
                key = lax.broadcasted_iota(jnp.int32, (t, t), 0)
                qry = lax.broadcasted_iota(jnp.int32, (t, t), 1)
                s = jnp.where(key <= qry, s, -jnp.inf)
            m_new = jnp.maximum(m, jnp.max(s, axis=0, keepdims=True))
            p = jnp.exp2(s - m_new)
            alpha = jnp.exp2(m - m_new)
            ps.append(p.astype(BF16))
            new_stats.append((m_new, alpha * l + jnp.sum(p, axis=0, keepdims=True)))
            alphas.append(alpha)
        return tuple(ps), tuple(new_stats), alphas

    def weighted_values(j, ps):
        vt = vt_ref[0, j]
        return [jnp.dot(vt[lo:hi], p, preferred_element_type=F32) for (lo, hi), p in zip(v_rows, ps)]

    def step(q_all, j, cur, nxt, carry):
        stats, accs = carry
        s_ref[nxt] = scores(q_all, j + 1)
        p_prev = tuple(p_ref[nxt, i] for i in range(n))
        pv = weighted_values(jnp.maximum(j - 1, 0), p_prev)
        pv = [jnp.where(j > 0, x, 0.0) for x in pv]
        p_cur, stats, alphas = softmax(s_ref[cur], stats, False)
        for i in range(n):
            p_ref[cur, i] = p_cur[i]
        accs = tuple(a * (acc + x) for a, acc, x in zip(alphas, accs, pv))
        return stats, accs

    def query_tile(qi, first, second, odd, next_first):
        q_all = make_q(qi)
        init = (tuple((jnp.full((1, t), M_INIT, F32), jnp.zeros((1, t), F32)) for _ in range(n)),
                tuple(jnp.zeros((hi - lo, t), F32) for lo, hi in v_rows))
        carry = lax.fori_loop(
            0, qi // 2,
            lambda i, c: step(q_all, 2 * i + 1, second, first, step(q_all, 2 * i, first, second, c)), init)
        if odd:
            carry = step(q_all, qi - 1, first, second, carry)
        diag, prev = (second, first) if odd else (first, second)
        stats, accs = carry
        s_ref[next_first] = scores(make_q(jnp.minimum(qi + 1, nq - 1)), 0)
        pv_prev = weighted_values(jnp.maximum(qi - 1, 0), tuple(p_ref[prev, i] for i in range(n)))
        pv_prev = [jnp.where(qi > 0, x, 0.0) for x in pv_prev]
        p_cur, stats, alphas = softmax(s_ref[diag], stats, True)
        pv_cur = weighted_values(qi, p_cur)
        emit(qi, [(a * (acc + x) + y, l)
                  for a, acc, x, y, (_, l) in zip(alphas, accs, pv_prev, pv_cur, stats)])

    p_ref[...] = jnp.zeros(p_ref.shape, BF16)
    s_ref[0] = scores(make_q(0), 0)

    def pair_of_tiles(u, _):
        query_tile(2 * u, 0, 1, False, 2)
        query_tile(2 * u + 1, 2, 1, True, 0)
        return 0

    lax.fori_loop(0, nq // 2, pair_of_tiles, 0)


def _diff_attn_kernel(lam_ref, q_ref, k_ref, vt_ref, sg_ref, o_ref, s_ref, p_ref, *, t, lambda_init):
    nq = q_ref.shape[1] // t
    seg = lax.broadcasted_iota(jnp.int32, (t, LANES), 1) // DIFF_HALF
    lam_p = lam_ref[...]
    lam = (jnp.exp(jnp.sum(lam_p[0:1] * lam_p[1:2], axis=1, keepdims=True))
           - jnp.exp(jnp.sum(lam_p[2:3] * lam_p[3:4], axis=1, keepdims=True)) + lambda_init)

    def make_q(qi):
        q = q_ref[0, pl.ds(pl.multiple_of(qi * t, t), t), :]
        return jnp.concatenate([jnp.where(seg == i, q, jnp.zeros_like(q)) for i in range(4)], axis=0)

    def emit(qi, res):
        a = [acc * (1.0 / l) for acc, l in res]
        heads = []
        for hh in range(2):
            o = a[2 * hh] - lam * a[2 * hh + 1]
            heads.append(o * lax.rsqrt(jnp.mean(o * o, axis=0, keepdims=True) + EPS))
        o = jnp.concatenate(heads, axis=0).T
        o_ref[0, pl.ds(pl.multiple_of(qi * t, t), t), :] = (
            o * sg_ref[...] * (1.0 - lambda_init)).astype(BF16)

    v_rows = [(HEAD_DIM * (i // 2), HEAD_DIM * (i // 2 + 1)) for i in range(4)]
    _flash_head_pair(make_q, 4, v_rows, k_ref, vt_ref, s_ref, p_ref, nq, t, emit)


def _moba_attn_kernel(q_ref, bias_ref, k_ref, vt_ref, ind_ref, o_ref, s_ref, p_ref, *, t):
    hp = pl.program_id(1)
    nq = q_ref.shape[1] // t
    lane = lax.broadcasted_iota(jnp.int32, (t, LANES), 1)

    def make_q(qi):
        rows = pl.ds(pl.multiple_of(qi * t, t), t)
        q = q_ref[0, rows, :]
        bias = bias_ref[0, rows, :]
        strips = []
        for hh in range(2):
            qz = jnp.where(lane // HEAD_DIM == hh, q, jnp.zeros_like(q))
            bz = jnp.where(lane // GATE_SEG == 2 * hp + hh, bias, jnp.zeros_like(bias))
            strips.append(jnp.concatenate([qz, bz], axis=1))
        return jnp.concatenate(strips, axis=0)

    def emit(qi, res):
        o = jnp.concatenate([acc * (1.0 / l) for acc, l in res], axis=0)
        o_ref[0, pl.ds(pl.multiple_of(qi * t, t), t), :] = o.T.astype(BF16)

    v_rows = [(0, HEAD_DIM), (HEAD_DIM, 2 * HEAD_DIM)]
    _flash_head_pair(make_q, 2, v_rows, k_ref, vt_ref, s_ref, p_ref, nq, t, emit, k_aug_ref=ind_ref)


def _attention(qk, vt, bias, ind, lam_params, subln_tile, lambda_init):
    B, S, _ = qk.shape
    t = MOBA_BLOCK
    n_pairs = SECTION // LANES
    assert (S // t) % 2 == 0
    grid = (B, n_pairs)
    qk_spec = lambda sec: pl.BlockSpec((1, S, LANES), lambda b, p: (b, 0, sec * n_pairs + p))
    vt_spec = lambda sec: pl.BlockSpec((1, S // t, LANES, t), lambda b, p: (b, 0, sec * n_pairs + p, 0))
    out_spec = pl.BlockSpec((1, S, LANES), lambda b, p: (b, 0, p))
    params = pltpu.CompilerParams(
        dimension_semantics=("arbitrary", "arbitrary"),
        vmem_limit_bytes=VMEM_LIMIT_BYTES)
    out_shape = jax.ShapeDtypeStruct((B, S, SECTION), BF16)
    pipeline_scratch = lambda n: [pltpu.VMEM((3, t, n * t), F32), pltpu.VMEM((3, n, t, t), BF16)]

    diff = pl.pallas_call(
        functools.partial(_diff_attn_kernel, t=t, lambda_init=lambda_init),
        grid=grid,
        in_specs=[pl.BlockSpec((4, DIFF_HALF), lambda b, p: (0, 0)),
                  qk_spec(0), qk_spec(1), vt_spec(0),
                  pl.BlockSpec((1, LANES), lambda b, p: (0, 0))],
        out_specs=out_spec, out_shape=out_shape, compiler_params=params,
        scratch_shapes=pipeline_scratch(4),
        name="diff_attn",
    )(lam_params, qk, qk, vt, subln_tile)

    moba = pl.pallas_call(
        functools.partial(_moba_attn_kernel, t=t),
        grid=grid,
        in_specs=[qk_spec(2),
                  pl.BlockSpec((1, S, LANES), lambda b, p: (b, 0, 0)),
                  qk_spec(3), vt_spec(1),
                  pl.BlockSpec((S, LANES), lambda b, p: (0, 0))],
        out_specs=out_spec, out_shape=out_shape, compiler_params=params,
        scratch_shapes=pipeline_scratch(2),
        name="moba_attn",
    )(qk, bias, qk, vt, ind)
    return diff, moba


def _out_ffn_kernel(mixa_ref, mixb_ref, x_ref, wo_ref, g_ref, wi_ref, w2_ref, o_ref, *, d_ff, chunk):
    mix = jnp.concatenate([mixa_ref[0], mixb_ref[0]], axis=1)
    x1 = x_ref[0] + jnp.dot(mix, wo_ref[...], preferred_element_type=F32)
    hn = _rms(x1, g_ref[...]).astype(BF16)
    acc = x1
    for c in range(d_ff // chunk):
        lo = c * chunk
        g = jnp.dot(hn, wi_ref[0, :, lo:lo + chunk], preferred_element_type=F32)
        u = jnp.dot(hn, wi_ref[0, :, d_ff + lo:d_ff + lo + chunk], preferred_element_type=F32)
        a = (g * jax.nn.sigmoid(g) * u).astype(BF16)
        acc = acc + jnp.dot(a, w2_ref[0, lo:lo + chunk, :], preferred_element_type=F32)
    o_ref[0] = acc


def _out_ffn(mix_a, col_a, mix_b, col_b, x, w_out, norm_g, w_in, w2, layer):
    B, S, D = x.shape
    d_ff = w2.shape[1]
    tm = 512
    half = w_out.shape[0] // 2
    const = lambda b, t: (0, 0)
    resident = lambda shape: pl.BlockSpec(shape, const, pipeline_mode=pl.Buffered(1))
    slab = lambda a: pl.BlockSpec((1,) + a.shape[1:], lambda b, t: (layer, 0, 0), pipeline_mode=pl.Buffered(1))
    tile = pl.BlockSpec((1, tm, D), lambda b, t: (b, t, 0))
    mix_spec = lambda col: pl.BlockSpec((1, tm, half), lambda b, t: (b, t, col))
    return pl.pallas_call(
        functools.partial(_out_ffn_kernel, d_ff=d_ff, chunk=256),
        grid=(B, S // tm),
        in_specs=[mix_spec(col_a), mix_spec(col_b), tile, resident(w_out.shape), pl.BlockSpec((1, D), const),
                  slab(w_in), slab(w2)],
        out_specs=tile,
        out_shape=jax.ShapeDtypeStruct((B, S, D), F32),
        compiler_params=pltpu.CompilerParams(
            dimension_semantics=("arbitrary", "arbitrary"),
            vmem_limit_bytes=VMEM_LIMIT_BYTES),
        name="out_ffn",
    )(mix_a, mix_b, x, w_out, norm_g, w_in, w2)


def _odd_mix_kernel(x_ref, g_ref, w_ref, b_ref, lng_ref, lnb_ref, ws_ref, bs_ref,
                    cw_ref, cb_ref, clg_ref, clb_ref, o_ref, ext_ref, *, tm):
    ti = pl.program_id(1)
    h = _rms(x_ref[0], g_ref[...]).astype(BF16)

    def proj(lo, width):
        return jnp.dot(h, w_ref[:, lo:lo + width], preferred_element_type=F32) + b_ref[:, lo:lo + width]

    gu = jax.nn.gelu(proj(0, GMLP_WIDTH), approximate=True)
    gv = jax.nn.gelu(proj(GMLP_WIDTH, GMLP_WIDTH), approximate=True)
    gv = _layer_norm(gv, lng_ref[...], lnb_ref[...]).astype(BF16)
    T = GMLP_CHUNK
    tri = lax.broadcasted_iota(jnp.int32, (T, T), 1) <= lax.broadcasted_iota(jnp.int32, (T, T), 0)
    lane = lax.broadcasted_iota(jnp.int32, (T, LANES), 1)
    first = lane < (GMLP_WIDTH // GMLP_GROUPS)
    zero = jnp.zeros((T, LANES), BF16)
    for lb in range(GMLP_WIDTH // LANES):
        wpair = jnp.concatenate(
            [jnp.where(tri, ws_ref[2 * lb + i], 0.0).astype(BF16) for i in range(2)], axis=1)
        bs = bs_ref[:, lb * LANES:(lb + 1) * LANES]
        for c in range(tm // T):
            vc = gv[c * T:(c + 1) * T, lb * LANES:(lb + 1) * LANES]
            vpair = jnp.concatenate([jnp.where(first, vc, zero), jnp.where(first, zero, vc)], axis=0)
            sg = jnp.dot(wpair, vpair, preferred_element_type=F32) + bs
            o_ref[0, c * T:(c + 1) * T, lb * LANES:(lb + 1) * LANES] = (
                gu[c * T:(c + 1) * T, lb * LANES:(lb + 1) * LANES] * sg).astype(BF16)

    lo = 2 * GMLP_WIDTH
    cval = proj(lo, CONV_CH) * jax.nn.sigmoid(proj(lo + CONV_CH, CONV_CH))

    @pl.when(ti == 0)
    def _():
        ext_ref[0:CONV_HALO, :] = jnp.zeros((CONV_HALO, CONV_CH), F32)

    ext_ref[CONV_HALO:CONV_HALO + tm, :] = cval
    first_tap = CONV_HALO - (CONV_KERNEL - 1)
    span = tm + 8 * ((CONV_KERNEL - 1) // 8)
    conv = jnp.zeros((tm, CONV_CH), F32) + cb_ref[...]
    for r in range(8):
        taps = [k for k in range(CONV_KERNEL) if (first_tap + k) % 8 == r]
        base = first_tap + taps[0]
        rows = ext_ref[base:base + span, :]
        for k in taps:
            off = first_tap + k - base
            conv = conv + rows[off:off + tm, :] * cw_ref[k:k + 1, :]
    ext_ref[0:CONV_HALO, :] = ext_ref[tm:tm + CONV_HALO, :]
    dn = _layer_norm(conv, clg_ref[...], clb_ref[...])
    o_ref[0, :, GMLP_WIDTH:GMLP_WIDTH + CONV_CH] = (dn * jax.nn.sigmoid(dn)).astype(BF16)


def _odd_mix(x, norm_g, w_in, b_in, ln_g, ln_b, w_s, bs_tile, conv_w, conv_b, cln_g, cln_b):
    B, S, D = x.shape
    tm = 512
    const2 = lambda b, t: (0, 0)
    full = lambda a: pl.BlockSpec(a.shape, (lambda b, t: (0,) * a.ndim))
    n_out = GMLP_WIDTH + CONV_CH
    span = tm + 8 * ((CONV_KERNEL - 1) // 8)
    args = (x, norm_g, w_in, b_in, ln_g, ln_b, w_s, bs_tile, conv_w, conv_b, cln_g, cln_b)
    return pl.pallas_call(
        functools.partial(_odd_mix_kernel, tm=tm),
        grid=(B, S // tm),
        in_specs=[pl.BlockSpec((1, tm, D), lambda b, t: (b, t, 0))] + [full(a) for a in args[1:]],
        out_specs=pl.BlockSpec((1, tm, n_out), lambda b, t: (b, t, 0)),
        out_shape=jax.ShapeDtypeStruct((B, S, n_out), BF16),
        scratch_shapes=[pltpu.VMEM((span + 8, CONV_CH), F32)],
        compiler_params=pltpu.CompilerParams(
            dimension_semantics=("arbitrary", "arbitrary"),
            vmem_limit_bytes=VMEM_LIMIT_BYTES),
        name="odd_mix",
    )(*args)


def _rope_tables(seq, dim, seg_per_block):
    inv = 1.0 / (ROPE_THETA ** (jnp.arange(0, dim, 2, dtype=F32) / dim))
    ang = jnp.arange(seq, dtype=F32)[:, None] * inv[None, :]
    cos, sin = jnp.cos(ang), jnp.sin(ang)
    cos = jnp.tile(jnp.concatenate([cos, cos], axis=1), (1, seg_per_block))
    sin = jnp.tile(jnp.concatenate([-sin, sin], axis=1), (1, seg_per_block))
    return cos, sin


def _block_diag_ones(seg):
    idx = jnp.arange(BD) // seg
    return (idx[:, None] == idx[None, :]).astype(BF16)


def _rotate_half_matrix(seg):
    dst = jnp.arange(BD)
    src = jnp.where(dst % seg < seg // 2, dst + seg // 2, dst - seg // 2)
    return (jnp.arange(BD)[:, None] == src[None, :]).astype(BF16)


def kernel(x, attn_norm_g, ffn_norm_g, ffn_w_in, ffn_w_out, even_w_in, even_w_out,
           diff_q_norm_g, diff_k_norm_g, diff_lambda_q1, diff_lambda_k1,
           diff_lambda_q2, diff_lambda_k2, diff_subln_g, moba_q_norm_g, moba_k_norm_g,
           odd_w_in, odd_b_in, odd_w_out, gmlp_ln_g, gmlp_ln_b, gmlp_w_s, gmlp_b_s,
           conv_w, conv_b, conv_ln_g, conv_ln_b):
    B, S, D = x.shape
    assert S % EVEN_PROJ_ROWS == 0 and S // MOBA_BLOCK <= GATE_SEG
    row = lambda v: v.reshape(1, -1).astype(F32)
    ffn_w_in_bf = ffn_w_in.astype(BF16)
    ffn_w_out_bf = ffn_w_out.astype(BF16)

    lambda_init = 0.8 - 0.6 * math.exp(-0.3 * 0)
    cosd, sind = _rope_tables(S, DIFF_HALF, LANES // DIFF_HALF)
    cosm, sinm = _rope_tables(S, HEAD_DIM, LANES // HEAD_DIM)
    gains = jnp.stack([
        jnp.tile(diff_q_norm_g[0], SECTION // DIFF_HALF),
        jnp.tile(diff_k_norm_g[0], SECTION // DIFF_HALF),
        jnp.tile(moba_q_norm_g[0], SECTION // HEAD_DIM),
        jnp.tile(moba_k_norm_g[0], SECTION // HEAD_DIM)]).astype(F32)
    qk, vt, bias = _even_proj(x, row(attn_norm_g[0]), even_w_in[0].astype(BF16),
                              _block_diag_ones(DIFF_HALF), _block_diag_ones(HEAD_DIM),
                              _rotate_half_matrix(DIFF_HALF), _rotate_half_matrix(HEAD_DIM), gains,
                              cosd, sind, cosm, sinm)
    key_block = jnp.arange(S)[:, None] // MOBA_BLOCK
    ind = (key_block == (jnp.arange(LANES)[None, :] % GATE_SEG)).astype(BF16)
    lam_params = jnp.stack([diff_lambda_q1[0], diff_lambda_k1[0],
                            diff_lambda_q2[0], diff_lambda_k2[0]]).astype(F32)
    subln_tile = jnp.tile(diff_subln_g[0], LANES // HEAD_DIM).reshape(1, LANES).astype(F32)
    diff, moba = _attention(qk, vt, bias, ind, lam_params, subln_tile, lambda_init)
    x = _out_ffn(diff, 0, moba, 0, x, even_w_out[0].astype(BF16), row(ffn_norm_g[0]),
                 ffn_w_in_bf, ffn_w_out_bf, 0)

    bs_tile = jnp.repeat(gmlp_b_s[0].T, GMLP_WIDTH // GMLP_GROUPS, axis=1).astype(F32)
    mix = _odd_mix(x, row(attn_norm_g[1]), odd_w_in[0].astype(BF16), row(odd_b_in[0]),
                   row(gmlp_ln_g[0]), row(gmlp_ln_b[0]), gmlp_w_s[0].astype(F32), bs_tile,
                   conv_w[0].astype(F32), row(conv_b[0]), row(conv_ln_g[0]), row(conv_ln_b[0]))
    x = _out_ffn(mix, 0, mix, 1, x, odd_w_out[0].astype(BF16), row(ffn_norm_g[1]),
                 ffn_w_in_bf, ffn_w_out_bf, 1)
    return x
```

```python
import functools
import math

import jax
import jax.numpy as jnp
from jax import lax
from jax.experimental import pallas as pl
from jax.experimental.pallas import tpu as pltpu

F32 = jnp.float32
BF16 = jnp.bfloat16

LANES = 128
VMEM_LIMIT_BYTES = 56 * 1024 * 1024

HEAD_DIM = 64
DIFF_HALF = 32
N_HEADS_DIFF = 8
N_HEADS_MOBA = 8
MOBA_BLOCK = 256
MOBA_TOPK = 3
GMLP_CHUNK = 128
GMLP_GROUPS = 8
GMLP_WIDTH = 512
CONV_CH = 512
CONV_KERNEL = 31
ROPE_THETA = 10000.0
EPS = 1e-6

SECTION = 512
BD = 256
EVEN_PROJ_ROWS = 1024
ATTN_STREAMS = 2
GATE_SEG = LANES // N_HEADS_MOBA
CONV_HALO = 32
MASKED_BIAS = -1e30
M_INIT = -1e29
LOG2E = math.log2(math.e)

_NT = (((1,), (1,)), ((), ()))


def _rms(x, g):
    return x * lax.rsqrt(jnp.mean(x * x, axis=-1, keepdims=True) + EPS) * g


def _layer_norm(x, g, b):
    mu = jnp.mean(x, axis=-1, keepdims=True)
    xc = x - mu
    return xc * lax.rsqrt(jnp.mean(xc * xc, axis=-1, keepdims=True) + EPS) * g + b


def _seg_allreduce(x, op, seg, lane):
    s = 1
    while s < seg:
        up = pltpu.roll(x, LANES - s, 1)
        dn = pltpu.roll(x, s, 1)
        x = op(x, jnp.where((lane & s) == 0, up, dn))
        s *= 2
    return x


def _even_proj_kernel(x_ref, g_ref, w_ref, bdd_ref, bdm_ref, pd_ref, pm_ref, gain_ref,
                      cosd_ref, sind_ref, cosm_ref, sinm_ref,
                      qk_ref, vt_ref, bias_ref, kbt_ref, *, tm):
    blocks = tm // MOBA_BLOCK
    first_blk = pl.program_id(1) * blocks
    h = _rms(x_ref[0], g_ref[...]).astype(BF16)
    lane = lax.broadcasted_iota(jnp.int32, (tm, LANES), 1)

    def tiles_dot(a, m_ref):
        return jnp.concatenate(
            [jnp.dot(a[:, c * BD:(c + 1) * BD], m_ref[...], preferred_element_type=F32)
             for c in range(SECTION // BD)], axis=1)

    def qk_section(sec, bd_ref, perm_ref, seg, cos_ref, sin_ref):
        y = jnp.dot(h, w_ref[:, sec * SECTION:(sec + 1) * SECTION], preferred_element_type=F32)
        ss = tiles_dot((y * y).astype(BF16), bd_ref)
        yn = y * lax.rsqrt(ss * (1.0 / seg) + EPS) * gain_ref[sec_row[sec]:sec_row[sec] + 1, :]
        rot = tiles_dot(yn.astype(BF16), perm_ref)
        cos = cos_ref[...]
        sin = sin_ref[...]
        return [yn[:, c * LANES:(c + 1) * LANES] * cos + rot[:, c * LANES:(c + 1) * LANES] * sin
                for c in range(SECTION // LANES)]

    def store(out_sec, chunks, scale):
        for c, yc in enumerate(chunks):
            lo = out_sec * SECTION + c * LANES
            qk_ref[0, :, lo:lo + LANES] = (yc * scale).astype(BF16)

    sec_row = {0: 0, 1: 1, 3: 2, 4: 3}
    store(0, qk_section(0, bdd_ref, pd_ref, DIFF_HALF, cosd_ref, sind_ref), DIFF_HALF ** -0.5 * LOG2E)
    store(1, qk_section(1, bdd_ref, pd_ref, DIFF_HALF, cosd_ref, sind_ref), 1.0)
    mq = qk_section(3, bdm_ref, pm_ref, HEAD_DIM, cosm_ref, sinm_ref)
    store(2, mq, HEAD_DIM ** -0.5 * LOG2E)
    mk = qk_section(4, bdm_ref, pm_ref, HEAD_DIM, cosm_ref, sinm_ref)
    store(3, mk, 1.0)
    for out_sec, sec in enumerate((2, 5)):
        lo = sec * SECTION
        vt = jnp.dot(h, w_ref[:, lo:lo + SECTION], preferred_element_type=F32).T.astype(BF16)
        for sb in range(blocks):
            vt_ref[0, sb, out_sec * SECTION:(out_sec + 1) * SECTION, :] = (
                vt[:, sb * MOBA_BLOCK:(sb + 1) * MOBA_BLOCK])

    @pl.when(first_blk == 0)
    def _():
        kbt_ref[...] = jnp.zeros_like(kbt_ref)

    mk_full = jnp.concatenate(mk, axis=1)
    head_of_lane = lax.broadcasted_iota(jnp.int32, (1, SECTION), 1) // HEAD_DIM
    for sb in range(blocks):
        kbar = jnp.mean(mk_full[sb * MOBA_BLOCK:(sb + 1) * MOBA_BLOCK], axis=0, keepdims=True)
        for hh in range(N_HEADS_MOBA):
            kbt_ref[pl.ds(hh * GATE_SEG + first_blk + sb, 1), :] = jnp.where(head_of_lane == hh, kbar, 0.0)

    def split(a):
        hi = a.astype(BF16)
        return hi, (a - hi.astype(F32)).astype(BF16)

    q_hi, q_lo = split(jnp.concatenate(mq, axis=1))
    k_hi, k_lo = split(kbt_ref[...])
    gate = lax.dot_general(jnp.concatenate([q_hi, q_lo, q_hi], axis=1),
                           jnp.concatenate([k_hi, k_hi, k_lo], axis=1), _NT,
                           preferred_element_type=F32)
    blk = lane & (GATE_SEG - 1)
    blk_f = blk.astype(F32)
    neg_inf = jnp.float32(-jnp.inf)
    own = first_blk + lax.broadcasted_iota(jnp.int32, (tm, LANES), 0) // MOBA_BLOCK
    avail = jnp.where(blk < own, gate, neg_inf)
    chosen = blk == own
    for _ in range(MOBA_TOPK):
        best = _seg_allreduce(avail, jnp.maximum, GATE_SEG, lane)
        cand = jnp.logical_and(avail == best, avail > neg_inf)
        first = _seg_allreduce(jnp.where(cand, blk_f, float(GATE_SEG)), jnp.minimum, GATE_SEG, lane)
        pick = blk_f == first
        chosen = jnp.logical_or(chosen, pick)
        avail = jnp.where(pick, neg_inf, avail)
    bias_ref[0] = jnp.where(chosen, 0.0, MASKED_BIAS).astype(BF16)


def _even_proj(x, norm_g, w_in, bdd, bdm, pd, pm, gains, cosd, sind, cosm, sinm):
    B, S, D = x.shape
    tm = EVEN_PROJ_ROWS
    blocks = tm // MOBA_BLOCK
    n_out = w_in.shape[1]
    n_qk = 4 * SECTION
    n_v = 2 * SECTION
    const = lambda b, t: (0, 0)
    tab = pl.BlockSpec((tm, LANES), lambda b, t: (t, 0))
    return pl.pallas_call(
        functools.partial(_even_proj_kernel, tm=tm),
        grid=(B, S // tm),
        in_specs=[
            pl.BlockSpec((1, tm, D), lambda b, t: (b, t, 0)),
            pl.BlockSpec((1, D), const),
            pl.BlockSpec((D, n_out), const, pipeline_mode=pl.Buffered(1)),
            pl.BlockSpec((BD, BD), const),
            pl.BlockSpec((BD, BD), const),
            pl.BlockSpec((BD, BD), const),
            pl.BlockSpec((BD, BD), const),
            pl.BlockSpec((4, SECTION), const),
            tab, tab, tab, tab,
        ],
        out_specs=[
            pl.BlockSpec((1, tm, n_qk), lambda b, t: (b, t, 0)),
            pl.BlockSpec((1, blocks, n_v, MOBA_BLOCK), lambda b, t: (b, t, 0, 0)),
            pl.BlockSpec((1, tm, LANES), lambda b, t: (b, t, 0)),
        ],
        out_shape=[
            jax.ShapeDtypeStruct((B, S, n_qk), BF16),
            jax.ShapeDtypeStruct((B, S // MOBA_BLOCK, n_v, MOBA_BLOCK), BF16),
            jax.ShapeDtypeStruct((B, S, LANES), BF16),
        ],
        scratch_shapes=[pltpu.VMEM((LANES, SECTION), F32)],
        compiler_params=pltpu.CompilerParams(
            dimension_semantics=("arbitrary", "arbitrary"),
            vmem_limit_bytes=VMEM_LIMIT_BYTES),
        name="even_proj",
    )(x, norm_g, w_in, bdd, bdm, pd, pm, gains, cosd, sind, cosm, sinm)


def _flash_head_pair(make_q, n, v_rows, k_ref, vt_ref, s_ref, p_ref, nq, t, emit, k_aug_ref=None):
    streams = k_ref.shape[0]
    per_stream = n // streams

    def scores(q_all, j):
        start = pl.multiple_of(j * t, t)
        out = []
        for g in range(streams):
            k = k_ref[g, pl.ds(start, t), :]
            if k_aug_ref is not None:
                k = jnp.concatenate([k, k_aug_ref[pl.ds(start, t), :]], axis=1)
            out.append(lax.dot_general(k, q_all[g], _NT, preferred_element_type=F32))
        return jnp.concatenate(out, axis=1)

    def softmax(s_all, stats, diagonal):
        ps, new_stats, alphas = [], [], []
        for i in range(n):
            m, l = stats[i]
            s = s_all[:, i * t:(i + 1) * t]
            if diagonal:
                key = lax.broadcasted_iota(jnp.int32, (t, t), 0)
                qry = lax.broadcasted_iota(jnp.int32, (t, t), 1)
                s = jnp.where(key <= qry, s, -jnp.inf)
            m_new = jnp.maximum(m, jnp.max(s, axis=0, keepdims=True))
            p = jnp.exp2(s - m_new)
            alpha = jnp.exp2(m - m_new)
            ps.append(p.astype(BF16))
            new_stats.append((m_new, alpha * l + jnp.sum(p, axis=0, keepdims=True)))
            alphas.append(alpha)
        return tuple(ps), tuple(new_stats), alphas

    def weighted_values(j, ps):
        out = []
        for g in range(streams):
            vt = vt_ref[g, j]
            for li, (lo, hi) in enumerate(v_rows):
                out.append(jnp.dot(vt[lo:hi], ps[g * per_stream + li], preferred_element_type=F32))
        return out

    def step(q_all, j, cur, nxt, carry):
        stats, accs = carry
        s_ref[nxt] = scores(q_all, j + 1)
        p_prev = tuple(p_ref[nxt, i] for i in range(n))
        pv = weighted_values(jnp.maximum(j - 1, 0), p_prev)
        pv = [jnp.where(j > 0, x, 0.0) for x in pv]
        p_cur, stats, alphas = softmax(s_ref[cur], stats, False)
        for i in range(n):
            p_ref[cur, i] = p_cur[i]
        accs = tuple(a * (acc + x) for a, acc, x in zip(alphas, accs, pv))
        return stats, accs

    def query_tile(qi, first, second, odd, next_first):
        q_all = make_q(qi)
        init = (tuple((jnp.full((1, t), M_INIT, F32), jnp.zeros((1, t), F32)) for _ in range(n)),
                tuple(jnp.zeros((hi - lo, t), F32) for _ in range(streams) for lo, hi in v_rows))
        carry = lax.fori_loop(
            0, qi // 2,
            lambda i, c: step(q_all, 2 * i + 1, second, first, step(q_all, 2 * i, first, second, c)), init)
        if odd:
            carry = step(q_all, qi - 1, first, second, carry)
        diag, prev = (second, first) if odd else (first, second)
        stats, accs = carry
        s_ref[next_first] = scores(make_q(jnp.minimum(qi + 1, nq - 1)), 0)
        pv_prev = weighted_values(jnp.maximum(qi - 1, 0), tuple(p_ref[prev, i] for i in range(n)))
        pv_prev = [jnp.where(qi > 0, x, 0.0) for x in pv_prev]
        p_cur, stats, alphas = softmax(s_ref[diag], stats, True)
        pv_cur = weighted_values(qi, p_cur)
        emit(qi, [(a * (acc + x) + y, l)
                  for a, acc, x, y, (_, l) in zip(alphas, accs, pv_prev, pv_cur, stats)])

    p_ref[...] = jnp.zeros(p_ref.shape, BF16)
    s_ref[0] = scores(make_q(0), 0)

    def pair_of_tiles(u, _):
        query_tile(2 * u, 0, 1, False, 2)
        query_tile(2 * u + 1, 2, 1, True, 0)
        return 0

    lax.fori_loop(0, nq // 2, pair_of_tiles, 0)


def _diff_attn_kernel(lam_ref, q_ref, k_ref, vt_ref, sg_ref, o_ref, s_ref, p_ref, *, t, lambda_init):
    nq = q_ref.shape[1] // t
    seg = lax.broadcasted_iota(jnp.int32, (t, LANES), 1) // DIFF_HALF
    lam_p = lam_ref[...]
    lam = (jnp.exp(jnp.sum(lam_p[0:1] * lam_p[1:2], axis=1, keepdims=True))
           - jnp.exp(jnp.sum(lam_p[2:3] * lam_p[3:4], axis=1, keepdims=True)) + lambda_init)

    streams = q_ref.shape[0]

    def make_q(qi):
        out = []
        for g in range(streams):
            q = q_ref[g, pl.ds(pl.multiple_of(qi * t, t), t), :]
            out.append(jnp.concatenate([jnp.where(seg == i, q, jnp.zeros_like(q)) for i in range(4)], axis=0))
        return out

    def emit(qi, res):
        a = [acc * (1.0 / l) for acc, l in res]
        for g in range(streams):
            heads = []
            for hh in range(2):
                o = a[4 * g + 2 * hh] - lam * a[4 * g + 2 * hh + 1]
                heads.append(o * lax.rsqrt(jnp.mean(o * o, axis=0, keepdims=True) + EPS))
            o = jnp.concatenate(heads, axis=0).T
            o_ref[g, pl.ds(pl.multiple_of(qi * t, t), t), :] = (
                o * sg_ref[...] * (1.0 - lambda_init)).astype(BF16)

    v_rows = [(HEAD_DIM * (i // 2), HEAD_DIM * (i // 2 + 1)) for i in range(4)]
    _flash_head_pair(make_q, 4 * streams, v_rows, k_ref, vt_ref, s_ref, p_ref, nq, t, emit)


def _moba_attn_kernel(q_ref, bias_ref, k_ref, vt_ref, ind_ref, o_ref, s_ref, p_ref, *, t):
    hp = pl.program_id(1)
    nq = q_ref.shape[1] // t
    lane = lax.broadcasted_iota(jnp.int32, (t, LANES), 1)

    streams = q_ref.shape[0]

    def make_q(qi):
        rows = pl.ds(pl.multiple_of(qi * t, t), t)
        out = []
        for g in range(streams):
            q = q_ref[g, rows, :]
            bias = bias_ref[g, rows, :]
            strips = []
            for hh in range(2):
                qz = jnp.where(lane // HEAD_DIM == hh, q, jnp.zeros_like(q))
                bz = jnp.where(lane // GATE_SEG == 2 * hp + hh, bias, jnp.zeros_like(bias))
                strips.append(jnp.concatenate([qz, bz], axis=1))
            out.append(jnp.concatenate(strips, axis=0))
        return out

    def emit(qi, res):
        for g in range(streams):
            o = jnp.concatenate([acc * (1.0 / l) for acc, l in res[2 * g:2 * g + 2]], axis=0)
            o_ref[g, pl.ds(pl.multiple_of(qi * t, t), t), :] = o.T.astype(BF16)

    v_rows = [(0, HEAD_DIM), (HEAD_DIM, 2 * HEAD_DIM)]
    _flash_head_pair(make_q, 2 * streams, v_rows, k_ref, vt_ref, s_ref, p_ref, nq, t, emit, k_aug_ref=ind_ref)


def _attention(qk, vt, bias, ind, lam_params, subln_tile, lambda_init):
    B, S, _ = qk.shape
    t = MOBA_BLOCK
    n_pairs = SECTION // LANES
    g = ATTN_STREAMS if B % ATTN_STREAMS == 0 else 1
    assert (S // t) % 2 == 0
    grid = (B // g, n_pairs)
    qk_spec = lambda sec: pl.BlockSpec((g, S, LANES), lambda b, p: (b, 0, sec * n_pairs + p))
    vt_spec = lambda sec: pl.BlockSpec((g, S // t, LANES, t), lambda b, p: (b, 0, sec * n_pairs + p, 0))
    out_spec = pl.BlockSpec((g, S, LANES), lambda b, p: (b, 0, p))
    params = pltpu.CompilerParams(
        dimension_semantics=("arbitrary", "arbitrary"),
        vmem_limit_bytes=VMEM_LIMIT_BYTES)
    out_shape = jax.ShapeDtypeStruct((B, S, SECTION), BF16)
    pipeline_scratch = lambda n: [pltpu.VMEM((3, t, n * t), F32), pltpu.VMEM((3, n, t, t), BF16)]

    diff = pl.pallas_call(
        functools.partial(_diff_attn_kernel, t=t, lambda_init=lambda_init),
        grid=grid,
        in_specs=[pl.BlockSpec((4, DIFF_HALF), lambda b, p: (0, 0)),
                  qk_spec(0), qk_spec(1), vt_spec(0),
                  pl.BlockSpec((1, LANES), lambda b, p: (0, 0))],
        out_specs=out_spec, out_shape=out_shape, compiler_params=params,
        scratch_shapes=pipeline_scratch(4 * g),
        name="diff_attn",
    )(lam_params, qk, qk, vt, subln_tile)

    moba = pl.pallas_call(
        functools.partial(_moba_attn_kernel, t=t),
        grid=grid,
        in_specs=[qk_spec(2),
                  pl.BlockSpec((g, S, LANES), lambda b, p: (b, 0, 0)),
                  qk_spec(3), vt_spec(1),
                  pl.BlockSpec((S, LANES), lambda b, p: (0, 0))],
        out_specs=out_spec, out_shape=out_shape, compiler_params=params,
        scratch_shapes=pipeline_scratch(2 * g),
        name="moba_attn",
    )(qk, bias, qk, vt, ind)
    return diff, moba


def _out_ffn_kernel(mixa_ref, mixb_ref, x_ref, wo_ref, g_ref, wi_ref, w2_ref, o_ref, *, d_ff, chunk):
    mix = jnp.concatenate([mixa_ref[0], mixb_ref[0]], axis=1)
    x1 = x_ref[0] + jnp.dot(mix, wo_ref[...], preferred_element_type=F32)
    hn = _rms(x1, g_ref[...]).astype(BF16)
    acc = x1
    for c in range(d_ff // chunk):
        lo = c * chunk
        g = jnp.dot(hn, wi_ref[0, :, lo:lo + chunk], preferred_element_type=F32)
        u = jnp.dot(hn, wi_ref[0, :, d_ff + lo:d_ff + lo + chunk], preferred_element_type=F32)
        a = (g * jax.nn.sigmoid(g) * u).astype(BF16)
        acc = acc + jnp.dot(a, w2_ref[0, lo:lo + chunk, :], preferred_element_type=F32)
    o_ref[0] = acc


def _out_ffn(mix_a, col_a, mix_b, col_b, x, w_out, norm_g, w_in, w2, layer):
    B, S, D = x.shape
    d_ff = w2.shape[1]
    tm = 512
    half = w_out.shape[0] // 2
    const = lambda b, t: (0, 0)
    resident = lambda shape: pl.BlockSpec(shape, const, pipeline_mode=pl.Buffered(1))
    slab = lambda a: pl.BlockSpec((1,) + a.shape[1:], lambda b, t: (layer, 0, 0), pipeline_mode=pl.Buffered(1))
    tile = pl.BlockSpec((1, tm, D), lambda b, t: (b, t, 0))
    mix_spec = lambda col: pl.BlockSpec((1, tm, half), lambda b, t: (b, t, col))
    return pl.pallas_call(
        functools.partial(_out_ffn_kernel, d_ff=d_ff, chunk=256),
        grid=(B, S // tm),
        in_specs=[mix_spec(col_a), mix_spec(col_b), tile, resident(w_out.shape), pl.BlockSpec((1, D), const),
                  slab(w_in), slab(w2)],
        out_specs=tile,
        out_shape=jax.ShapeDtypeStruct((B, S, D), F32),
        compiler_params=pltpu.CompilerParams(
            dimension_semantics=("arbitrary", "arbitrary"),
            vmem_limit_bytes=VMEM_LIMIT_BYTES),
        name="out_ffn",
    )(mix_a, mix_b, x, w_out, norm_g, w_in, w2)


def _odd_mix_kernel(x_ref, g_ref, w_ref, b_ref, lng_ref, lnb_ref, ws_ref, bs_ref,
                    cw_ref, cb_ref, clg_ref, clb_ref, o_ref, ext_ref, *, tm):
    ti = pl.program_id(1)
    h = _rms(x_ref[0], g_ref[...]).astype(BF16)

    def proj(lo, width):
        return jnp.dot(h, w_ref[:, lo:lo + width], preferred_element_type=F32) + b_ref[:, lo:lo + width]

    gu = jax.nn.gelu(proj(0, GMLP_WIDTH), approximate=True)
    gv = jax.nn.gelu(proj(GMLP_WIDTH, GMLP_WIDTH), approximate=True)
    gv = _layer_norm(gv, lng_ref[...], lnb_ref[...]).astype(BF16)
    T = GMLP_CHUNK
    tri = lax.broadcasted_iota(jnp.int32, (T, T), 1) <= lax.broadcasted_iota(jnp.int32, (T, T), 0)
    lane = lax.broadcasted_iota(jnp.int32, (T, LANES), 1)
    first = lane < (GMLP_WIDTH // GMLP_GROUPS)
    zero = jnp.zeros((T, LANES), BF16)
    for lb in range(GMLP_WIDTH // LANES):
        wpair = jnp.concatenate(
            [jnp.where(tri, ws_ref[2 * lb + i], 0.0).astype(BF16) for i in range(2)], axis=1)
        bs = bs_ref[:, lb * LANES:(lb + 1) * LANES]
        for c in range(tm // T):
            vc = gv[c * T:(c + 1) * T, lb * LANES:(lb + 1) * LANES]
            vpair = jnp.concatenate([jnp.where(first, vc, zero), jnp.where(first, zero, vc)], axis=0)
            sg = jnp.dot(wpair, vpair, preferred_element_type=F32) + bs
            o_ref[0, c * T:(c + 1) * T, lb * LANES:(lb + 1) * LANES] = (
                gu[c * T:(c + 1) * T, lb * LANES:(lb + 1) * LANES] * sg).astype(BF16)

    lo = 2 * GMLP_WIDTH
    cval = proj(lo, CONV_CH) * jax.nn.sigmoid(proj(lo + CONV_CH, CONV_CH))

    @pl.when(ti == 0)
    def _():
        ext_ref[0:CONV_HALO, :] = jnp.zeros((CONV_HALO, CONV_CH), F32)

    ext_ref[CONV_HALO:CONV_HALO + tm, :] = cval
    first_tap = CONV_HALO - (CONV_KERNEL - 1)
    span = tm + 8 * ((CONV_KERNEL - 1) // 8)
    conv = jnp.zeros((tm, CONV_CH), F32) + cb_ref[...]
    for r in range(8):
        taps = [k for k in range(CONV_KERNEL) if (first_tap + k) % 8 == r]
        base = first_tap + taps[0]
        rows = ext_ref[base:base + span, :]
        for k in taps:
            off = first_tap + k - base
            conv = conv + rows[off:off + tm, :] * cw_ref[k:k + 1, :]
    ext_ref[0:CONV_HALO, :] = ext_ref[tm:tm + CONV_HALO, :]
    dn = _layer_norm(conv, clg_ref[...], clb_ref[...])
    o_ref[0, :, GMLP_WIDTH:GMLP_WIDTH + CONV_CH] = (dn * jax.nn.sigmoid(dn)).astype(BF16)


def _odd_mix(x, norm_g, w_in, b_in, ln_g, ln_b, w_s, bs_tile, conv_w, conv_b, cln_g, cln_b):
    B, S, D = x.shape
    tm = 512
    const2 = lambda b, t: (0, 0)
    full = lambda a: pl.BlockSpec(a.shape, (lambda b, t: (0,) * a.ndim))
    n_out = GMLP_WIDTH + CONV_CH
    span = tm + 8 * ((CONV_KERNEL - 1) // 8)
    args = (x, norm_g, w_in, b_in, ln_g, ln_b, w_s, bs_tile, conv_w, conv_b, cln_g, cln_b)
    return pl.pallas_call(
        functools.partial(_odd_mix_kernel, tm=tm),
        grid=(B, S // tm),
        in_specs=[pl.BlockSpec((1, tm, D), lambda b, t: (b, t, 0))] + [full(a) for a in args[1:]],
        out_specs=pl.BlockSpec((1, tm, n_out), lambda b, t: (b, t, 0)),
        out_shape=jax.ShapeDtypeStruct((B, S, n_out), BF16),
        scratch_shapes=[pltpu.VMEM((span + 8, CONV_CH), F32)],
        compiler_params=pltpu.CompilerParams(
            dimension_semantics=("arbitrary", "arbitrary"),
            vmem_limit_bytes=VMEM_LIMIT_BYTES),
        name="odd_mix",
    )(*args)


def _rope_tables(seq, dim, seg_per_block):
    inv = 1.0 / (ROPE_THETA ** (jnp.arange(0, dim, 2, dtype=F32) / dim))
    ang = jnp.arange(seq, dtype=F32)[:, None] * inv[None, :]
    cos, sin = jnp.cos(ang), jnp.sin(ang)
    cos = jnp.tile(jnp.concatenate([cos, cos], axis=1), (1, seg_per_block))
    sin = jnp.tile(jnp.concatenate([-sin, sin], axis=1), (1, seg_per_block))
    return cos, sin


def _block_diag_ones(seg):
    idx = jnp.arange(BD) // seg
    return (idx[:, None] == idx[None, :]).astype(BF16)


def _rotate_half_matrix(seg):
    dst = jnp.arange(BD)
    src = jnp.where(dst % seg < seg // 2, dst + seg // 2, dst - seg // 2)
    return (jnp.arange(BD)[:, None] == src[None, :]).astype(BF16)


def kernel(x, attn_norm_g, ffn_norm_g, ffn_w_in, ffn_w_out, even_w_in, even_w_out,
           diff_q_norm_g, diff_k_norm_g, diff_lambda_q1, diff_lambda_k1,
           diff_lambda_q2, diff_lambda_k2, diff_subln_g, moba_q_norm_g, moba_k_norm_g,
           odd_w_in, odd_b_in, odd_w_out, gmlp_ln_g, gmlp_ln_b, gmlp_w_s, gmlp_b_s,
           conv_w, conv_b, conv_ln_g, conv_ln_b):
    B, S, D = x.shape
    assert S % EVEN_PROJ_ROWS == 0 and S // MOBA_BLOCK <= GATE_SEG
    row = lambda v: v.reshape(1, -1).astype(F32)
    ffn_w_in_bf = ffn_w_in.astype(BF16)
    ffn_w_out_bf = ffn_w_out.astype(BF16)

    lambda_init = 0.8 - 0.6 * math.exp(-0.3 * 0)
    cosd, sind = _rope_tables(S, DIFF_HALF, LANES // DIFF_HALF)
    cosm, sinm = _rope_tables(S, HEAD_DIM, LANES // HEAD_DIM)
    gains = jnp.stack([
        jnp.tile(diff_q_norm_g[0], SECTION // DIFF_HALF),
        jnp.tile(diff_k_norm_g[0], SECTION // DIFF_HALF),
        jnp.tile(moba_q_norm_g[0], SECTION // HEAD_DIM),
        jnp.tile(moba_k_norm_g[0], SECTION // HEAD_DIM)]).astype(F32)
    qk, vt, bias = _even_proj(x, row(attn_norm_g[0]), even_w_in[0].astype(BF16),
                              _block_diag_ones(DIFF_HALF), _block_diag_ones(HEAD_DIM),
                              _rotate_half_matrix(DIFF_HALF), _rotate_half_matrix(HEAD_DIM), gains,
                              cosd, sind, cosm, sinm)
    key_block = jnp.arange(S)[:, None] // MOBA_BLOCK
    ind = (key_block == (jnp.arange(LANES)[None, :] % GATE_SEG)).astype(BF16)
    lam_params = jnp.stack([diff_lambda_q1[0], diff_lambda_k1[0],
                            diff_lambda_q2[0], diff_lambda_k2[0]]).astype(F32)
    subln_tile = jnp.tile(diff_subln_g[0], LANES // HEAD_DIM).reshape(1, LANES).astype(F32)
    diff, moba = _attention(qk, vt, bias, ind, lam_params, subln_tile, lambda_init)
    x = _out_ffn(diff, 0, moba, 0, x, even_w_out[0].astype(BF16), row(ffn_norm_g[0]),
                 ffn_w_in_bf, ffn_w_out_bf, 0)

    bs_tile = jnp.repeat(gmlp_b_s[0].T, GMLP_WIDTH // GMLP_GROUPS, axis=1).astype(F32)
    mix = _odd_mix(x, row(attn_norm_g[1]), odd_w_in[0].astype(BF16), row(odd_b_in[0]),
                   row(gmlp_ln_g[0]), row(gmlp_ln_b[0]), gmlp_w_s[0].astype(F32), bs_tile,
                   conv_w[0].astype(F32), row(conv_b[0]), row(conv_ln_g[0]), row(conv_ln_b[0]))
    x = _out_ffn(mix, 0, mix, 1, x, odd_w_out[0].astype(BF16), row(ffn_norm_g[1]),
                 ffn_w_in_bf, ffn_w_out_bf, 1)
    return x
```

```python
import functools
import math

import jax
import jax.numpy as jnp
from jax import lax
from jax.experimental import pallas as pl
from jax.experimental.pallas import tpu as pltpu

F32 = jnp.float32
BF16 = jnp.bfloat16

LANES = 128
VMEM_LIMIT_BYTES = 56 * 1024 * 1024

HEAD_DIM = 64
DIFF_HALF = 32
N_HEADS_DIFF = 8
N_HEADS_MOBA = 8
MOBA_BLOCK = 256
MOBA_TOPK = 3
GMLP_CHUNK = 128
GMLP_GROUPS = 8
GMLP_WIDTH = 512
CONV_CH = 512
CONV_KERNEL = 31
ROPE_THETA = 10000.0
EPS = 1e-6

SECTION = 512
BD = 256
EVEN_PROJ_ROWS = 1024
ATTN_STREAMS = 2
SCORE_BOUND = 40.0
ONES_ROWS = 16
GATE_SEG = LANES // N_HEADS_MOBA
CONV_HALO = 32
MASKED_BIAS = -1e30
M_INIT = -1e29
LOG2E = math.log2(math.e)

_NT = (((1,), (1,)), ((), ()))


def _rms(x, g):
    return x * lax.rsqrt(jnp.mean(x * x, axis=-1, keepdims=True) + EPS) * g


def _layer_norm(x, g, b):
    mu = jnp.mean(x, axis=-1, keepdims=True)
    xc = x - mu
    return xc * lax.rsqrt(jnp.mean(xc * xc, axis=-1, keepdims=True) + EPS) * g + b


def _seg_allreduce(x, op, seg, lane):
    s = 1
    while s < seg:
        up = pltpu.roll(x, LANES - s, 1)
        dn = pltpu.roll(x, s, 1)
        x = op(x, jnp.where((lane & s) == 0, up, dn))
        s *= 2
    return x


def _even_proj_kernel(x_ref, g_ref, w_ref, bdd_ref, bdm_ref, pd_ref, pm_ref, gain_ref,
                      cosd_ref, sind_ref, cosm_ref, sinm_ref,
                      qk_ref, vt_ref, bias_ref, kbt_ref, *, tm):
    blocks = tm // MOBA_BLOCK
    first_blk = pl.program_id(1) * blocks
    h = _rms(x_ref[0], g_ref[...]).astype(BF16)
    lane = lax.broadcasted_iota(jnp.int32, (tm, LANES), 1)

    def tiles_dot(a, m_ref):
        return jnp.concatenate(
            [jnp.dot(a[:, c * BD:(c + 1) * BD], m_ref[...], preferred_element_type=F32)
             for c in range(SECTION // BD)], axis=1)

    def qk_section(sec, bd_ref, perm_ref, seg, cos_ref, sin_ref):
        y = jnp.dot(h, w_ref[:, sec * SECTION:(sec + 1) * SECTION], preferred_element_type=F32)
        ss = tiles_dot((y * y).astype(BF16), bd_ref)
        yn = y * lax.rsqrt(ss * (1.0 / seg) + EPS) * gain_ref[sec_row[sec]:sec_row[sec] + 1, :]
        rot = tiles_dot(yn.astype(BF16), perm_ref)
        cos = cos_ref[...]
        sin = sin_ref[...]
        return [yn[:, c * LANES:(c + 1) * LANES] * cos + rot[:, c * LANES:(c + 1) * LANES] * sin
                for c in range(SECTION // LANES)]

    def store(out_sec, chunks, scale):
        for c, yc in enumerate(chunks):
            lo = out_sec * SECTION + c * LANES
            qk_ref[0, :, lo:lo + LANES] = (yc * scale).astype(BF16)

    sec_row = {0: 0, 1: 1, 3: 2, 4: 3}
    store(0, qk_section(0, bdd_ref, pd_ref, DIFF_HALF, cosd_ref, sind_ref), DIFF_HALF ** -0.5 * LOG2E)
    store(1, qk_section(1, bdd_ref, pd_ref, DIFF_HALF, cosd_ref, sind_ref), 1.0)
    mq = qk_section(3, bdm_ref, pm_ref, HEAD_DIM, cosm_ref, sinm_ref)
    store(2, mq, HEAD_DIM ** -0.5 * LOG2E)
    mk = qk_section(4, bdm_ref, pm_ref, HEAD_DIM, cosm_ref, sinm_ref)
    store(3, mk, 1.0)
    for out_sec, sec in enumerate((2, 5)):
        lo = sec * SECTION
        vt = jnp.dot(h, w_ref[:, lo:lo + SECTION], preferred_element_type=F32).T.astype(BF16)
        for sb in range(blocks):
            vt_ref[0, sb, out_sec * SECTION:(out_sec + 1) * SECTION, :] = (
                vt[:, sb * MOBA_BLOCK:(sb + 1) * MOBA_BLOCK])

    @pl.when(first_blk == 0)
    def _():
        kbt_ref[...] = jnp.zeros_like(kbt_ref)

    mk_full = jnp.concatenate(mk, axis=1)
    head_of_lane = lax.broadcasted_iota(jnp.int32, (1, SECTION), 1) // HEAD_DIM
    for sb in range(blocks):
        kbar = jnp.mean(mk_full[sb * MOBA_BLOCK:(sb + 1) * MOBA_BLOCK], axis=0, keepdims=True)
        for hh in range(N_HEADS_MOBA):
            kbt_ref[pl.ds(hh * GATE_SEG + first_blk + sb, 1), :] = jnp.where(head_of_lane == hh, kbar, 0.0)

    def split(a):
        hi = a.astype(BF16)
        return hi, (a - hi.astype(F32)).astype(BF16)

    q_hi, q_lo = split(jnp.concatenate(mq, axis=1))
    k_hi, k_lo = split(kbt_ref[...])
    gate = lax.dot_general(jnp.concatenate([q_hi, q_lo, q_hi], axis=1),
                           jnp.concatenate([k_hi, k_hi, k_lo], axis=1), _NT,
                           preferred_element_type=F32)
    blk = lane & (GATE_SEG - 1)
    blk_f = blk.astype(F32)
    neg_inf = jnp.float32(-jnp.inf)
    own = first_blk + lax.broadcasted_iota(jnp.int32, (tm, LANES), 0) // MOBA_BLOCK
    avail = jnp.where(blk < own, gate, neg_inf)
    chosen = blk == own
    for _ in range(MOBA_TOPK):
        best = _seg_allreduce(avail, jnp.maximum, GATE_SEG, lane)
        cand = jnp.logical_and(avail == best, avail > neg_inf)
        first = _seg_allreduce(jnp.where(cand, blk_f, float(GATE_SEG)), jnp.minimum, GATE_SEG, lane)
        pick = blk_f == first
        chosen = jnp.logical_or(chosen, pick)
        avail = jnp.where(pick, neg_inf, avail)
    bias_ref[0] = jnp.where(chosen, 0.0, MASKED_BIAS).astype(BF16)


def _even_proj(x, norm_g, w_in, bdd, bdm, pd, pm, gains, cosd, sind, cosm, sinm):
    B, S, D = x.shape
    tm = EVEN_PROJ_ROWS
    blocks = tm // MOBA_BLOCK
    n_out = w_in.shape[1]
    n_qk = 4 * SECTION
    n_v = 2 * SECTION
    const = lambda b, t: (0, 0)
    tab = pl.BlockSpec((tm, LANES), lambda b, t: (t, 0))
    return pl.pallas_call(
        functools.partial(_even_proj_kernel, tm=tm),
        grid=(B, S // tm),
        in_specs=[
            pl.BlockSpec((1, tm, D), lambda b, t: (b, t, 0)),
            pl.BlockSpec((1, D), const),
            pl.BlockSpec((D, n_out), const, pipeline_mode=pl.Buffered(1)),
            pl.BlockSpec((BD, BD), const),
            pl.BlockSpec((BD, BD), const),
            pl.BlockSpec((BD, BD), const),
            pl.BlockSpec((BD, BD), const),
            pl.BlockSpec((4, SECTION), const),
            tab, tab, tab, tab,
        ],
        out_specs=[
            pl.BlockSpec((1, tm, n_qk), lambda b, t: (b, t, 0)),
            pl.BlockSpec((1, blocks, n_v, MOBA_BLOCK), lambda b, t: (b, t, 0, 0)),
            pl.BlockSpec((1, tm, LANES), lambda b, t: (b, t, 0)),
        ],
        out_shape=[
            jax.ShapeDtypeStruct((B, S, n_qk), BF16),
            jax.ShapeDtypeStruct((B, S // MOBA_BLOCK, n_v, MOBA_BLOCK), BF16),
            jax.ShapeDtypeStruct((B, S, LANES), BF16),
        ],
        scratch_shapes=[pltpu.VMEM((LANES, SECTION), F32)],
        compiler_params=pltpu.CompilerParams(
            dimension_semantics=("arbitrary", "arbitrary"),
            vmem_limit_bytes=VMEM_LIMIT_BYTES),
        name="even_proj",
    )(x, norm_g, w_in, bdd, bdm, pd, pm, gains, cosd, sind, cosm, sinm)


def _flash_head_pair(make_q, n, v_rows, k_ref, vt_ref, s_ref, p_ref, nq, t, emit, k_aug_ref=None):
    streams = k_ref.shape[0]
    per_stream = n // streams

    def scores(q_all, j):
        start = pl.multiple_of(j * t, t)
        out = []
        for g in range(streams):
            k = k_ref[g, pl.ds(start, t), :]
            if k_aug_ref is not None:
                k = jnp.concatenate([k, k_aug_ref[pl.ds(start, t), :]], axis=1)
            out.append(lax.dot_general(k, q_all[g], _NT, preferred_element_type=F32))
        return jnp.concatenate(out, axis=1)

    def softmax(s_all, stats, diagonal):
        ps, new_stats, alphas = [], [], []
        for i in range(n):
            m, l = stats[i]
            s = s_all[:, i * t:(i + 1) * t]
            if diagonal:
                key = lax.broadcasted_iota(jnp.int32, (t, t), 0)
                qry = lax.broadcasted_iota(jnp.int32, (t, t), 1)
                s = jnp.where(key <= qry, s, -jnp.inf)
            m_new = jnp.maximum(m, jnp.max(s, axis=0, keepdims=True))
            p = jnp.exp2(s - m_new)
            alpha = jnp.exp2(m - m_new)
            ps.append(p.astype(BF16))
            new_stats.append((m_new, alpha * l + jnp.sum(p, axis=0, keepdims=True)))
            alphas.append(alpha)
        return tuple(ps), tuple(new_stats), alphas

    def weighted_values(j, ps):
        out = []
        for g in range(streams):
            vt = vt_ref[g, j]
            for li, (lo, hi) in enumerate(v_rows):
                out.append(jnp.dot(vt[lo:hi], ps[g * per_stream + li], preferred_element_type=F32))
        return out

    def step(q_all, j, cur, nxt, carry):
        stats, accs = carry
        s_ref[nxt] = scores(q_all, j + 1)
        p_prev = tuple(p_ref[nxt, i] for i in range(n))
        pv = weighted_values(jnp.maximum(j - 1, 0), p_prev)
        pv = [jnp.where(j > 0, x, 0.0) for x in pv]
        p_cur, stats, alphas = softmax(s_ref[cur], stats, False)
        for i in range(n):
            p_ref[cur, i] = p_cur[i]
        accs = tuple(a * (acc + x) for a, acc, x in zip(alphas, accs, pv))
        return stats, accs

    def query_tile(qi, first, second, odd, next_first):
        q_all = make_q(qi)
        init = (tuple((jnp.full((1, t), M_INIT, F32), jnp.zeros((1, t), F32)) for _ in range(n)),
                tuple(jnp.zeros((hi - lo, t), F32) for _ in range(streams) for lo, hi in v_rows))
        carry = lax.fori_loop(
            0, qi // 2,
            lambda i, c: step(q_all, 2 * i + 1, second, first, step(q_all, 2 * i, first, second, c)), init)
        if odd:
            carry = step(q_all, qi - 1, first, second, carry)
        diag, prev = (second, first) if odd else (first, second)
        stats, accs = carry
        s_ref[next_first] = scores(make_q(jnp.minimum(qi + 1, nq - 1)), 0)
        pv_prev = weighted_values(jnp.maximum(qi - 1, 0), tuple(p_ref[prev, i] for i in range(n)))
        pv_prev = [jnp.where(qi > 0, x, 0.0) for x in pv_prev]
        p_cur, stats, alphas = softmax(s_ref[diag], stats, True)
        pv_cur = weighted_values(qi, p_cur)
        emit(qi, [(a * (acc + x) + y, l)
                  for a, acc, x, y, (_, l) in zip(alphas, accs, pv_prev, pv_cur, stats)])

    p_ref[...] = jnp.zeros(p_ref.shape, BF16)
    s_ref[0] = scores(make_q(0), 0)

    def pair_of_tiles(u, _):
        query_tile(2 * u, 0, 1, False, 2)
        query_tile(2 * u + 1, 2, 1, True, 0)
        return 0

    lax.fori_loop(0, nq // 2, pair_of_tiles, 0)


def _flash_bounded(make_q, n, v_rows, k_ref, vt_ref, s_ref, nq, t, emit, k_aug_ref=None):
    streams = k_ref.shape[0]
    per_stream = n // streams
    ones = jnp.ones((ONES_ROWS, t), BF16)
    width = per_stream * t
    key = lax.broadcasted_iota(jnp.int32, (t, width), 0)
    qry = lax.broadcasted_iota(jnp.int32, (t, width), 1) & (t - 1)

    def scores(q_all, j, slot):
        start = pl.multiple_of(j * t, t)
        for g in range(streams):
            k = k_ref[g, pl.ds(start, t), :]
            if k_aug_ref is not None:
                k = jnp.concatenate([k, k_aug_ref[pl.ds(start, t), :]], axis=1)
            s_ref[slot, g] = lax.dot_general(k, q_all[g], _NT, preferred_element_type=F32)

    def accumulate(j, slot, accs, diagonal):
        out = []
        for g in range(streams):
            p = jnp.exp2(s_ref[slot, g])
            if diagonal:
                p = jnp.where(key <= qry, p, 0.0)
            p = p.astype(BF16)
            vt = vt_ref[g, j]
            for li, (lo, hi) in enumerate(v_rows):
                lhs = jnp.concatenate([vt[lo:hi], ones], axis=0)
                out.append(accs[g * per_stream + li]
                           + jnp.dot(lhs, p[:, li * t:(li + 1) * t], preferred_element_type=F32))
        return tuple(out)

    def step(q_all, j, cur, nxt, accs):
        scores(q_all, j + 1, nxt)
        return accumulate(j, cur, accs, False)

    def query_tile(qi, first, second, odd, next_first):
        q_all = make_q(qi)
        init = tuple(jnp.zeros((hi - lo + ONES_ROWS, t), F32) for _ in range(streams) for lo, hi in v_rows)
        accs = lax.fori_loop(
            0, qi // 2,
            lambda i, a: step(q_all, 2 * i + 1, second, first, step(q_all, 2 * i, first, second, a)), init)
        if odd:
            accs = step(q_all, qi - 1, first, second, accs)
        scores(make_q(jnp.minimum(qi + 1, nq - 1)), 0, next_first)
        accs = accumulate(qi, second if odd else first, accs, True)
        rows = [hi - lo for _ in range(streams) for lo, hi in v_rows]
        emit(qi, [(a[:r], a[r:r + 1]) for a, r in zip(accs, rows)])

    scores(make_q(0), 0, 0)

    def pair_of_tiles(u, _):
        query_tile(2 * u, 0, 1, False, 2)
        query_tile(2 * u + 1, 2, 1, True, 0)
        return 0

    lax.fori_loop(0, nq // 2, pair_of_tiles, 0)


def _diff_attn_kernel(lam_ref, q_ref, k_ref, vt_ref, sg_ref, o_ref, *scratch, t, lambda_init, bounded):
    nq = q_ref.shape[1] // t
    seg = lax.broadcasted_iota(jnp.int32, (t, LANES), 1) // DIFF_HALF
    lam_p = lam_ref[...]
    lam = (jnp.exp(jnp.sum(lam_p[0:1] * lam_p[1:2], axis=1, keepdims=True))
           - jnp.exp(jnp.sum(lam_p[2:3] * lam_p[3:4], axis=1, keepdims=True)) + lambda_init)

    streams = q_ref.shape[0]

    def make_q(qi):
        out = []
        for g in range(streams):
            q = q_ref[g, pl.ds(pl.multiple_of(qi * t, t), t), :]
            out.append(jnp.concatenate([jnp.where(seg == i, q, jnp.zeros_like(q)) for i in range(4)], axis=0))
        return out

    def emit(qi, res):
        a = [acc * (1.0 / l) for acc, l in res]
        for g in range(streams):
            heads = []
            for hh in range(2):
                o = a[4 * g + 2 * hh] - lam * a[4 * g + 2 * hh + 1]
                heads.append(o * lax.rsqrt(jnp.mean(o * o, axis=0, keepdims=True) + EPS))
            o = jnp.concatenate(heads, axis=0).T
            o_ref[g, pl.ds(pl.multiple_of(qi * t, t), t), :] = (
                o * sg_ref[...] * (1.0 - lambda_init)).astype(BF16)

    v_rows = [(HEAD_DIM * (i // 2), HEAD_DIM * (i // 2 + 1)) for i in range(4)]
    if bounded:
        _flash_bounded(make_q, 4 * streams, v_rows, k_ref, vt_ref, *scratch, nq, t, emit)
    else:
        _flash_head_pair(make_q, 4 * streams, v_rows, k_ref, vt_ref, *scratch, nq, t, emit)


def _moba_attn_kernel(q_ref, bias_ref, k_ref, vt_ref, ind_ref, o_ref, *scratch, t, bounded):
    hp = pl.program_id(1)
    nq = q_ref.shape[1] // t
    lane = lax.broadcasted_iota(jnp.int32, (t, LANES), 1)

    streams = q_ref.shape[0]

    def make_q(qi):
        rows = pl.ds(pl.multiple_of(qi * t, t), t)
        out = []
        for g in range(streams):
            q = q_ref[g, rows, :]
            bias = bias_ref[g, rows, :]
            strips = []
            for hh in range(2):
                qz = jnp.where(lane // HEAD_DIM == hh, q, jnp.zeros_like(q))
                bz = jnp.where(lane // GATE_SEG == 2 * hp + hh, bias, jnp.zeros_like(bias))
                strips.append(jnp.concatenate([qz, bz], axis=1))
            out.append(jnp.concatenate(strips, axis=0))
        return out

    def emit(qi, res):
        for g in range(streams):
            o = jnp.concatenate([acc * (1.0 / l) for acc, l in res[2 * g:2 * g + 2]], axis=0)
            o_ref[g, pl.ds(pl.multiple_of(qi * t, t), t), :] = o.T.astype(BF16)

    v_rows = [(0, HEAD_DIM), (HEAD_DIM, 2 * HEAD_DIM)]
    if bounded:
        _flash_bounded(make_q, 2 * streams, v_rows, k_ref, vt_ref, *scratch, nq, t, emit, k_aug_ref=ind_ref)
    else:
        _flash_head_pair(make_q, 2 * streams, v_rows, k_ref, vt_ref, *scratch, nq, t, emit, k_aug_ref=ind_ref)


def _attention(qk, vt, bias, ind, lam_params, subln_tile, lambda_init, bounded):
    B, S, _ = qk.shape
    t = MOBA_BLOCK
    n_pairs = SECTION // LANES
    g = ATTN_STREAMS if B % ATTN_STREAMS == 0 else 1
    assert (S // t) % 2 == 0
    grid = (B // g, n_pairs)
    qk_spec = lambda sec: pl.BlockSpec((g, S, LANES), lambda b, p: (b, 0, sec * n_pairs + p))
    vt_spec = lambda sec: pl.BlockSpec((g, S // t, LANES, t), lambda b, p: (b, 0, sec * n_pairs + p, 0))
    out_spec = pl.BlockSpec((g, S, LANES), lambda b, p: (b, 0, p))
    params = pltpu.CompilerParams(
        dimension_semantics=("arbitrary", "arbitrary"),
        vmem_limit_bytes=VMEM_LIMIT_BYTES)
    out_shape = jax.ShapeDtypeStruct((B, S, SECTION), BF16)
    pipeline_scratch = lambda n: [pltpu.VMEM((3, g, t, n // g * t), F32)] if bounded else [
        pltpu.VMEM((3, t, n * t), F32), pltpu.VMEM((3, n, t, t), BF16)]

    diff = pl.pallas_call(
        functools.partial(_diff_attn_kernel, t=t, lambda_init=lambda_init, bounded=bounded),
        grid=grid,
        in_specs=[pl.BlockSpec((4, DIFF_HALF), lambda b, p: (0, 0)),
                  qk_spec(0), qk_spec(1), vt_spec(0),
                  pl.BlockSpec((1, LANES), lambda b, p: (0, 0))],
        out_specs=out_spec, out_shape=out_shape, compiler_params=params,
        scratch_shapes=pipeline_scratch(4 * g),
        name="diff_attn",
    )(lam_params, qk, qk, vt, subln_tile)

    moba = pl.pallas_call(
        functools.partial(_moba_attn_kernel, t=t, bounded=bounded),
        grid=grid,
        in_specs=[qk_spec(2),
                  pl.BlockSpec((g, S, LANES), lambda b, p: (b, 0, 0)),
                  qk_spec(3), vt_spec(1),
                  pl.BlockSpec((S, LANES), lambda b, p: (0, 0))],
        out_specs=out_spec, out_shape=out_shape, compiler_params=params,
        scratch_shapes=pipeline_scratch(2 * g),
        name="moba_attn",
    )(qk, bias, qk, vt, ind)
    return diff, moba


def _out_ffn_kernel(mixa_ref, mixb_ref, x_ref, wo_ref, g_ref, wi_ref, w2_ref, o_ref, *, d_ff, chunk):
    mix = jnp.concatenate([mixa_ref[0], mixb_ref[0]], axis=1)
    x1 = x_ref[0] + jnp.dot(mix, wo_ref[...], preferred_element_type=F32)
    hn = _rms(x1, g_ref[...]).astype(BF16)
    acc = x1
    for c in range(d_ff // chunk):
        lo = c * chunk
        g = jnp.dot(hn, wi_ref[0, :, lo:lo + chunk], preferred_element_type=F32)
        u = jnp.dot(hn, wi_ref[0, :, d_ff + lo:d_ff + lo + chunk], preferred_element_type=F32)
        a = (g * jax.nn.sigmoid(g) * u).astype(BF16)
        acc = acc + jnp.dot(a, w2_ref[0, lo:lo + chunk, :], preferred_element_type=F32)
    o_ref[0] = acc


def _out_ffn(mix_a, col_a, mix_b, col_b, x, w_out, norm_g, w_in, w2, layer):
    B, S, D = x.shape
    d_ff = w2.shape[1]
    tm = 512
    half = w_out.shape[0] // 2
    const = lambda b, t: (0, 0)
    resident = lambda shape: pl.BlockSpec(shape, const, pipeline_mode=pl.Buffered(1))
    slab = lambda a: pl.BlockSpec((1,) + a.shape[1:], lambda b, t: (layer, 0, 0), pipeline_mode=pl.Buffered(1))
    tile = pl.BlockSpec((1, tm, D), lambda b, t: (b, t, 0))
    mix_spec = lambda col: pl.BlockSpec((1, tm, half), lambda b, t: (b, t, col))
    return pl.pallas_call(
        functools.partial(_out_ffn_kernel, d_ff=d_ff, chunk=256),
        grid=(B, S // tm),
        in_specs=[mix_spec(col_a), mix_spec(col_b), tile, resident(w_out.shape), pl.BlockSpec((1, D), const),
                  slab(w_in), slab(w2)],
        out_specs=tile,
        out_shape=jax.ShapeDtypeStruct((B, S, D), F32),
        compiler_params=pltpu.CompilerParams(
            dimension_semantics=("arbitrary", "arbitrary"),
            vmem_limit_bytes=VMEM_LIMIT_BYTES),
        name="out_ffn",
    )(mix_a, mix_b, x, w_out, norm_g, w_in, w2)


def _odd_mix_kernel(x_ref, g_ref, w_ref, b_ref, lng_ref, lnb_ref, ws_ref, bs_ref,
                    cw_ref, cb_ref, clg_ref, clb_ref, o_ref, ext_ref, *, tm):
    ti = pl.program_id(1)
    h = _rms(x_ref[0], g_ref[...]).astype(BF16)

    def proj(lo, width):
        return jnp.dot(h, w_ref[:, lo:lo + width], preferred_element_type=F32) + b_ref[:, lo:lo + width]

    gu = jax.nn.gelu(proj(0, GMLP_WIDTH), approximate=True)
    gv = jax.nn.gelu(proj(GMLP_WIDTH, GMLP_WIDTH), approximate=True)
    gv = _layer_norm(gv, lng_ref[...], lnb_ref[...]).astype(BF16)
    T = GMLP_CHUNK
    tri = lax.broadcasted_iota(jnp.int32, (T, T), 1) <= lax.broadcasted_iota(jnp.int32, (T, T), 0)
    lane = lax.broadcasted_iota(jnp.int32, (T, LANES), 1)
    first = lane < (GMLP_WIDTH // GMLP_GROUPS)
    zero = jnp.zeros((T, LANES), BF16)
    for lb in range(GMLP_WIDTH // LANES):
        wpair = jnp.concatenate(
            [jnp.where(tri, ws_ref[2 * lb + i], 0.0).astype(BF16) for i in range(2)], axis=1)
        bs = bs_ref[:, lb * LANES:(lb + 1) * LANES]
        for c in range(tm // T):
            vc = gv[c * T:(c + 1) * T, lb * LANES:(lb + 1) * LANES]
            vpair = jnp.concatenate([jnp.where(first, vc, zero), jnp.where(first, zero, vc)], axis=0)
            sg = jnp.dot(wpair, vpair, preferred_element_type=F32) + bs
            o_ref[0, c * T:(c + 1) * T, lb * LANES:(lb + 1) * LANES] = (
                gu[c * T:(c + 1) * T, lb * LANES:(lb + 1) * LANES] * sg).astype(BF16)

    lo = 2 * GMLP_WIDTH
    cval = proj(lo, CONV_CH) * jax.nn.sigmoid(proj(lo + CONV_CH, CONV_CH))

    @pl.when(ti == 0)
    def _():
        ext_ref[0:CONV_HALO, :] = jnp.zeros((CONV_HALO, CONV_CH), F32)

    ext_ref[CONV_HALO:CONV_HALO + tm, :] = cval
    first_tap = CONV_HALO - (CONV_KERNEL - 1)
    span = tm + 8 * ((CONV_KERNEL - 1) // 8)
    conv = jnp.zeros((tm, CONV_CH), F32) + cb_ref[...]
    for r in range(8):
        taps = [k for k in range(CONV_KERNEL) if (first_tap + k) % 8 == r]
        base = first_tap + taps[0]
        rows = ext_ref[base:base + span, :]
        for k in taps:
            off = first_tap + k - base
            conv = conv + rows[off:off + tm, :] * cw_ref[k:k + 1, :]
    ext_ref[0:CONV_HALO, :] = ext_ref[tm:tm + CONV_HALO, :]
    dn = _layer_norm(conv, clg_ref[...], clb_ref[...])
    o_ref[0, :, GMLP_WIDTH:GMLP_WIDTH + CONV_CH] = (dn * jax.nn.sigmoid(dn)).astype(BF16)


def _odd_mix(x, norm_g, w_in, b_in, ln_g, ln_b, w_s, bs_tile, conv_w, conv_b, cln_g, cln_b):
    B, S, D = x.shape
    tm = 512
    const2 = lambda b, t: (0, 0)
    full = lambda a: pl.BlockSpec(a.shape, (lambda b, t: (0,) * a.ndim))
    n_out = GMLP_WIDTH + CONV_CH
    span = tm + 8 * ((CONV_KERNEL - 1) // 8)
    args = (x, norm_g, w_in, b_in, ln_g, ln_b, w_s, bs_tile, conv_w, conv_b, cln_g, cln_b)
    return pl.pallas_call(
        functools.partial(_odd_mix_kernel, tm=tm),
        grid=(B, S // tm),
        in_specs=[pl.BlockSpec((1, tm, D), lambda b, t: (b, t, 0))] + [full(a) for a in args[1:]],
        out_specs=pl.BlockSpec((1, tm, n_out), lambda b, t: (b, t, 0)),
        out_shape=jax.ShapeDtypeStruct((B, S, n_out), BF16),
        scratch_shapes=[pltpu.VMEM((span + 8, CONV_CH), F32)],
        compiler_params=pltpu.CompilerParams(
            dimension_semantics=("arbitrary", "arbitrary"),
            vmem_limit_bytes=VMEM_LIMIT_BYTES),
        name="odd_mix",
    )(*args)


def _rope_tables(seq, dim, seg_per_block):
    inv = 1.0 / (ROPE_THETA ** (jnp.arange(0, dim, 2, dtype=F32) / dim))
    ang = jnp.arange(seq, dtype=F32)[:, None] * inv[None, :]
    cos, sin = jnp.cos(ang), jnp.sin(ang)
    cos = jnp.tile(jnp.concatenate([cos, cos], axis=1), (1, seg_per_block))
    sin = jnp.tile(jnp.concatenate([-sin, sin], axis=1), (1, seg_per_block))
    return cos, sin


def _block_diag_ones(seg):
    idx = jnp.arange(BD) // seg
    return (idx[:, None] == idx[None, :]).astype(BF16)


def _rotate_half_matrix(seg):
    dst = jnp.arange(BD)
    src = jnp.where(dst % seg < seg // 2, dst + seg // 2, dst - seg // 2)
    return (jnp.arange(BD)[:, None] == src[None, :]).astype(BF16)


def kernel(x, attn_norm_g, ffn_norm_g, ffn_w_in, ffn_w_out, even_w_in, even_w_out,
           diff_q_norm_g, diff_k_norm_g, diff_lambda_q1, diff_lambda_k1,
           diff_lambda_q2, diff_lambda_k2, diff_subln_g, moba_q_norm_g, moba_k_norm_g,
           odd_w_in, odd_b_in, odd_w_out, gmlp_ln_g, gmlp_ln_b, gmlp_w_s, gmlp_b_s,
           conv_w, conv_b, conv_ln_g, conv_ln_b):
    B, S, D = x.shape
    assert S % EVEN_PROJ_ROWS == 0 and S // MOBA_BLOCK <= GATE_SEG
    row = lambda v: v.reshape(1, -1).astype(F32)
    ffn_w_in_bf = ffn_w_in.astype(BF16)
    ffn_w_out_bf = ffn_w_out.astype(BF16)

    lambda_init = 0.8 - 0.6 * math.exp(-0.3 * 0)
    cosd, sind = _rope_tables(S, DIFF_HALF, LANES // DIFF_HALF)
    cosm, sinm = _rope_tables(S, HEAD_DIM, LANES // HEAD_DIM)
    gains = jnp.stack([
        jnp.tile(diff_q_norm_g[0], SECTION // DIFF_HALF),
        jnp.tile(diff_k_norm_g[0], SECTION // DIFF_HALF),
        jnp.tile(moba_q_norm_g[0], SECTION // HEAD_DIM),
        jnp.tile(moba_k_norm_g[0], SECTION // HEAD_DIM)]).astype(F32)
    qk, vt, bias = _even_proj(x, row(attn_norm_g[0]), even_w_in[0].astype(BF16),
                              _block_diag_ones(DIFF_HALF), _block_diag_ones(HEAD_DIM),
                              _rotate_half_matrix(DIFF_HALF), _rotate_half_matrix(HEAD_DIM), gains,
                              cosd, sind, cosm, sinm)
    key_block = jnp.arange(S)[:, None] // MOBA_BLOCK
    ind = (key_block == (jnp.arange(LANES)[None, :] % GATE_SEG)).astype(BF16)
    lam_params = jnp.stack([diff_lambda_q1[0], diff_lambda_k1[0],
                            diff_lambda_q2[0], diff_lambda_k2[0]]).astype(F32)
    subln_tile = jnp.tile(diff_subln_g[0], LANES // HEAD_DIM).reshape(1, LANES).astype(F32)
    def score_bound(d, gq, gk):
        return (d * d ** -0.5 * LOG2E * (1.0 + 2.0 ** -7)
                * jnp.max(jnp.abs(gq.astype(F32))) * jnp.max(jnp.abs(gk.astype(F32))))

    bounded = jnp.logical_and(score_bound(DIFF_HALF, diff_q_norm_g[0], diff_k_norm_g[0]) <= SCORE_BOUND,
                              score_bound(HEAD_DIM, moba_q_norm_g[0], moba_k_norm_g[0]) <= SCORE_BOUND)
    attend = lambda flag: (lambda *a: _attention(*a, lambda_init, flag))
    diff, moba = lax.cond(bounded, attend(True), attend(False), qk, vt, bias, ind, lam_params, subln_tile)
    x = _out_ffn(diff, 0, moba, 0, x, even_w_out[0].astype(BF16), row(ffn_norm_g[0]),
                 ffn_w_in_bf, ffn_w_out_bf, 0)

    bs_tile = jnp.repeat(gmlp_b_s[0].T, GMLP_WIDTH // GMLP_GROUPS, axis=1).astype(F32)
    mix = _odd_mix(x, row(attn_norm_g[1]), odd_w_in[0].astype(BF16), row(odd_b_in[0]),
                   row(gmlp_ln_g[0]), row(gmlp_ln_b[0]), gmlp_w_s[0].astype(F32), bs_tile,
                   conv_w[0].astype(F32), row(conv_b[0]), row(conv_ln_g[0]), row(conv_ln_b[0]))
    x = _out_ffn(mix, 0, mix, 1, x, odd_w_out[0].astype(BF16), row(ffn_norm_g[1]),
                 ffn_w_in_bf, ffn_w_out_bf, 1)
    return x
```

```python
import functools
import math

import jax
import jax.numpy as jnp
from jax import lax
from jax.experimental import pallas as pl
from jax.experimental.pallas import tpu as pltpu

F32 = jnp.float32
BF16 = jnp.bfloat16

LANES = 128
VMEM_LIMIT_BYTES = 56 * 1024 * 1024

HEAD_DIM = 64
DIFF_HALF = 32
N_HEADS_DIFF = 8
N_HEADS_MOBA = 8
MOBA_BLOCK = 256
MOBA_TOPK = 3
GMLP_CHUNK = 128
GMLP_GROUPS = 8
GMLP_WIDTH = 512
CONV_CH = 512
CONV_KERNEL = 31
ROPE_THETA = 10000.0
EPS = 1e-6

SECTION = 512
BD = 256
EVEN_PROJ_ROWS = 1024
ATTN_STREAMS = 2
SCORE_BOUND = 40.0
ONES_ROWS = 16
GATE_SEG = LANES // N_HEADS_MOBA
CONV_HALO = 32
MASKED_BIAS = -1e30
M_INIT = -1e29
LOG2E = math.log2(math.e)

_NT = (((1,), (1,)), ((), ()))


def _rms(x, g):
    return x * lax.rsqrt(jnp.mean(x * x, axis=-1, keepdims=True) + EPS) * g


def _layer_norm(x, g, b):
    mu = jnp.mean(x, axis=-1, keepdims=True)
    xc = x - mu
    return xc * lax.rsqrt(jnp.mean(xc * xc, axis=-1, keepdims=True) + EPS) * g + b


def _seg_allreduce(x, op, seg, lane):
    s = 1
    while s < seg:
        up = pltpu.roll(x, LANES - s, 1)
        dn = pltpu.roll(x, s, 1)
        x = op(x, jnp.where((lane & s) == 0, up, dn))
        s *= 2
    return x


def _even_proj_kernel(x_ref, g_ref, w_ref, bdd_ref, bdm_ref, pd_ref, pm_ref, gain_ref,
                      cosd_ref, sind_ref, cosm_ref, sinm_ref,
                      qk_ref, vt_ref, bias_ref, kbt_ref, *, tm):
    blocks = tm // MOBA_BLOCK
    first_blk = pl.program_id(1) * blocks
    h = _rms(x_ref[0], g_ref[...]).astype(BF16)
    lane = lax.broadcasted_iota(jnp.int32, (tm, LANES), 1)

    def tiles_dot(a, m_ref):
        return jnp.concatenate(
            [jnp.dot(a[:, c * BD:(c + 1) * BD], m_ref[...], preferred_element_type=F32)
             for c in range(SECTION // BD)], axis=1)

    def qk_section(sec, bd_ref, perm_ref, seg, cos_ref, sin_ref):
        y = jnp.dot(h, w_ref[:, sec * SECTION:(sec + 1) * SECTION], preferred_element_type=F32)
        ss = tiles_dot((y * y).astype(BF16), bd_ref)
        yn = y * lax.rsqrt(ss * (1.0 / seg) + EPS) * gain_ref[sec_row[sec]:sec_row[sec] + 1, :]
        rot = tiles_dot(yn.astype(BF16), perm_ref)
        cos = cos_ref[...]
        sin = sin_ref[...]
        return [yn[:, c * LANES:(c + 1) * LANES] * cos + rot[:, c * LANES:(c + 1) * LANES] * sin
                for c in range(SECTION // LANES)]

    def store(out_sec, chunks, scale):
        for c, yc in enumerate(chunks):
            lo = out_sec * SECTION + c * LANES
            qk_ref[0, :, lo:lo + LANES] = (yc * scale).astype(BF16)

    sec_row = {0: 0, 1: 1, 3: 2, 4: 3}
    store(0, qk_section(0, bdd_ref, pd_ref, DIFF_HALF, cosd_ref, sind_ref), DIFF_HALF ** -0.5 * LOG2E)
    store(1, qk_section(1, bdd_ref, pd_ref, DIFF_HALF, cosd_ref, sind_ref), 1.0)
    mq = qk_section(3, bdm_ref, pm_ref, HEAD_DIM, cosm_ref, sinm_ref)
    store(2, mq, HEAD_DIM ** -0.5 * LOG2E)
    mk = qk_section(4, bdm_ref, pm_ref, HEAD_DIM, cosm_ref, sinm_ref)
    store(3, mk, 1.0)
    for out_sec, sec in enumerate((2, 5)):
        lo = sec * SECTION
        vt = jnp.dot(h, w_ref[:, lo:lo + SECTION], preferred_element_type=F32).T.astype(BF16)
        for sb in range(blocks):
            vt_ref[0, sb, out_sec * SECTION:(out_sec + 1) * SECTION, :] = (
                vt[:, sb * MOBA_BLOCK:(sb + 1) * MOBA_BLOCK])

    @pl.when(first_blk == 0)
    def _():
        kbt_ref[...] = jnp.zeros_like(kbt_ref)

    mk_full = jnp.concatenate(mk, axis=1)
    head_of_lane = lax.broadcasted_iota(jnp.int32, (1, SECTION), 1) // HEAD_DIM
    for sb in range(blocks):
        kbar = jnp.mean(mk_full[sb * MOBA_BLOCK:(sb + 1) * MOBA_BLOCK], axis=0, keepdims=True)
        for hh in range(N_HEADS_MOBA):
            kbt_ref[pl.ds(hh * GATE_SEG + first_blk + sb, 1), :] = jnp.where(head_of_lane == hh, kbar, 0.0)

    def split(a):
        hi = a.astype(BF16)
        return hi, (a - hi.astype(F32)).astype(BF16)

    q_hi, q_lo = split(jnp.concatenate(mq, axis=1))
    k_hi, k_lo = split(kbt_ref[...])
    gate = lax.dot_general(jnp.concatenate([q_hi, q_lo, q_hi], axis=1),
                           jnp.concatenate([k_hi, k_hi, k_lo], axis=1), _NT,
                           preferred_element_type=F32)
    blk = lane & (GATE_SEG - 1)
    blk_f = blk.astype(F32)
    neg_inf = jnp.float32(-jnp.inf)
    own = first_blk + lax.broadcasted_iota(jnp.int32, (tm, LANES), 0) // MOBA_BLOCK
    avail = jnp.where(blk < own, gate, neg_inf)
    chosen = blk == own
    for _ in range(MOBA_TOPK):
        best = _seg_allreduce(avail, jnp.maximum, GATE_SEG, lane)
        cand = jnp.logical_and(avail == best, avail > neg_inf)
        first = _seg_allreduce(jnp.where(cand, blk_f, float(GATE_SEG)), jnp.minimum, GATE_SEG, lane)
        pick = blk_f == first
        chosen = jnp.logical_or(chosen, pick)
        avail = jnp.where(pick, neg_inf, avail)
    bias_ref[0] = jnp.where(chosen, 0.0, MASKED_BIAS).astype(BF16)


def _even_proj(x, norm_g, w_in, bdd, bdm, pd, pm, gains, cosd, sind, cosm, sinm):
    B, S, D = x.shape
    tm = EVEN_PROJ_ROWS
    blocks = tm // MOBA_BLOCK
    n_out = w_in.shape[1]
    n_qk = 4 * SECTION
    n_v = 2 * SECTION
    const = lambda b, t: (0, 0)
    tab = pl.BlockSpec((tm, LANES), lambda b, t: (t, 0))
    return pl.pallas_call(
        functools.partial(_even_proj_kernel, tm=tm),
        grid=(B, S // tm),
        in_specs=[
            pl.BlockSpec((1, tm, D), lambda b, t: (b, t, 0)),
            pl.BlockSpec((1, D), const),
            pl.BlockSpec((D, n_out), const, pipeline_mode=pl.Buffered(1)),
            pl.BlockSpec((BD, BD), const),
            pl.BlockSpec((BD, BD), const),
            pl.BlockSpec((BD, BD), const),
            pl.BlockSpec((BD, BD), const),
            pl.BlockSpec((4, SECTION), const),
            tab, tab, tab, tab,
        ],
        out_specs=[
            pl.BlockSpec((1, tm, n_qk), lambda b, t: (b, t, 0)),
            pl.BlockSpec((1, blocks, n_v, MOBA_BLOCK), lambda b, t: (b, t, 0, 0)),
            pl.BlockSpec((1, tm, LANES), lambda b, t: (b, t, 0)),
        ],
        out_shape=[
            jax.ShapeDtypeStruct((B, S, n_qk), BF16),
            jax.ShapeDtypeStruct((B, S // MOBA_BLOCK, n_v, MOBA_BLOCK), BF16),
            jax.ShapeDtypeStruct((B, S, LANES), BF16),
        ],
        scratch_shapes=[pltpu.VMEM((LANES, SECTION), F32)],
        compiler_params=pltpu.CompilerParams(
            dimension_semantics=("arbitrary", "arbitrary"),
            vmem_limit_bytes=VMEM_LIMIT_BYTES),
        name="even_proj",
    )(x, norm_g, w_in, bdd, bdm, pd, pm, gains, cosd, sind, cosm, sinm)


def _flash_head_pair(make_q, n, v_rows, k_ref, vt_ref, s_ref, p_ref, nq, t, emit, k_aug_ref=None):
    streams = k_ref.shape[0]
    per_stream = n // streams

    def scores(q_all, j):
        start = pl.multiple_of(j * t, t)
        out = []
        for g in range(streams):
            k = k_ref[g, pl.ds(start, t), :]
            if k_aug_ref is not None:
                k = jnp.concatenate([k, k_aug_ref[pl.ds(start, t), :]], axis=1)
            out.append(lax.dot_general(k, q_all[g], _NT, preferred_element_type=F32))
        return jnp.concatenate(out, axis=1)

    def softmax(s_all, stats, diagonal):
        ps, new_stats, alphas = [], [], []
        for i in range(n):
            m, l = stats[i]
            s = s_all[:, i * t:(i + 1) * t]
            if diagonal:
                key = lax.broadcasted_iota(jnp.int32, (t, t), 0)
                qry = lax.broadcasted_iota(jnp.int32, (t, t), 1)
                s = jnp.where(key <= qry, s, -jnp.inf)
            m_new = jnp.maximum(m, jnp.max(s, axis=0, keepdims=True))
            p = jnp.exp2(s - m_new)
            alpha = jnp.exp2(m - m_new)
            ps.append(p.astype(BF16))
            new_stats.append((m_new, alpha * l + jnp.sum(p, axis=0, keepdims=True)))
            alphas.append(alpha)
        return tuple(ps), tuple(new_stats), alphas

    def weighted_values(j, ps):
        out = []
        for g in range(streams):
            vt = vt_ref[g, j]
            for li, (lo, hi) in enumerate(v_rows):
                out.append(jnp.dot(vt[lo:hi], ps[g * per_stream + li], preferred_element_type=F32))
        return out

    def step(q_all, j, cur, nxt, carry):
        stats, accs = carry
        s_ref[nxt] = scores(q_all, j + 1)
        p_prev = tuple(p_ref[nxt, i] for i in range(n))
        pv = weighted_values(jnp.maximum(j - 1, 0), p_prev)
        pv = [jnp.where(j > 0, x, 0.0) for x in pv]
        p_cur, stats, alphas = softmax(s_ref[cur], stats, False)
        for i in range(n):
            p_ref[cur, i] = p_cur[i]
        accs = tuple(a * (acc + x) for a, acc, x in zip(alphas, accs, pv))
        return stats, accs

    def query_tile(qi, first, second, odd, next_first):
        q_all = make_q(qi)
        init = (tuple((jnp.full((1, t), M_INIT, F32), jnp.zeros((1, t), F32)) for _ in range(n)),
                tuple(jnp.zeros((hi - lo, t), F32) for _ in range(streams) for lo, hi in v_rows))
        carry = lax.fori_loop(
            0, qi // 2,
            lambda i, c: step(q_all, 2 * i + 1, second, first, step(q_all, 2 * i, first, second, c)), init)
        if odd:
            carry = step(q_all, qi - 1, first, second, carry)
        diag, prev = (second, first) if odd else (first, second)
        stats, accs = carry
        s_ref[next_first] = scores(make_q(jnp.minimum(qi + 1, nq - 1)), 0)
        pv_prev = weighted_values(jnp.maximum(qi - 1, 0), tuple(p_ref[prev, i] for i in range(n)))
        pv_prev = [jnp.where(qi > 0, x, 0.0) for x in pv_prev]
        p_cur, stats, alphas = softmax(s_ref[diag], stats, True)
        pv_cur = weighted_values(qi, p_cur)
        emit(qi, [(a * (acc + x) + y, l)
                  for a, acc, x, y, (_, l) in zip(alphas, accs, pv_prev, pv_cur, stats)])

    p_ref[...] = jnp.zeros(p_ref.shape, BF16)
    s_ref[0] = scores(make_q(0), 0)

    def pair_of_tiles(u, _):
        query_tile(2 * u, 0, 1, False, 2)
        query_tile(2 * u + 1, 2, 1, True, 0)
        return 0

    lax.fori_loop(0, nq // 2, pair_of_tiles, 0)


def _flash_bounded(make_q, n, v_rows, k_ref, vt_ref, s_ref, nq, t, emit, k_aug_ref=None):
    streams = k_ref.shape[0]
    per_stream = n // streams
    ones = jnp.ones((ONES_ROWS, t), BF16)
    width = per_stream * t
    key = lax.broadcasted_iota(jnp.int32, (t, width), 0)
    qry = lax.broadcasted_iota(jnp.int32, (t, width), 1) & (t - 1)

    def scores(q_all, j, slot):
        start = pl.multiple_of(j * t, t)
        for g in range(streams):
            k = k_ref[g, pl.ds(start, t), :]
            if k_aug_ref is not None:
                k = jnp.concatenate([k, k_aug_ref[pl.ds(start, t), :]], axis=1)
            s_ref[slot, g] = lax.dot_general(k, q_all[g], _NT, preferred_element_type=F32)

    def accumulate(j, slot, accs, diagonal):
        out = []
        for g in range(streams):
            p = jnp.exp2(s_ref[slot, g])
            if diagonal:
                p = jnp.where(key <= qry, p, 0.0)
            p = p.astype(BF16)
            vt = vt_ref[g, j]
            for li, (lo, hi) in enumerate(v_rows):
                lhs = jnp.concatenate([vt[lo:hi], ones], axis=0)
                out.append(accs[g * per_stream + li]
                           + jnp.dot(lhs, p[:, li * t:(li + 1) * t], preferred_element_type=F32))
        return tuple(out)

    def step(q_all, j, cur, nxt, accs):
        scores(q_all, j + 1, nxt)
        return accumulate(j, cur, accs, False)

    def query_tile(qi, first, second, odd, next_first):
        q_all = make_q(qi)
        init = tuple(jnp.zeros((hi - lo + ONES_ROWS, t), F32) for _ in range(streams) for lo, hi in v_rows)
        accs = lax.fori_loop(
            0, qi // 2,
            lambda i, a: step(q_all, 2 * i + 1, second, first, step(q_all, 2 * i, first, second, a)), init)
        if odd:
            accs = step(q_all, qi - 1, first, second, accs)
        scores(make_q(jnp.minimum(qi + 1, nq - 1)), 0, next_first)
        accs = accumulate(qi, second if odd else first, accs, True)
        rows = [hi - lo for _ in range(streams) for lo, hi in v_rows]
        emit(qi, [(a[:r], a[r:r + 1]) for a, r in zip(accs, rows)])

    scores(make_q(0), 0, 0)

    def pair_of_tiles(u, _):
        query_tile(2 * u, 0, 1, False, 2)
        query_tile(2 * u + 1, 2, 1, True, 0)
        return 0

    lax.fori_loop(0, nq // 2, pair_of_tiles, 0)


def _diff_attn_kernel(lam_ref, q_ref, k_ref, vt_ref, sg_ref, o_ref, *scratch, t, lambda_init, bounded):
    nq = q_ref.shape[1] // t
    seg = lax.broadcasted_iota(jnp.int32, (t, LANES), 1) // DIFF_HALF
    lam_p = lam_ref[...]
    lam = (jnp.exp(jnp.sum(lam_p[0:1] * lam_p[1:2], axis=1, keepdims=True))
           - jnp.exp(jnp.sum(lam_p[2:3] * lam_p[3:4], axis=1, keepdims=True)) + lambda_init)

    streams = q_ref.shape[0]

    def make_q(qi):
        out = []
        for g in range(streams):
            q = q_ref[g, pl.ds(pl.multiple_of(qi * t, t), t), :]
            out.append(jnp.concatenate([jnp.where(seg == i, q, jnp.zeros_like(q)) for i in range(4)], axis=0))
        return out

    def emit(qi, res):
        a = [acc * (1.0 / l) for acc, l in res]
        for g in range(streams):
            heads = []
            for hh in range(2):
                o = a[4 * g + 2 * hh] - lam * a[4 * g + 2 * hh + 1]
                heads.append(o * lax.rsqrt(jnp.mean(o * o, axis=0, keepdims=True) + EPS))
            o = jnp.concatenate(heads, axis=0).T
            o_ref[g, pl.ds(pl.multiple_of(qi * t, t), t), :] = (
                o * sg_ref[...] * (1.0 - lambda_init)).astype(BF16)

    v_rows = [(HEAD_DIM * (i // 2), HEAD_DIM * (i // 2 + 1)) for i in range(4)]
    if bounded:
        _flash_bounded(make_q, 4 * streams, v_rows, k_ref, vt_ref, *scratch, nq, t, emit)
    else:
        _flash_head_pair(make_q, 4 * streams, v_rows, k_ref, vt_ref, *scratch, nq, t, emit)


def _moba_attn_kernel(q_ref, bias_ref, k_ref, vt_ref, ind_ref, o_ref, *scratch, t, bounded):
    hp = pl.program_id(1)
    nq = q_ref.shape[1] // t
    lane = lax.broadcasted_iota(jnp.int32, (t, LANES), 1)

    streams = q_ref.shape[0]

    def make_q(qi):
        rows = pl.ds(pl.multiple_of(qi * t, t), t)
        out = []
        for g in range(streams):
            q = q_ref[g, rows, :]
            bias = bias_ref[g, rows, :]
            strips = []
            for hh in range(2):
                qz = jnp.where(lane // HEAD_DIM == hh, q, jnp.zeros_like(q))
                bz = jnp.where(lane // GATE_SEG == 2 * hp + hh, bias, jnp.zeros_like(bias))
                strips.append(jnp.concatenate([qz, bz], axis=1))
            out.append(jnp.concatenate(strips, axis=0))
        return out

    def emit(qi, res):
        for g in range(streams):
            o = jnp.concatenate([acc * (1.0 / l) for acc, l in res[2 * g:2 * g + 2]], axis=0)
            o_ref[g, pl.ds(pl.multiple_of(qi * t, t), t), :] = o.T.astype(BF16)

    v_rows = [(0, HEAD_DIM), (HEAD_DIM, 2 * HEAD_DIM)]
    if bounded:
        _flash_bounded(make_q, 2 * streams, v_rows, k_ref, vt_ref, *scratch, nq, t, emit, k_aug_ref=ind_ref)
    else:
        _flash_head_pair(make_q, 2 * streams, v_rows, k_ref, vt_ref, *scratch, nq, t, emit, k_aug_ref=ind_ref)


def _attention(qk, vt, bias, ind, lam_params, subln_tile, lambda_init, bounded):
    B, S, _ = qk.shape
    t = MOBA_BLOCK
    n_pairs = SECTION // LANES
    g = ATTN_STREAMS if B % ATTN_STREAMS == 0 else 1
    assert (S // t) % 2 == 0
    grid = (B // g, n_pairs)
    qk_spec = lambda sec: pl.BlockSpec((g, S, LANES), lambda b, p: (b, 0, sec * n_pairs + p))
    vt_spec = lambda sec: pl.BlockSpec((g, S // t, LANES, t), lambda b, p: (b, 0, sec * n_pairs + p, 0))
    out_spec = pl.BlockSpec((g, S, LANES), lambda b, p: (b, 0, p))
    params = pltpu.CompilerParams(
        dimension_semantics=("arbitrary", "arbitrary"),
        vmem_limit_bytes=VMEM_LIMIT_BYTES)
    out_shape = jax.ShapeDtypeStruct((B, S, SECTION), BF16)
    pipeline_scratch = lambda n: [pltpu.VMEM((3, g, t, n // g * t), F32)] if bounded else [
        pltpu.VMEM((3, t, n * t), F32), pltpu.VMEM((3, n, t, t), BF16)]

    diff = pl.pallas_call(
        functools.partial(_diff_attn_kernel, t=t, lambda_init=lambda_init, bounded=bounded),
        grid=grid,
        in_specs=[pl.BlockSpec((4, DIFF_HALF), lambda b, p: (0, 0)),
                  qk_spec(0), qk_spec(1), vt_spec(0),
                  pl.BlockSpec((1, LANES), lambda b, p: (0, 0))],
        out_specs=out_spec, out_shape=out_shape, compiler_params=params,
        scratch_shapes=pipeline_scratch(4 * g),
        name="diff_attn",
    )(lam_params, qk, qk, vt, subln_tile)

    moba = pl.pallas_call(
        functools.partial(_moba_attn_kernel, t=t, bounded=bounded),
        grid=grid,
        in_specs=[qk_spec(2),
                  pl.BlockSpec((g, S, LANES), lambda b, p: (b, 0, 0)),
                  qk_spec(3), vt_spec(1),
                  pl.BlockSpec((S, LANES), lambda b, p: (0, 0))],
        out_specs=out_spec, out_shape=out_shape, compiler_params=params,
        scratch_shapes=pipeline_scratch(2 * g),
        name="moba_attn",
    )(qk, bias, qk, vt, ind)
    return diff, moba


def _out_ffn_kernel(mixa_ref, mixb_ref, x_ref, wo_ref, g_ref, wi_ref, w2_ref, o_ref, *, d_ff, chunk):
    mix = jnp.concatenate([mixa_ref[0], mixb_ref[0]], axis=1)
    x1 = x_ref[0] + jnp.dot(mix, wo_ref[...], preferred_element_type=F32)
    hn = _rms(x1, g_ref[...]).astype(BF16)
    acc = x1
    for c in range(d_ff // chunk):
        lo = c * chunk
        g = jnp.dot(hn, wi_ref[0, :, lo:lo + chunk], preferred_element_type=F32)
        u = jnp.dot(hn, wi_ref[0, :, d_ff + lo:d_ff + lo + chunk], preferred_element_type=F32)
        a = (g * jax.nn.sigmoid(g) * u).astype(BF16)
        acc = acc + jnp.dot(a, w2_ref[0, lo:lo + chunk, :], preferred_element_type=F32)
    o_ref[0] = acc


def _out_ffn(mix_a, col_a, mix_b, col_b, x, w_out, norm_g, w_in, w2, layer):
    B, S, D = x.shape
    d_ff = w2.shape[1]
    tm = 512
    half = w_out.shape[0] // 2
    const = lambda b, t: (0, 0)
    resident = lambda shape: pl.BlockSpec(shape, const, pipeline_mode=pl.Buffered(1))
    slab = lambda a: pl.BlockSpec((1,) + a.shape[1:], lambda b, t: (layer, 0, 0), pipeline_mode=pl.Buffered(1))
    tile = pl.BlockSpec((1, tm, D), lambda b, t: (b, t, 0))
    mix_spec = lambda col: pl.BlockSpec((1, tm, half), lambda b, t: (b, t, col))
    return pl.pallas_call(
        functools.partial(_out_ffn_kernel, d_ff=d_ff, chunk=256),
        grid=(B, S // tm),
        in_specs=[mix_spec(col_a), mix_spec(col_b), tile, resident(w_out.shape), pl.BlockSpec((1, D), const),
                  slab(w_in), slab(w2)],
        out_specs=tile,
        out_shape=jax.ShapeDtypeStruct((B, S, D), F32),
        compiler_params=pltpu.CompilerParams(
            dimension_semantics=("arbitrary", "arbitrary"),
            vmem_limit_bytes=VMEM_LIMIT_BYTES),
        name="out_ffn",
    )(mix_a, mix_b, x, w_out, norm_g, w_in, w2)


def _odd_mix_kernel(x_ref, g_ref, w_ref, b_ref, lng_ref, lnb_ref, ws_ref, bs_ref,
                    cw_ref, cb_ref, clg_ref, clb_ref, o_ref, ext_ref, *, tm):
    ti = pl.program_id(1)
    h = _rms(x_ref[0], g_ref[...]).astype(BF16)

    def proj(lo, width):
        return jnp.dot(h, w_ref[:, lo:lo + width], preferred_element_type=F32) + b_ref[:, lo:lo + width]

    gu = jax.nn.gelu(proj(0, GMLP_WIDTH), approximate=True)
    gv = jax.nn.gelu(proj(GMLP_WIDTH, GMLP_WIDTH), approximate=True)
    gv = _layer_norm(gv, lng_ref[...], lnb_ref[...]).astype(BF16)
    T = GMLP_CHUNK
    tri = lax.broadcasted_iota(jnp.int32, (T, T), 1) <= lax.broadcasted_iota(jnp.int32, (T, T), 0)
    lane = lax.broadcasted_iota(jnp.int32, (T, LANES), 1)
    first = lane < (GMLP_WIDTH // GMLP_GROUPS)
    zero = jnp.zeros((T, LANES), BF16)
    for lb in range(GMLP_WIDTH // LANES):
        wpair = jnp.concatenate(
            [jnp.where(tri, ws_ref[2 * lb + i], 0.0).astype(BF16) for i in range(2)], axis=1)
        bs = bs_ref[:, lb * LANES:(lb + 1) * LANES]
        for c in range(tm // T):
            vc = gv[c * T:(c + 1) * T, lb * LANES:(lb + 1) * LANES]
            vpair = jnp.concatenate([jnp.where(first, vc, zero), jnp.where(first, zero, vc)], axis=0)
            sg = jnp.dot(wpair, vpair, preferred_element_type=F32) + bs
            o_ref[0, c * T:(c + 1) * T, lb * LANES:(lb + 1) * LANES] = (
                gu[c * T:(c + 1) * T, lb * LANES:(lb + 1) * LANES] * sg).astype(BF16)

    lo = 2 * GMLP_WIDTH
    cval = proj(lo, CONV_CH) * jax.nn.sigmoid(proj(lo + CONV_CH, CONV_CH))

    @pl.when(ti == 0)
    def _():
        ext_ref[0:CONV_HALO, :] = jnp.zeros((CONV_HALO, CONV_CH), F32)

    ext_ref[CONV_HALO:CONV_HALO + tm, :] = cval
    first_tap = CONV_HALO - (CONV_KERNEL - 1)
    ext = ext_ref[...]
    rows_total = ext.shape[0]
    conv = jnp.zeros((tm, CONV_CH), F32) + cb_ref[...]
    for r in range(8):
        rolled = ext if r == 0 else pltpu.roll(ext, rows_total - r, 0)
        for k in range(CONV_KERNEL):
            if (first_tap + k) % 8 == r:
                off = first_tap + k - r
                conv = conv + rolled[off:off + tm, :] * cw_ref[k:k + 1, :]
    ext_ref[0:CONV_HALO, :] = ext_ref[tm:tm + CONV_HALO, :]
    dn = _layer_norm(conv, clg_ref[...], clb_ref[...])
    o_ref[0, :, GMLP_WIDTH:GMLP_WIDTH + CONV_CH] = (dn * jax.nn.sigmoid(dn)).astype(BF16)


def _odd_mix(x, norm_g, w_in, b_in, ln_g, ln_b, w_s, bs_tile, conv_w, conv_b, cln_g, cln_b):
    B, S, D = x.shape
    tm = 512
    const2 = lambda b, t: (0, 0)
    full = lambda a: pl.BlockSpec(a.shape, (lambda b, t: (0,) * a.ndim))
    n_out = GMLP_WIDTH + CONV_CH
    span = tm + 8 * ((CONV_KERNEL - 1) // 8)
    args = (x, norm_g, w_in, b_in, ln_g, ln_b, w_s, bs_tile, conv_w, conv_b, cln_g, cln_b)
    return pl.pallas_call(
        functools.partial(_odd_mix_kernel, tm=tm),
        grid=(B, S // tm),
        in_specs=[pl.BlockSpec((1, tm, D), lambda b, t: (b, t, 0))] + [full(a) for a in args[1:]],
        out_specs=pl.BlockSpec((1, tm, n_out), lambda b, t: (b, t, 0)),
        out_shape=jax.ShapeDtypeStruct((B, S, n_out), BF16),
        scratch_shapes=[pltpu.VMEM((span + 8, CONV_CH), F32)],
        compiler_params=pltpu.CompilerParams(
            dimension_semantics=("arbitrary", "arbitrary"),
            vmem_limit_bytes=VMEM_LIMIT_BYTES),
        name="odd_mix",
    )(*args)


def _rope_tables(seq, dim, seg_per_block):
    inv = 1.0 / (ROPE_THETA ** (jnp.arange(0, dim, 2, dtype=F32) / dim))
    ang = jnp.arange(seq, dtype=F32)[:, None] * inv[None, :]
    cos, sin = jnp.cos(ang), jnp.sin(ang)
    cos = jnp.tile(jnp.concatenate([cos, cos], axis=1), (1, seg_per_block))
    sin = jnp.tile(jnp.concatenate([-sin, sin], axis=1), (1, seg_per_block))
    return cos, sin


def _block_diag_ones(seg):
    idx = jnp.arange(BD) // seg
    return (idx[:, None] == idx[None, :]).astype(BF16)


def _rotate_half_matrix(seg):
    dst = jnp.arange(BD)
    src = jnp.where(dst % seg < seg // 2, dst + seg // 2, dst - seg // 2)
    return (jnp.arange(BD)[:, None] == src[None, :]).astype(BF16)


def kernel(x, attn_norm_g, ffn_norm_g, ffn_w_in, ffn_w_out, even_w_in, even_w_out,
           diff_q_norm_g, diff_k_norm_g, diff_lambda_q1, diff_lambda_k1,
           diff_lambda_q2, diff_lambda_k2, diff_subln_g, moba_q_norm_g, moba_k_norm_g,
           odd_w_in, odd_b_in, odd_w_out, gmlp_ln_g, gmlp_ln_b, gmlp_w_s, gmlp_b_s,
           conv_w, conv_b, conv_ln_g, conv_ln_b):
    B, S, D = x.shape
    assert S % EVEN_PROJ_ROWS == 0 and S // MOBA_BLOCK <= GATE_SEG
    row = lambda v: v.reshape(1, -1).astype(F32)
    ffn_w_in_bf = ffn_w_in.astype(BF16)
    ffn_w_out_bf = ffn_w_out.astype(BF16)

    lambda_init = 0.8 - 0.6 * math.exp(-0.3 * 0)
    cosd, sind = _rope_tables(S, DIFF_HALF, LANES // DIFF_HALF)
    cosm, sinm = _rope_tables(S, HEAD_DIM, LANES // HEAD_DIM)
    gains = jnp.stack([
        jnp.tile(diff_q_norm_g[0], SECTION // DIFF_HALF),
        jnp.tile(diff_k_norm_g[0], SECTION // DIFF_HALF),
        jnp.tile(moba_q_norm_g[0], SECTION // HEAD_DIM),
        jnp.tile(moba_k_norm_g[0], SECTION // HEAD_DIM)]).astype(F32)
    qk, vt, bias = _even_proj(x, row(attn_norm_g[0]), even_w_in[0].astype(BF16),
                              _block_diag_ones(DIFF_HALF), _block_diag_ones(HEAD_DIM),
                              _rotate_half_matrix(DIFF_HALF), _rotate_half_matrix(HEAD_DIM), gains,
                              cosd, sind, cosm, sinm)
    key_block = jnp.arange(S)[:, None] // MOBA_BLOCK
    ind = (key_block == (jnp.arange(LANES)[None, :] % GATE_SEG)).astype(BF16)
    lam_params = jnp.stack([diff_lambda_q1[0], diff_lambda_k1[0],
                            diff_lambda_q2[0], diff_lambda_k2[0]]).astype(F32)
    subln_tile = jnp.tile(diff_subln_g[0], LANES // HEAD_DIM).reshape(1, LANES).astype(F32)
    def score_bound(d, gq, gk):
        return (d * d ** -0.5 * LOG2E * (1.0 + 2.0 ** -7)
                * jnp.max(jnp.abs(gq.astype(F32))) * jnp.max(jnp.abs(gk.astype(F32))))

    bounded = jnp.logical_and(score_bound(DIFF_HALF, diff_q_norm_g[0], diff_k_norm_g[0]) <= SCORE_BOUND,
                              score_bound(HEAD_DIM, moba_q_norm_g[0], moba_k_norm_g[0]) <= SCORE_BOUND)
    attend = lambda flag: (lambda *a: _attention(*a, lambda_init, flag))
    diff, moba = lax.cond(bounded, attend(True), attend(False), qk, vt, bias, ind, lam_params, subln_tile)
    x = _out_ffn(diff, 0, moba, 0, x, even_w_out[0].astype(BF16), row(ffn_norm_g[0]),
                 ffn_w_in_bf, ffn_w_out_bf, 0)

    bs_tile = jnp.repeat(gmlp_b_s[0].T, GMLP_WIDTH // GMLP_GROUPS, axis=1).astype(F32)
    mix = _odd_mix(x, row(attn_norm_g[1]), odd_w_in[0].astype(BF16), row(odd_b_in[0]),
                   row(gmlp_ln_g[0]), row(gmlp_ln_b[0]), gmlp_w_s[0].astype(F32), bs_tile,
                   conv_w[0].astype(F32), row(conv_b[0]), row(conv_ln_g[0]), row(conv_ln_b[0]))
    x = _out_ffn(mix, 0, mix, 1, x, odd_w_out[0].astype(BF16), row(ffn_norm_g[1]),
                 ffn_w_in_bf, ffn_w_out_bf, 1)
    return x
```

```python
import functools
import math

import jax
import jax.numpy as jnp
from jax import lax
from jax.experimental import pallas as pl
from jax.experimental.pallas import tpu as pltpu

F32 = jnp.float32
BF16 = jnp.bfloat16

LANES = 128
VMEM_LIMIT_BYTES = 56 * 1024 * 1024

HEAD_DIM = 64
DIFF_HALF = 32
N_HEADS_DIFF = 8
N_HEADS_MOBA = 8
MOBA_BLOCK = 256
MOBA_TOPK = 3
GMLP_CHUNK = 128
GMLP_GROUPS = 8
GMLP_WIDTH = 512
CONV_CH = 512
CONV_KERNEL = 31
ROPE_THETA = 10000.0
EPS = 1e-6

SECTION = 512
BD = 256
EVEN_PROJ_ROWS = 1024
ATTN_STREAMS = 2
SCORE_BOUND = 40.0
ONES_ROWS = 16
GATE_SEG = LANES // N_HEADS_MOBA
CONV_HALO = 32
MASKED_BIAS = -1e30
M_INIT = -1e29
LOG2E = math.log2(math.e)

_NT = (((1,), (1,)), ((), ()))


def _rms(x, g):
    return x * lax.rsqrt(jnp.mean(x * x, axis=-1, keepdims=True) + EPS) * g


def _layer_norm(x, g, b):
    mu = jnp.mean(x, axis=-1, keepdims=True)
    xc = x - mu
    return xc * lax.rsqrt(jnp.mean(xc * xc, axis=-1, keepdims=True) + EPS) * g + b


def _even_proj_kernel(x_ref, g_ref, w_ref, bdd_ref, bdm_ref, pd_ref, pm_ref, gain_ref,
                      cosd_ref, sind_ref, cosm_ref, sinm_ref,
                      qk_ref, vt_ref, bias_ref, kbt_ref, *, tm):
    blocks = tm // MOBA_BLOCK
    first_blk = pl.program_id(1) * blocks
    h = _rms(x_ref[0], g_ref[...]).astype(BF16)

    def tiles_dot(a, m_ref):
        return jnp.concatenate(
            [jnp.dot(a[:, c * BD:(c + 1) * BD], m_ref[...], preferred_element_type=F32)
             for c in range(SECTION // BD)], axis=1)

    def qk_section(sec, bd_ref, perm_ref, seg, cos_ref, sin_ref):
        y = jnp.dot(h, w_ref[:, sec * SECTION:(sec + 1) * SECTION], preferred_element_type=F32)
        ss = tiles_dot((y * y).astype(BF16), bd_ref)
        yn = y * lax.rsqrt(ss * (1.0 / seg) + EPS) * gain_ref[sec_row[sec]:sec_row[sec] + 1, :]
        rot = tiles_dot(yn.astype(BF16), perm_ref)
        cos = cos_ref[...]
        sin = sin_ref[...]
        return [yn[:, c * LANES:(c + 1) * LANES] * cos + rot[:, c * LANES:(c + 1) * LANES] * sin
                for c in range(SECTION // LANES)]

    def store(out_sec, chunks, scale):
        for c, yc in enumerate(chunks):
            lo = out_sec * SECTION + c * LANES
            qk_ref[0, :, lo:lo + LANES] = (yc * scale).astype(BF16)

    sec_row = {0: 0, 1: 1, 3: 2, 4: 3}
    store(0, qk_section(0, bdd_ref, pd_ref, DIFF_HALF, cosd_ref, sind_ref), DIFF_HALF ** -0.5 * LOG2E)
    store(1, qk_section(1, bdd_ref, pd_ref, DIFF_HALF, cosd_ref, sind_ref), 1.0)
    mq = qk_section(3, bdm_ref, pm_ref, HEAD_DIM, cosm_ref, sinm_ref)
    store(2, mq, HEAD_DIM ** -0.5 * LOG2E)
    mk = qk_section(4, bdm_ref, pm_ref, HEAD_DIM, cosm_ref, sinm_ref)
    store(3, mk, 1.0)
    for out_sec, sec in enumerate((2, 5)):
        lo = sec * SECTION
        vt = jnp.dot(h, w_ref[:, lo:lo + SECTION], preferred_element_type=F32).T.astype(BF16)
        for sb in range(blocks):
            vt_ref[0, sb, out_sec * SECTION:(out_sec + 1) * SECTION, :] = (
                vt[:, sb * MOBA_BLOCK:(sb + 1) * MOBA_BLOCK])

    @pl.when(first_blk == 0)
    def _():
        kbt_ref[...] = jnp.zeros_like(kbt_ref)

    mk_full = jnp.concatenate(mk, axis=1)
    head_of_lane = lax.broadcasted_iota(jnp.int32, (1, SECTION), 1) // HEAD_DIM
    for sb in range(blocks):
        kbar = jnp.mean(mk_full[sb * MOBA_BLOCK:(sb + 1) * MOBA_BLOCK], axis=0, keepdims=True)
        for hh in range(N_HEADS_MOBA):
            kbt_ref[pl.ds(hh * GATE_SEG + first_blk + sb, 1), :] = jnp.where(head_of_lane == hh, kbar, 0.0)

    def split(a):
        hi = a.astype(BF16)
        return hi, (a - hi.astype(F32)).astype(BF16)

    q_hi, q_lo = split(jnp.concatenate(mq, axis=1))
    k_hi, k_lo = split(kbt_ref[...])
    gate = lax.dot_general(jnp.concatenate([k_hi, k_hi, k_lo], axis=1),
                           jnp.concatenate([q_hi, q_lo, q_hi], axis=1), _NT,
                           preferred_element_type=F32)
    blk = lax.broadcasted_iota(jnp.int32, (GATE_SEG, tm), 0)
    blk_f = blk.astype(F32)
    neg_inf = jnp.float32(-jnp.inf)
    own = first_blk + lax.broadcasted_iota(jnp.int32, (GATE_SEG, tm), 1) // MOBA_BLOCK
    bias_rows = []
    for hh in range(N_HEADS_MOBA):
        avail = jnp.where(blk < own, gate[hh * GATE_SEG:(hh + 1) * GATE_SEG], neg_inf)
        chosen = blk == own
        for _ in range(MOBA_TOPK):
            best = jnp.max(avail, axis=0, keepdims=True)
            cand = jnp.logical_and(avail == best, avail > neg_inf)
            first = jnp.min(jnp.where(cand, blk_f, float(GATE_SEG)), axis=0, keepdims=True)
            pick = blk_f == first
            chosen = jnp.logical_or(chosen, pick)
            avail = jnp.where(pick, neg_inf, avail)
        bias_rows.append(jnp.where(chosen, 0.0, MASKED_BIAS))
    bias_ref[0] = jnp.concatenate(bias_rows, axis=0).T.astype(BF16)


def _even_proj(x, norm_g, w_in, bdd, bdm, pd, pm, gains, cosd, sind, cosm, sinm):
    B, S, D = x.shape
    tm = EVEN_PROJ_ROWS
    blocks = tm // MOBA_BLOCK
    n_out = w_in.shape[1]
    n_qk = 4 * SECTION
    n_v = 2 * SECTION
    const = lambda b, t: (0, 0)
    tab = pl.BlockSpec((tm, LANES), lambda b, t: (t, 0))
    return pl.pallas_call(
        functools.partial(_even_proj_kernel, tm=tm),
        grid=(B, S // tm),
        in_specs=[
            pl.BlockSpec((1, tm, D), lambda b, t: (b, t, 0)),
            pl.BlockSpec((1, D), const),
            pl.BlockSpec((D, n_out), const, pipeline_mode=pl.Buffered(1)),
            pl.BlockSpec((BD, BD), const),
            pl.BlockSpec((BD, BD), const),
            pl.BlockSpec((BD, BD), const),
            pl.BlockSpec((BD, BD), const),
            pl.BlockSpec((4, SECTION), const),
            tab, tab, tab, tab,
        ],
        out_specs=[
            pl.BlockSpec((1, tm, n_qk), lambda b, t: (b, t, 0)),
            pl.BlockSpec((1, blocks, n_v, MOBA_BLOCK), lambda b, t: (b, t, 0, 0)),
            pl.BlockSpec((1, tm, LANES), lambda b, t: (b, t, 0)),
        ],
        out_shape=[
            jax.ShapeDtypeStruct((B, S, n_qk), BF16),
            jax.ShapeDtypeStruct((B, S // MOBA_BLOCK, n_v, MOBA_BLOCK), BF16),
            jax.ShapeDtypeStruct((B, S, LANES), BF16),
        ],
        scratch_shapes=[pltpu.VMEM((LANES, SECTION), F32)],
        compiler_params=pltpu.CompilerParams(
            dimension_semantics=("arbitrary", "arbitrary"),
            vmem_limit_bytes=VMEM_LIMIT_BYTES),
        name="even_proj",
    )(x, norm_g, w_in, bdd, bdm, pd, pm, gains, cosd, sind, cosm, sinm)


def _flash_head_pair(make_q, n, v_rows, k_ref, vt_ref, s_ref, p_ref, nq, t, emit, k_aug_ref=None):
    streams = k_ref.shape[0]
    per_stream = n // streams

    def scores(q_all, j):
        start = pl.multiple_of(j * t, t)
        out = []
        for g in range(streams):
            k = k_ref[g, pl.ds(start, t), :]
            if k_aug_ref is not None:
                k = jnp.concatenate([k, k_aug_ref[pl.ds(start, t), :]], axis=1)
            out.append(lax.dot_general(k, q_all[g], _NT, preferred_element_type=F32))
        return jnp.concatenate(out, axis=1)

    def softmax(s_all, stats, diagonal):
        ps, new_stats, alphas = [], [], []
        for i in range(n):
            m, l = stats[i]
            s = s_all[:, i * t:(i + 1) * t]
            if diagonal:
                key = lax.broadcasted_iota(jnp.int32, (t, t), 0)
                qry = lax.broadcasted_iota(jnp.int32, (t, t), 1)
                s = jnp.where(key <= qry, s, -jnp.inf)
            m_new = jnp.maximum(m, jnp.max(s, axis=0, keepdims=True))
            p = jnp.exp2(s - m_new)
            alpha = jnp.exp2(m - m_new)
            ps.append(p.astype(BF16))
            new_stats.append((m_new, alpha * l + jnp.sum(p, axis=0, keepdims=True)))
            alphas.append(alpha)
        return tuple(ps), tuple(new_stats), alphas

    def weighted_values(j, ps):
        out = []
        for g in range(streams):
            vt = vt_ref[g, j]
            for li, (lo, hi) in enumerate(v_rows):
                out.append(jnp.dot(vt[lo:hi], ps[g * per_stream + li], preferred_element_type=F32))
        return out

    def step(q_all, j, cur, nxt, carry):
        stats, accs = carry
        s_ref[nxt] = scores(q_all, j + 1)
        p_prev = tuple(p_ref[nxt, i] for i in range(n))
        pv = weighted_values(jnp.maximum(j - 1, 0), p_prev)
        pv = [jnp.where(j > 0, x, 0.0) for x in pv]
        p_cur, stats, alphas = softmax(s_ref[cur], stats, False)
        for i in range(n):
            p_ref[cur, i] = p_cur[i]
        accs = tuple(a * (acc + x) for a, acc, x in zip(alphas, accs, pv))
        return stats, accs

    def query_tile(qi, first, second, odd, next_first):
        q_all = make_q(qi)
        init = (tuple((jnp.full((1, t), M_INIT, F32), jnp.zeros((1, t), F32)) for _ in range(n)),
                tuple(jnp.zeros((hi - lo, t), F32) for _ in range(streams) for lo, hi in v_rows))
        carry = lax.fori_loop(
            0, qi // 2,
            lambda i, c: step(q_all, 2 * i + 1, second, first, step(q_all, 2 * i, first, second, c)), init)
        if odd:
            carry = step(q_all, qi - 1, first, second, carry)
        diag, prev = (second, first) if odd else (first, second)
        stats, accs = carry
        s_ref[next_first] = scores(make_q(jnp.minimum(qi + 1, nq - 1)), 0)
        pv_prev = weighted_values(jnp.maximum(qi - 1, 0), tuple(p_ref[prev, i] for i in range(n)))
        pv_prev = [jnp.where(qi > 0, x, 0.0) for x in pv_prev]
        p_cur, stats, alphas = softmax(s_ref[diag], stats, True)
        pv_cur = weighted_values(qi, p_cur)
        emit(qi, [(a * (acc + x) + y, l)
                  for a, acc, x, y, (_, l) in zip(alphas, accs, pv_prev, pv_cur, stats)])

    p_ref[...] = jnp.zeros(p_ref.shape, BF16)
    s_ref[0] = scores(make_q(0), 0)

    def pair_of_tiles(u, _):
        query_tile(2 * u, 0, 1, False, 2)
        query_tile(2 * u + 1, 2, 1, True, 0)
        return 0

    lax.fori_loop(0, nq // 2, pair_of_tiles, 0)


def _flash_bounded(make_q, n, v_rows, k_ref, vt_ref, s_ref, nq, t, emit, k_aug_ref=None):
    streams = k_ref.shape[0]
    per_stream = n // streams
    ones = jnp.ones((ONES_ROWS, t), BF16)
    width = per_stream * t
    key = lax.broadcasted_iota(jnp.int32, (t, width), 0)
    qry = lax.broadcasted_iota(jnp.int32, (t, width), 1) & (t - 1)

    def scores(q_all, j, slot):
        start = pl.multiple_of(j * t, t)
        for g in range(streams):
            k = k_ref[g, pl.ds(start, t), :]
            if k_aug_ref is not None:
                k = jnp.concatenate([k, k_aug_ref[pl.ds(start, t), :]], axis=1)
            s_ref[slot, g] = lax.dot_general(k, q_all[g], _NT, preferred_element_type=F32)

    def accumulate(j, slot, accs, diagonal):
        out = []
        for g in range(streams):
            p = jnp.exp2(s_ref[slot, g])
            if diagonal:
                p = jnp.where(key <= qry, p, 0.0)
            p = p.astype(BF16)
            vt = vt_ref[g, j]
            for li, (lo, hi) in enumerate(v_rows):
                lhs = jnp.concatenate([vt[lo:hi], ones], axis=0)
                out.append(accs[g * per_stream + li]
                           + jnp.dot(lhs, p[:, li * t:(li + 1) * t], preferred_element_type=F32))
        return tuple(out)

    def step(q_all, j, cur, nxt, accs):
        scores(q_all, j + 1, nxt)
        return accumulate(j, cur, accs, False)

    def query_tile(qi, first, second, odd, next_first):
        q_all = make_q(qi)
        init = tuple(jnp.zeros((hi - lo + ONES_ROWS, t), F32) for _ in range(streams) for lo, hi in v_rows)
        accs = lax.fori_loop(
            0, qi // 2,
            lambda i, a: step(q_all, 2 * i + 1, second, first, step(q_all, 2 * i, first, second, a)), init)
        if odd:
            accs = step(q_all, qi - 1, first, second, accs)
        scores(make_q(jnp.minimum(qi + 1, nq - 1)), 0, next_first)
        accs = accumulate(qi, second if odd else first, accs, True)
        rows = [hi - lo for _ in range(streams) for lo, hi in v_rows]
        emit(qi, [(a[:r], a[r:r + 1]) for a, r in zip(accs, rows)])

    scores(make_q(0), 0, 0)

    def pair_of_tiles(u, _):
        query_tile(2 * u, 0, 1, False, 2)
        query_tile(2 * u + 1, 2, 1, True, 0)
        return 0

    lax.fori_loop(0, nq // 2, pair_of_tiles, 0)


def _diff_attn_kernel(lam_ref, q_ref, k_ref, vt_ref, sg_ref, o_ref, *scratch, t, lambda_init, bounded):
    nq = q_ref.shape[1] // t
    seg = lax.broadcasted_iota(jnp.int32, (t, LANES), 1) // DIFF_HALF
    lam_p = lam_ref[...]
    lam = (jnp.exp(jnp.sum(lam_p[0:1] * lam_p[1:2], axis=1, keepdims=True))
           - jnp.exp(jnp.sum(lam_p[2:3] * lam_p[3:4], axis=1, keepdims=True)) + lambda_init)

    streams = q_ref.shape[0]

    def make_q(qi):
        out = []
        for g in range(streams):
            q = q_ref[g, pl.ds(pl.multiple_of(qi * t, t), t), :]
            out.append(jnp.concatenate([jnp.where(seg == i, q, jnp.zeros_like(q)) for i in range(4)], axis=0))
        return out

    def emit(qi, res):
        a = [acc * (1.0 / l) for acc, l in res]
        for g in range(streams):
            heads = []
            for hh in range(2):
                o = a[4 * g + 2 * hh] - lam * a[4 * g + 2 * hh + 1]
                heads.append(o * lax.rsqrt(jnp.mean(o * o, axis=0, keepdims=True) + EPS))
            o = jnp.concatenate(heads, axis=0).T
            o_ref[g, pl.ds(pl.multiple_of(qi * t, t), t), :] = (
                o * sg_ref[...] * (1.0 - lambda_init)).astype(BF16)

    v_rows = [(HEAD_DIM * (i // 2), HEAD_DIM * (i // 2 + 1)) for i in range(4)]
    if bounded:
        _flash_bounded(make_q, 4 * streams, v_rows, k_ref, vt_ref, *scratch, nq, t, emit)
    else:
        _flash_head_pair(make_q, 4 * streams, v_rows, k_ref, vt_ref, *scratch, nq, t, emit)


def _moba_attn_kernel(q_ref, bias_ref, k_ref, vt_ref, ind_ref, o_ref, *scratch, t, bounded):
    hp = pl.program_id(1)
    nq = q_ref.shape[1] // t
    lane = lax.broadcasted_iota(jnp.int32, (t, LANES), 1)

    streams = q_ref.shape[0]

    def make_q(qi):
        rows = pl.ds(pl.multiple_of(qi * t, t), t)
        out = []
        for g in range(streams):
            q = q_ref[g, rows, :]
            bias = bias_ref[g, rows, :]
            strips = []
            for hh in range(2):
                qz = jnp.where(lane // HEAD_DIM == hh, q, jnp.zeros_like(q))
                bz = jnp.where(lane // GATE_SEG == 2 * hp + hh, bias, jnp.zeros_like(bias))
                strips.append(jnp.concatenate([qz, bz], axis=1))
            out.append(jnp.concatenate(strips, axis=0))
        return out

    def emit(qi, res):
        for g in range(streams):
            o = jnp.concatenate([acc * (1.0 / l) for acc, l in res[2 * g:2 * g + 2]], axis=0)
            o_ref[g, pl.ds(pl.multiple_of(qi * t, t), t), :] = o.T.astype(BF16)

    v_rows = [(0, HEAD_DIM), (HEAD_DIM, 2 * HEAD_DIM)]
    if bounded:
        _flash_bounded(make_q, 2 * streams, v_rows, k_ref, vt_ref, *scratch, nq, t, emit, k_aug_ref=ind_ref)
    else:
        _flash_head_pair(make_q, 2 * streams, v_rows, k_ref, vt_ref, *scratch, nq, t, emit, k_aug_ref=ind_ref)


def _attention(qk, vt, bias, ind, lam_params, subln_tile, lambda_init, bounded):
    B, S, _ = qk.shape
    t = MOBA_BLOCK
    n_pairs = SECTION // LANES
    g = ATTN_STREAMS if B % ATTN_STREAMS == 0 else 1
    assert (S // t) % 2 == 0
    grid = (B // g, n_pairs)
    qk_spec = lambda sec: pl.BlockSpec((g, S, LANES), lambda b, p: (b, 0, sec * n_pairs + p))
    vt_spec = lambda sec: pl.BlockSpec((g, S // t, LANES, t), lambda b, p: (b, 0, sec * n_pairs + p, 0))
    out_spec = pl.BlockSpec((g, S, LANES), lambda b, p: (b, 0, p))
    params = pltpu.CompilerParams(
        dimension_semantics=("arbitrary", "arbitrary"),
        vmem_limit_bytes=VMEM_LIMIT_BYTES)
    out_shape = jax.ShapeDtypeStruct((B, S, SECTION), BF16)
    pipeline_scratch = lambda n: [pltpu.VMEM((3, g, t, n // g * t), F32)] if bounded else [
        pltpu.VMEM((3, t, n * t), F32), pltpu.VMEM((3, n, t, t), BF16)]

    diff = pl.pallas_call(
        functools.partial(_diff_attn_kernel, t=t, lambda_init=lambda_init, bounded=bounded),
        grid=grid,
        in_specs=[pl.BlockSpec((4, DIFF_HALF), lambda b, p: (0, 0)),
                  qk_spec(0), qk_spec(1), vt_spec(0),
                  pl.BlockSpec((1, LANES), lambda b, p: (0, 0))],
        out_specs=out_spec, out_shape=out_shape, compiler_params=params,
        scratch_shapes=pipeline_scratch(4 * g),
        name="diff_attn",
    )(lam_params, qk, qk, vt, subln_tile)

    moba = pl.pallas_call(
        functools.partial(_moba_attn_kernel, t=t, bounded=bounded),
        grid=grid,
        in_specs=[qk_spec(2),
                  pl.BlockSpec((g, S, LANES), lambda b, p: (b, 0, 0)),
                  qk_spec(3), vt_spec(1),
                  pl.BlockSpec((S, LANES), lambda b, p: (0, 0))],
        out_specs=out_spec, out_shape=out_shape, compiler_params=params,
        scratch_shapes=pipeline_scratch(2 * g),
        name="moba_attn",
    )(qk, bias, qk, vt, ind)
    return diff, moba


def _out_ffn_kernel(mixa_ref, mixb_ref, x_ref, wo_ref, g_ref, wi_ref, w2_ref, o_ref, *, d_ff, chunk):
    mix = jnp.concatenate([mixa_ref[0], mixb_ref[0]], axis=1)
    x1 = x_ref[0] + jnp.dot(mix, wo_ref[...], preferred_element_type=F32)
    hn = _rms(x1, g_ref[...]).astype(BF16)
    acc = x1
    for c in range(d_ff // chunk):
        lo = c * chunk
        g = jnp.dot(hn, wi_ref[0, :, lo:lo + chunk], preferred_element_type=F32)
        u = jnp.dot(hn, wi_ref[0, :, d_ff + lo:d_ff + lo + chunk], preferred_element_type=F32)
        a = (g * jax.nn.sigmoid(g) * u).astype(BF16)
        acc = acc + jnp.dot(a, w2_ref[0, lo:lo + chunk, :], preferred_element_type=F32)
    o_ref[0] = acc


def _out_ffn(mix_a, col_a, mix_b, col_b, x, w_out, norm_g, w_in, w2, layer):
    B, S, D = x.shape
    d_ff = w2.shape[1]
    tm = 512
    half = w_out.shape[0] // 2
    const = lambda b, t: (0, 0)
    resident = lambda shape: pl.BlockSpec(shape, const, pipeline_mode=pl.Buffered(1))
    slab = lambda a: pl.BlockSpec((1,) + a.shape[1:], lambda b, t: (layer, 0, 0), pipeline_mode=pl.Buffered(1))
    tile = pl.BlockSpec((1, tm, D), lambda b, t: (b, t, 0))
    mix_spec = lambda col: pl.BlockSpec((1, tm, half), lambda b, t: (b, t, col))
    return pl.pallas_call(
        functools.partial(_out_ffn_kernel, d_ff=d_ff, chunk=256),
        grid=(B, S // tm),
        in_specs=[mix_spec(col_a), mix_spec(col_b), tile, resident(w_out.shape), pl.BlockSpec((1, D), const),
                  slab(w_in), slab(w2)],
        out_specs=tile,
        out_shape=jax.ShapeDtypeStruct((B, S, D), F32),
        compiler_params=pltpu.CompilerParams(
            dimension_semantics=("arbitrary", "arbitrary"),
            vmem_limit_bytes=VMEM_LIMIT_BYTES),
        name="out_ffn",
    )(mix_a, mix_b, x, w_out, norm_g, w_in, w2)


def _odd_mix_kernel(x_ref, g_ref, w_ref, b_ref, lng_ref, lnb_ref, ws_ref, bs_ref,
                    cw_ref, cb_ref, clg_ref, clb_ref, o_ref, ext_ref, *, tm):
    ti = pl.program_id(1)
    h = _rms(x_ref[0], g_ref[...]).astype(BF16)

    def proj(lo, width):
        return jnp.dot(h, w_ref[:, lo:lo + width], preferred_element_type=F32) + b_ref[:, lo:lo + width]

    gu = jax.nn.gelu(proj(0, GMLP_WIDTH), approximate=True)
    gv = jax.nn.gelu(proj(GMLP_WIDTH, GMLP_WIDTH), approximate=True)
    gv = _layer_norm(gv, lng_ref[...], lnb_ref[...]).astype(BF16)
    T = GMLP_CHUNK
    tri = lax.broadcasted_iota(jnp.int32, (T, T), 1) <= lax.broadcasted_iota(jnp.int32, (T, T), 0)
    lane = lax.broadcasted_iota(jnp.int32, (T, LANES), 1)
    first = lane < (GMLP_WIDTH // GMLP_GROUPS)
    zero = jnp.zeros((T, LANES), BF16)
    for lb in range(GMLP_WIDTH // LANES):
        wpair = jnp.concatenate(
            [jnp.where(tri, ws_ref[2 * lb + i], 0.0).astype(BF16) for i in range(2)], axis=1)
        bs = bs_ref[:, lb * LANES:(lb + 1) * LANES]
        for c in range(tm // T):
            vc = gv[c * T:(c + 1) * T, lb * LANES:(lb + 1) * LANES]
            vpair = jnp.concatenate([jnp.where(first, vc, zero), jnp.where(first, zero, vc)], axis=0)
            sg = jnp.dot(wpair, vpair, preferred_element_type=F32) + bs
            o_ref[0, c * T:(c + 1) * T, lb * LANES:(lb + 1) * LANES] = (
                gu[c * T:(c + 1) * T, lb * LANES:(lb + 1) * LANES] * sg).astype(BF16)

    lo = 2 * GMLP_WIDTH
    cval = proj(lo, CONV_CH) * jax.nn.sigmoid(proj(lo + CONV_CH, CONV_CH))

    @pl.when(ti == 0)
    def _():
        ext_ref[0:CONV_HALO, :] = jnp.zeros((CONV_HALO, CONV_CH), F32)

    ext_ref[CONV_HALO:CONV_HALO + tm, :] = cval
    first_tap = CONV_HALO - (CONV_KERNEL - 1)
    ext = ext_ref[...]
    rows_total = ext.shape[0]
    conv = jnp.zeros((tm, CONV_CH), F32) + cb_ref[...]
    for r in range(8):
        rolled = ext if r == 0 else pltpu.roll(ext, rows_total - r, 0)
        for k in range(CONV_KERNEL):
            if (first_tap + k) % 8 == r:
                off = first_tap + k - r
                conv = conv + rolled[off:off + tm, :] * cw_ref[k:k + 1, :]
    ext_ref[0:CONV_HALO, :] = ext_ref[tm:tm + CONV_HALO, :]
    dn = _layer_norm(conv, clg_ref[...], clb_ref[...])
    o_ref[0, :, GMLP_WIDTH:GMLP_WIDTH + CONV_CH] = (dn * jax.nn.sigmoid(dn)).astype(BF16)


def _odd_mix(x, norm_g, w_in, b_in, ln_g, ln_b, w_s, bs_tile, conv_w, conv_b, cln_g, cln_b):
    B, S, D = x.shape
    tm = 512
    const2 = lambda b, t: (0, 0)
    full = lambda a: pl.BlockSpec(a.shape, (lambda b, t: (0,) * a.ndim))
    n_out = GMLP_WIDTH + CONV_CH
    span = tm + 8 * ((CONV_KERNEL - 1) // 8)
    args = (x, norm_g, w_in, b_in, ln_g, ln_b, w_s, bs_tile, conv_w, conv_b, cln_g, cln_b)
    return pl.pallas_call(
        functools.partial(_odd_mix_kernel, tm=tm),
        grid=(B, S // tm),
        in_specs=[pl.BlockSpec((1, tm, D), lambda b, t: (b, t, 0))] + [full(a) for a in args[1:]],
        out_specs=pl.BlockSpec((1, tm, n_out), lambda b, t: (b, t, 0)),
        out_shape=jax.ShapeDtypeStruct((B, S, n_out), BF16),
        scratch_shapes=[pltpu.VMEM((span + 8, CONV_CH), F32)],
        compiler_params=pltpu.CompilerParams(
            dimension_semantics=("arbitrary", "arbitrary"),
            vmem_limit_bytes=VMEM_LIMIT_BYTES),
        name="odd_mix",
    )(*args)


def _rope_tables(seq, dim, seg_per_block):
    inv = 1.0 / (ROPE_THETA ** (jnp.arange(0, dim, 2, dtype=F32) / dim))
    ang = jnp.arange(seq, dtype=F32)[:, None] * inv[None, :]
    cos, sin = jnp.cos(ang), jnp.sin(ang)
    cos = jnp.tile(jnp.concatenate([cos, cos], axis=1), (1, seg_per_block))
    sin = jnp.tile(jnp.concatenate([-sin, sin], axis=1), (1, seg_per_block))
    return cos, sin


def _block_diag_ones(seg):
    idx = jnp.arange(BD) // seg
    return (idx[:, None] == idx[None, :]).astype(BF16)


def _rotate_half_matrix(seg):
    dst = jnp.arange(BD)
    src = jnp.where(dst % seg < seg // 2, dst + seg // 2, dst - seg // 2)
    return (jnp.arange(BD)[:, None] == src[None, :]).astype(BF16)


def kernel(x, attn_norm_g, ffn_norm_g, ffn_w_in, ffn_w_out, even_w_in, even_w_out,
           diff_q_norm_g, diff_k_norm_g, diff_lambda_q1, diff_lambda_k1,
           diff_lambda_q2, diff_lambda_k2, diff_subln_g, moba_q_norm_g, moba_k_norm_g,
           odd_w_in, odd_b_in, odd_w_out, gmlp_ln_g, gmlp_ln_b, gmlp_w_s, gmlp_b_s,
           conv_w, conv_b, conv_ln_g, conv_ln_b):
    B, S, D = x.shape
    assert S % EVEN_PROJ_ROWS == 0 and S // MOBA_BLOCK <= GATE_SEG
    row = lambda v: v.reshape(1, -1).astype(F32)
    ffn_w_in_bf = ffn_w_in.astype(BF16)
    ffn_w_out_bf = ffn_w_out.astype(BF16)

    lambda_init = 0.8 - 0.6 * math.exp(-0.3 * 0)
    cosd, sind = _rope_tables(S, DIFF_HALF, LANES // DIFF_HALF)
    cosm, sinm = _rope_tables(S, HEAD_DIM, LANES // HEAD_DIM)
    gains = jnp.stack([
        jnp.tile(diff_q_norm_g[0], SECTION // DIFF_HALF),
        jnp.tile(diff_k_norm_g[0], SECTION // DIFF_HALF),
        jnp.tile(moba_q_norm_g[0], SECTION // HEAD_DIM),
        jnp.tile(moba_k_norm_g[0], SECTION // HEAD_DIM)]).astype(F32)
    qk, vt, bias = _even_proj(x, row(attn_norm_g[0]), even_w_in[0].astype(BF16),
                              _block_diag_ones(DIFF_HALF), _block_diag_ones(HEAD_DIM),
                              _rotate_half_matrix(DIFF_HALF), _rotate_half_matrix(HEAD_DIM), gains,
                              cosd, sind, cosm, sinm)
    key_block = jnp.arange(S)[:, None] // MOBA_BLOCK
    ind = (key_block == (jnp.arange(LANES)[None, :] % GATE_SEG)).astype(BF16)
    lam_params = jnp.stack([diff_lambda_q1[0], diff_lambda_k1[0],
                            diff_lambda_q2[0], diff_lambda_k2[0]]).astype(F32)
    subln_tile = jnp.tile(diff_subln_g[0], LANES // HEAD_DIM).reshape(1, LANES).astype(F32)
    def score_bound(d, gq, gk):
        return (d * d ** -0.5 * LOG2E * (1.0 + 2.0 ** -7)
                * jnp.max(jnp.abs(gq.astype(F32))) * jnp.max(jnp.abs(gk.astype(F32))))

    bounded = jnp.logical_and(score_bound(DIFF_HALF, diff_q_norm_g[0], diff_k_norm_g[0]) <= SCORE_BOUND,
                              score_bound(HEAD_DIM, moba_q_norm_g[0], moba_k_norm_g[0]) <= SCORE_BOUND)
    attend = lambda flag: (lambda *a: _attention(*a, lambda_init, flag))
    diff, moba = lax.cond(bounded, attend(True), attend(False), qk, vt, bias, ind, lam_params, subln_tile)
    x = _out_ffn(diff, 0, moba, 0, x, even_w_out[0].astype(BF16), row(ffn_norm_g[0]),
                 ffn_w_in_bf, ffn_w_out_bf, 0)

    bs_tile = jnp.repeat(gmlp_b_s[0].T, GMLP_WIDTH // GMLP_GROUPS, axis=1).astype(F32)
    mix = _odd_mix(x, row(attn_norm_g[1]), odd_w_in[0].astype(BF16), row(odd_b_in[0]),
                   row(gmlp_ln_g[0]), row(gmlp_ln_b[0]), gmlp_w_s[0].astype(F32), bs_tile,
                   conv_w[0].astype(F32), row(conv_b[0]), row(conv_ln_g[0]), row(conv_ln_b[0]))
    x = _out_ffn(mix, 0, mix, 1, x, odd_w_out[0].astype(BF16), row(ffn_norm_g[1]),
                 ffn_w_in_bf, ffn_w_out_bf, 1)
    return x
```

```python
import functools
import math

import jax
import jax.numpy as jnp
from jax import lax
from jax.experimental import pallas as pl
from jax.experimental.pallas import tpu as pltpu

F32 = jnp.float32
BF16 = jnp.bfloat16

LANES = 128
VMEM_LIMIT_BYTES = 56 * 1024 * 1024

HEAD_DIM = 64
DIFF_HALF = 32
N_HEADS_DIFF = 8
N_HEADS_MOBA = 8
MOBA_BLOCK = 256
MOBA_TOPK = 3
GMLP_CHUNK = 128
GMLP_GROUPS = 8
GMLP_WIDTH = 512
CONV_CH = 512
CONV_KERNEL = 31
ROPE_THETA = 10000.0
EPS = 1e-6

SECTION = 512
BD = 256
EVEN_PROJ_ROWS = 1024
ATTN_STREAMS = 2
MOBA_STREAMS_BOUNDED = 4
SCORE_BOUND = 40.0
ONES_ROWS = 16
GATE_SEG = LANES // N_HEADS_MOBA
CONV_HALO = 32
MASKED_BIAS = -1e30
M_INIT = -1e29
LOG2E = math.log2(math.e)

_NT = (((1,), (1,)), ((), ()))


def _rms(x, g):
    return x * lax.rsqrt(jnp.mean(x * x, axis=-1, keepdims=True) + EPS) * g


def _layer_norm(x, g, b):
    mu = jnp.mean(x, axis=-1, keepdims=True)
    xc = x - mu
    return xc * lax.rsqrt(jnp.mean(xc * xc, axis=-1, keepdims=True) + EPS) * g + b


def _even_proj_kernel(x_ref, g_ref, w_ref, bdd_ref, bdm_ref, pd_ref, pm_ref, gain_ref,
                      cosd_ref, sind_ref, cosm_ref, sinm_ref,
                      qk_ref, vt_ref, bias_ref, kbt_ref, *, tm):
    blocks = tm // MOBA_BLOCK
    first_blk = pl.program_id(1) * blocks
    h = _rms(x_ref[0], g_ref[...]).astype(BF16)

    def tiles_dot(a, m_ref):
        return jnp.concatenate(
            [jnp.dot(a[:, c * BD:(c + 1) * BD], m_ref[...], preferred_element_type=F32)
             for c in range(SECTION // BD)], axis=1)

    def qk_section(sec, bd_ref, perm_ref, seg, cos_ref, sin_ref):
        y = jnp.dot(h, w_ref[:, sec * SECTION:(sec + 1) * SECTION], preferred_element_type=F32)
        ss = tiles_dot((y * y).astype(BF16), bd_ref)
        yn = y * lax.rsqrt(ss * (1.0 / seg) + EPS) * gain_ref[sec_row[sec]:sec_row[sec] + 1, :]
        rot = tiles_dot(yn.astype(BF16), perm_ref)
        cos = cos_ref[...]
        sin = sin_ref[...]
        return [yn[:, c * LANES:(c + 1) * LANES] * cos + rot[:, c * LANES:(c + 1) * LANES] * sin
                for c in range(SECTION // LANES)]

    def store(out_sec, chunks, scale):
        for c, yc in enumerate(chunks):
            lo = out_sec * SECTION + c * LANES
            qk_ref[0, :, lo:lo + LANES] = (yc * scale).astype(BF16)

    sec_row = {0: 0, 1: 1, 3: 2, 4: 3}
    store(0, qk_section(0, bdd_ref, pd_ref, DIFF_HALF, cosd_ref, sind_ref), DIFF_HALF ** -0.5 * LOG2E)
    store(1, qk_section(1, bdd_ref, pd_ref, DIFF_HALF, cosd_ref, sind_ref), 1.0)
    mq = qk_section(3, bdm_ref, pm_ref, HEAD_DIM, cosm_ref, sinm_ref)
    store(2, mq, HEAD_DIM ** -0.5 * LOG2E)
    mk = qk_section(4, bdm_ref, pm_ref, HEAD_DIM, cosm_ref, sinm_ref)
    store(3, mk, 1.0)
    for out_sec, sec in enumerate((2, 5)):
        lo = sec * SECTION
        vt = jnp.dot(h, w_ref[:, lo:lo + SECTION], preferred_element_type=F32).T.astype(BF16)
        for sb in range(blocks):
            vt_ref[0, sb, out_sec * SECTION:(out_sec + 1) * SECTION, :] = (
                vt[:, sb * MOBA_BLOCK:(sb + 1) * MOBA_BLOCK])

    @pl.when(first_blk == 0)
    def _():
        kbt_ref[...] = jnp.zeros_like(kbt_ref)

    mk_full = jnp.concatenate(mk, axis=1)
    head_of_lane = lax.broadcasted_iota(jnp.int32, (1, SECTION), 1) // HEAD_DIM
    for sb in range(blocks):
        kbar = jnp.mean(mk_full[sb * MOBA_BLOCK:(sb + 1) * MOBA_BLOCK], axis=0, keepdims=True)
        for hh in range(N_HEADS_MOBA):
            kbt_ref[pl.ds(hh * GATE_SEG + first_blk + sb, 1), :] = jnp.where(head_of_lane == hh, kbar, 0.0)

    def split(a):
        hi = a.astype(BF16)
        return hi, (a - hi.astype(F32)).astype(BF16)

    q_hi, q_lo = split(jnp.concatenate(mq, axis=1))
    k_hi, k_lo = split(kbt_ref[...])
    gate = lax.dot_general(jnp.concatenate([k_hi, k_hi, k_lo], axis=1),
                           jnp.concatenate([q_hi, q_lo, q_hi], axis=1), _NT,
                           preferred_element_type=F32)
    blk = lax.broadcasted_iota(jnp.int32, (GATE_SEG, tm), 0)
    blk_f = blk.astype(F32)
    neg_inf = jnp.float32(-jnp.inf)
    own = first_blk + lax.broadcasted_iota(jnp.int32, (GATE_SEG, tm), 1) // MOBA_BLOCK
    bias_rows = []
    for hh in range(N_HEADS_MOBA):
        avail = jnp.where(blk < own, gate[hh * GATE_SEG:(hh + 1) * GATE_SEG], neg_inf)
        chosen = blk == own
        for _ in range(MOBA_TOPK):
            best = jnp.max(avail, axis=0, keepdims=True)
            cand = jnp.logical_and(avail == best, avail > neg_inf)
            first = jnp.min(jnp.where(cand, blk_f, float(GATE_SEG)), axis=0, keepdims=True)
            pick = blk_f == first
            chosen = jnp.logical_or(chosen, pick)
            avail = jnp.where(pick, neg_inf, avail)
        bias_rows.append(jnp.where(chosen, 0.0, MASKED_BIAS))
    bias_ref[0] = jnp.concatenate(bias_rows, axis=0).T.astype(BF16)


def _even_proj(x, norm_g, w_in, bdd, bdm, pd, pm, gains, cosd, sind, cosm, sinm):
    B, S, D = x.shape
    tm = EVEN_PROJ_ROWS
    blocks = tm // MOBA_BLOCK
    n_out = w_in.shape[1]
    n_qk = 4 * SECTION
    n_v = 2 * SECTION
    const = lambda b, t: (0, 0)
    tab = pl.BlockSpec((tm, LANES), lambda b, t: (t, 0))
    return pl.pallas_call(
        functools.partial(_even_proj_kernel, tm=tm),
        grid=(B, S // tm),
        in_specs=[
            pl.BlockSpec((1, tm, D), lambda b, t: (b, t, 0)),
            pl.BlockSpec((1, D), const),
            pl.BlockSpec((D, n_out), const, pipeline_mode=pl.Buffered(1)),
            pl.BlockSpec((BD, BD), const),
            pl.BlockSpec((BD, BD), const),
            pl.BlockSpec((BD, BD), const),
            pl.BlockSpec((BD, BD), const),
            pl.BlockSpec((4, SECTION), const),
            tab, tab, tab, tab,
        ],
        out_specs=[
            pl.BlockSpec((1, tm, n_qk), lambda b, t: (b, t, 0)),
            pl.BlockSpec((1, blocks, n_v, MOBA_BLOCK), lambda b, t: (b, t, 0, 0)),
            pl.BlockSpec((1, tm, LANES), lambda b, t: (b, t, 0)),
        ],
        out_shape=[
            jax.ShapeDtypeStruct((B, S, n_qk), BF16),
            jax.ShapeDtypeStruct((B, S // MOBA_BLOCK, n_v, MOBA_BLOCK), BF16),
            jax.ShapeDtypeStruct((B, S, LANES), BF16),
        ],
        scratch_shapes=[pltpu.VMEM((LANES, SECTION), F32)],
        compiler_params=pltpu.CompilerParams(
            dimension_semantics=("arbitrary", "arbitrary"),
            vmem_limit_bytes=VMEM_LIMIT_BYTES),
        name="even_proj",
    )(x, norm_g, w_in, bdd, bdm, pd, pm, gains, cosd, sind, cosm, sinm)


def _flash_head_pair(make_q, n, v_rows, k_ref, vt_ref, s_ref, p_ref, nq, t, emit, k_aug_ref=None):
    streams = k_ref.shape[0]
    per_stream = n // streams

    def scores(q_all, j):
        start = pl.multiple_of(j * t, t)
        out = []
        for g in range(streams):
            k = k_ref[g, pl.ds(start, t), :]
            if k_aug_ref is not None:
                k = jnp.concatenate([k, k_aug_ref[pl.ds(start, t), :]], axis=1)
            out.append(lax.dot_general(k, q_all[g], _NT, preferred_element_type=F32))
        return jnp.concatenate(out, axis=1)

    def softmax(s_all, stats, diagonal):
        ps, new_stats, alphas = [], [], []
        for i in range(n):
            m, l = stats[i]
            s = s_all[:, i * t:(i + 1) * t]
            if diagonal:
                key = lax.broadcasted_iota(jnp.int32, (t, t), 0)
                qry = lax.broadcasted_iota(jnp.int32, (t, t), 1)
                s = jnp.where(key <= qry, s, -jnp.inf)
            m_new = jnp.maximum(m, jnp.max(s, axis=0, keepdims=True))
            p = jnp.exp2(s - m_new)
            alpha = jnp.exp2(m - m_new)
            ps.append(p.astype(BF16))
            new_stats.append((m_new, alpha * l + jnp.sum(p, axis=0, keepdims=True)))
            alphas.append(alpha)
        return tuple(ps), tuple(new_stats), alphas

    def weighted_values(j, ps):
        out = []
        for g in range(streams):
            vt = vt_ref[g, j]
            for li, (lo, hi) in enumerate(v_rows):
                out.append(jnp.dot(vt[lo:hi], ps[g * per_stream + li], preferred_element_type=F32))
        return out

    def step(q_all, j, cur, nxt, carry):
        stats, accs = carry
        s_ref[nxt] = scores(q_all, j + 1)
        p_prev = tuple(p_ref[nxt, i] for i in range(n))
        pv = weighted_values(jnp.maximum(j - 1, 0), p_prev)
        pv = [jnp.where(j > 0, x, 0.0) for x in pv]
        p_cur, stats, alphas = softmax(s_ref[cur], stats, False)
        for i in range(n):
            p_ref[cur, i] = p_cur[i]
        accs = tuple(a * (acc + x) for a, acc, x in zip(alphas, accs, pv))
        return stats, accs

    def query_tile(qi, first, second, odd, next_first):
        q_all = make_q(qi)
        init = (tuple((jnp.full((1, t), M_INIT, F32), jnp.zeros((1, t), F32)) for _ in range(n)),
                tuple(jnp.zeros((hi - lo, t), F32) for _ in range(streams) for lo, hi in v_rows))
        carry = lax.fori_loop(
            0, qi // 2,
            lambda i, c: step(q_all, 2 * i + 1, second, first, step(q_all, 2 * i, first, second, c)), init)
        if odd:
            carry = step(q_all, qi - 1, first, second, carry)
        diag, prev = (second, first) if odd else (first, second)
        stats, accs = carry
        s_ref[next_first] = scores(make_q(jnp.minimum(qi + 1, nq - 1)), 0)
        pv_prev = weighted_values(jnp.maximum(qi - 1, 0), tuple(p_ref[prev, i] for i in range(n)))
        pv_prev = [jnp.where(qi > 0, x, 0.0) for x in pv_prev]
        p_cur, stats, alphas = softmax(s_ref[diag], stats, True)
        pv_cur = weighted_values(qi, p_cur)
        emit(qi, [(a * (acc + x) + y, l)
                  for a, acc, x, y, (_, l) in zip(alphas, accs, pv_prev, pv_cur, stats)])

    p_ref[...] = jnp.zeros(p_ref.shape, BF16)
    s_ref[0] = scores(make_q(0), 0)

    def pair_of_tiles(u, _):
        query_tile(2 * u, 0, 1, False, 2)
        query_tile(2 * u + 1, 2, 1, True, 0)
        return 0

    lax.fori_loop(0, nq // 2, pair_of_tiles, 0)


def _flash_bounded(make_q, n, v_rows, k_ref, vt_ref, s_ref, nq, t, emit, k_aug_ref=None):
    streams = k_ref.shape[0]
    per_stream = n // streams
    ones = jnp.ones((ONES_ROWS, t), BF16)
    width = per_stream * t
    key = lax.broadcasted_iota(jnp.int32, (t, width), 0)
    qry = lax.broadcasted_iota(jnp.int32, (t, width), 1) & (t - 1)

    def scores(q_all, j, slot):
        start = pl.multiple_of(j * t, t)
        for g in range(streams):
            k = k_ref[g, pl.ds(start, t), :]
            if k_aug_ref is not None:
                k = jnp.concatenate([k, k_aug_ref[pl.ds(start, t), :]], axis=1)
            s_ref[slot, g] = lax.dot_general(k, q_all[g], _NT, preferred_element_type=F32)

    def accumulate(j, slot, accs, diagonal):
        out = []
        for g in range(streams):
            p = jnp.exp2(s_ref[slot, g])
            if diagonal:
                p = jnp.where(key <= qry, p, 0.0)
            p = p.astype(BF16)
            vt = vt_ref[g, j]
            for li, (lo, hi) in enumerate(v_rows):
                lhs = jnp.concatenate([vt[lo:hi], ones], axis=0)
                out.append(accs[g * per_stream + li]
                           + jnp.dot(lhs, p[:, li * t:(li + 1) * t], preferred_element_type=F32))
        return tuple(out)

    def step(q_all, j, cur, nxt, accs):
        scores(q_all, j + 1, nxt)
        return accumulate(j, cur, accs, False)

    rows = [hi - lo for _ in range(streams) for lo, hi in v_rows]

    def query_tile(u, first, second, odd, next_first):
        qi = 2 * u + odd
        q_all = make_q(qi)

        def four_steps(i, a):
            j = 4 * i
            a = step(q_all, j, first, second, a)
            a = step(q_all, j + 1, second, first, a)
            a = step(q_all, j + 2, first, second, a)
            return step(q_all, j + 3, second, first, a)

        init = tuple(jnp.zeros((r + ONES_ROWS, t), F32) for r in rows)
        accs = lax.fori_loop(0, qi // 4, four_steps, init)

        def tail(accs, two_more):
            j = 4 * (qi // 4)
            if two_more:
                accs = step(q_all, j, first, second, accs)
                accs = step(q_all, j + 1, second, first, accs)
                j = j + 2
            if odd:
                accs = step(q_all, j, first, second, accs)
            scores(make_q(jnp.minimum(qi + 1, nq - 1)), 0, next_first)
            accs = accumulate(qi, second if odd else first, accs, True)
            emit(qi, [(a[:r], a[r:r + 1]) for a, r in zip(accs, rows)])

        pl.when((u & 1) == 0)(lambda: tail(accs, False))
        pl.when((u & 1) == 1)(lambda: tail(accs, True))

    scores(make_q(0), 0, 0)

    def pair_of_tiles(u, _):
        query_tile(u, 0, 1, 0, 2)
        query_tile(u, 2, 1, 1, 0)
        return 0

    lax.fori_loop(0, nq // 2, pair_of_tiles, 0)


def _diff_attn_kernel(lam_ref, q_ref, k_ref, vt_ref, sg_ref, o_ref, *scratch, t, lambda_init, bounded):
    nq = q_ref.shape[1] // t
    seg = lax.broadcasted_iota(jnp.int32, (t, LANES), 1) // DIFF_HALF
    lam_p = lam_ref[...]
    lam = (jnp.exp(jnp.sum(lam_p[0:1] * lam_p[1:2], axis=1, keepdims=True))
           - jnp.exp(jnp.sum(lam_p[2:3] * lam_p[3:4], axis=1, keepdims=True)) + lambda_init)

    streams = q_ref.shape[0]

    def make_q(qi):
        out = []
        for g in range(streams):
            q = q_ref[g, pl.ds(pl.multiple_of(qi * t, t), t), :]
            out.append(jnp.concatenate([jnp.where(seg == i, q, jnp.zeros_like(q)) for i in range(4)], axis=0))
        return out

    def emit(qi, res):
        a = [acc * (1.0 / l) for acc, l in res]
        for g in range(streams):
            heads = []
            for hh in range(2):
                o = a[4 * g + 2 * hh] - lam * a[4 * g + 2 * hh + 1]
                heads.append(o * lax.rsqrt(jnp.mean(o * o, axis=0, keepdims=True) + EPS))
            o = jnp.concatenate(heads, axis=0).T
            o_ref[g, pl.ds(pl.multiple_of(qi * t, t), t), :] = (
                o * sg_ref[...] * (1.0 - lambda_init)).astype(BF16)

    v_rows = [(HEAD_DIM * (i // 2), HEAD_DIM * (i // 2 + 1)) for i in range(4)]
    if bounded:
        _flash_bounded(make_q, 4 * streams, v_rows, k_ref, vt_ref, *scratch, nq, t, emit)
    else:
        _flash_head_pair(make_q, 4 * streams, v_rows, k_ref, vt_ref, *scratch, nq, t, emit)


def _moba_attn_kernel(q_ref, bias_ref, k_ref, vt_ref, ind_ref, o_ref, *scratch, t, bounded):
    hp = pl.program_id(1)
    nq = q_ref.shape[1] // t
    lane = lax.broadcasted_iota(jnp.int32, (t, LANES), 1)

    streams = q_ref.shape[0]

    def make_q(qi):
        rows = pl.ds(pl.multiple_of(qi * t, t), t)
        out = []
        for g in range(streams):
            q = q_ref[g, rows, :]
            bias = bias_ref[g, rows, :]
            strips = []
            for hh in range(2):
                qz = jnp.where(lane // HEAD_DIM == hh, q, jnp.zeros_like(q))
                bz = jnp.where(lane // GATE_SEG == 2 * hp + hh, bias, jnp.zeros_like(bias))
                strips.append(jnp.concatenate([qz, bz], axis=1))
            out.append(jnp.concatenate(strips, axis=0))
        return out

    def emit(qi, res):
        for g in range(streams):
            o = jnp.concatenate([acc * (1.0 / l) for acc, l in res[2 * g:2 * g + 2]], axis=0)
            o_ref[g, pl.ds(pl.multiple_of(qi * t, t), t), :] = o.T.astype(BF16)

    v_rows = [(0, HEAD_DIM), (HEAD_DIM, 2 * HEAD_DIM)]
    if bounded:
        _flash_bounded(make_q, 2 * streams, v_rows, k_ref, vt_ref, *scratch, nq, t, emit, k_aug_ref=ind_ref)
    else:
        _flash_head_pair(make_q, 2 * streams, v_rows, k_ref, vt_ref, *scratch, nq, t, emit, k_aug_ref=ind_ref)


def _attention(qk, vt, bias, ind, lam_params, subln_tile, lambda_init, bounded):
    B, S, _ = qk.shape
    t = MOBA_BLOCK
    n_pairs = SECTION // LANES
    assert (S // t) % 2 == 0
    params = pltpu.CompilerParams(
        dimension_semantics=("arbitrary", "arbitrary"),
        vmem_limit_bytes=VMEM_LIMIT_BYTES)
    out_shape = jax.ShapeDtypeStruct((B, S, SECTION), BF16)

    def streams(wanted):
        return wanted if B % wanted == 0 else 1

    def seq_spec(g, col):
        return pl.BlockSpec((g, S, LANES), lambda b, p: (b, 0, col(p)))

    def vt_spec(g, sec):
        return pl.BlockSpec((g, S // t, LANES, t), lambda b, p: (b, 0, sec * n_pairs + p, 0))

    def pipeline_scratch(g, n):
        if bounded:
            return [pltpu.VMEM((3, g, t, n * t), F32)]
        return [pltpu.VMEM((3, t, g * n * t), F32), pltpu.VMEM((3, g * n, t, t), BF16)]

    g = streams(ATTN_STREAMS)
    diff = pl.pallas_call(
        functools.partial(_diff_attn_kernel, t=t, lambda_init=lambda_init, bounded=bounded),
        grid=(B // g, n_pairs),
        in_specs=[pl.BlockSpec((4, DIFF_HALF), lambda b, p: (0, 0)),
                  seq_spec(g, lambda p: p), seq_spec(g, lambda p: n_pairs + p), vt_spec(g, 0),
                  pl.BlockSpec((1, LANES), lambda b, p: (0, 0))],
        out_specs=seq_spec(g, lambda p: p), out_shape=out_shape, compiler_params=params,
        scratch_shapes=pipeline_scratch(g, 4),
        name="diff_attn",
    )(lam_params, qk, qk, vt, subln_tile)

    g = streams(MOBA_STREAMS_BOUNDED if bounded else ATTN_STREAMS)
    moba = pl.pallas_call(
        functools.partial(_moba_attn_kernel, t=t, bounded=bounded),
        grid=(B // g, n_pairs),
        in_specs=[seq_spec(g, lambda p: 2 * n_pairs + p),
                  seq_spec(g, lambda p: 0),
                  seq_spec(g, lambda p: 3 * n_pairs + p), vt_spec(g, 1),
                  pl.BlockSpec((S, LANES), lambda b, p: (0, 0))],
        out_specs=seq_spec(g, lambda p: p), out_shape=out_shape, compiler_params=params,
        scratch_shapes=pipeline_scratch(g, 2),
        name="moba_attn",
    )(qk, bias, qk, vt, ind)
    return diff, moba


def _out_ffn_kernel(mixa_ref, mixb_ref, x_ref, wo_ref, g_ref, wi_ref, w2_ref, o_ref, *, d_ff, chunk):
    mix = jnp.concatenate([mixa_ref[0], mixb_ref[0]], axis=1)
    x1 = x_ref[0] + jnp.dot(mix, wo_ref[...], preferred_element_type=F32)
    hn = _rms(x1, g_ref[...]).astype(BF16)
    acc = x1
    for c in range(d_ff // chunk):
        lo = c * chunk
        g = jnp.dot(hn, wi_ref[0, :, lo:lo + chunk], preferred_element_type=F32)
        u = jnp.dot(hn, wi_ref[0, :, d_ff + lo:d_ff + lo + chunk], preferred_element_type=F32)
        a = (g * jax.nn.sigmoid(g) * u).astype(BF16)
        acc = acc + jnp.dot(a, w2_ref[0, lo:lo + chunk, :], preferred_element_type=F32)
    o_ref[0] = acc


def _out_ffn(mix_a, col_a, mix_b, col_b, x, w_out, norm_g, w_in, w2, layer):
    B, S, D = x.shape
    d_ff = w2.shape[1]
    tm = 512
    half = w_out.shape[0] // 2
    const = lambda b, t: (0, 0)
    resident = lambda shape: pl.BlockSpec(shape, const, pipeline_mode=pl.Buffered(1))
    slab = lambda a: pl.BlockSpec((1,) + a.shape[1:], lambda b, t: (layer, 0, 0), pipeline_mode=pl.Buffered(1))
    tile = pl.BlockSpec((1, tm, D), lambda b, t: (b, t, 0))
    mix_spec = lambda col: pl.BlockSpec((1, tm, half), lambda b, t: (b, t, col))
    return pl.pallas_call(
        functools.partial(_out_ffn_kernel, d_ff=d_ff, chunk=256),
        grid=(B, S // tm),
        in_specs=[mix_spec(col_a), mix_spec(col_b), tile, resident(w_out.shape), pl.BlockSpec((1, D), const),
                  slab(w_in), slab(w2)],
        out_specs=tile,
        out_shape=jax.ShapeDtypeStruct((B, S, D), F32),
        compiler_params=pltpu.CompilerParams(
            dimension_semantics=("arbitrary", "arbitrary"),
            vmem_limit_bytes=VMEM_LIMIT_BYTES),
        name="out_ffn",
    )(mix_a, mix_b, x, w_out, norm_g, w_in, w2)


def _odd_mix_kernel(x_ref, g_ref, w_ref, b_ref, lng_ref, lnb_ref, ws_ref, bs_ref,
                    cw_ref, cb_ref, clg_ref, clb_ref, o_ref, ext_ref, *, tm):
    ti = pl.program_id(1)
    h = _rms(x_ref[0], g_ref[...]).astype(BF16)

    def proj(lo, width):
        return jnp.dot(h, w_ref[:, lo:lo + width], preferred_element_type=F32) + b_ref[:, lo:lo + width]

    gu = jax.nn.gelu(proj(0, GMLP_WIDTH), approximate=True)
    gv = jax.nn.gelu(proj(GMLP_WIDTH, GMLP_WIDTH), approximate=True)
    gv = _layer_norm(gv, lng_ref[...], lnb_ref[...]).astype(BF16)
    T = GMLP_CHUNK
    tri = lax.broadcasted_iota(jnp.int32, (T, T), 1) <= lax.broadcasted_iota(jnp.int32, (T, T), 0)
    lane = lax.broadcasted_iota(jnp.int32, (T, LANES), 1)
    first = lane < (GMLP_WIDTH // GMLP_GROUPS)
    zero = jnp.zeros((T, LANES), BF16)
    for lb in range(GMLP_WIDTH // LANES):
        wpair = jnp.concatenate(
            [jnp.where(tri, ws_ref[2 * lb + i], 0.0).astype(BF16) for i in range(2)], axis=1)
        bs = bs_ref[:, lb * LANES:(lb + 1) * LANES]
        for c in range(tm // T):
            vc = gv[c * T:(c + 1) * T, lb * LANES:(lb + 1) * LANES]
            vpair = jnp.concatenate([jnp.where(first, vc, zero), jnp.where(first, zero, vc)], axis=0)
            sg = jnp.dot(wpair, vpair, preferred_element_type=F32) + bs
            o_ref[0, c * T:(c + 1) * T, lb * LANES:(lb + 1) * LANES] = (
                gu[c * T:(c + 1) * T, lb * LANES:(lb + 1) * LANES] * sg).astype(BF16)

    lo = 2 * GMLP_WIDTH
    cval = proj(lo, CONV_CH) * jax.nn.sigmoid(proj(lo + CONV_CH, CONV_CH))

    @pl.when(ti == 0)
    def _():
        ext_ref[0:CONV_HALO, :] = jnp.zeros((CONV_HALO, CONV_CH), F32)

    ext_ref[CONV_HALO:CONV_HALO + tm, :] = cval
    first_tap = CONV_HALO - (CONV_KERNEL - 1)
    ext = ext_ref[...]
    rows_total = ext.shape[0]
    conv = jnp.zeros((tm, CONV_CH), F32) + cb_ref[...]
    for r in range(8):
        rolled = ext if r == 0 else pltpu.roll(ext, rows_total - r, 0)
        for k in range(CONV_KERNEL):
            if (first_tap + k) % 8 == r:
                off = first_tap + k - r
                conv = conv + rolled[off:off + tm, :] * cw_ref[k:k + 1, :]
    ext_ref[0:CONV_HALO, :] = ext_ref[tm:tm + CONV_HALO, :]
    dn = _layer_norm(conv, clg_ref[...], clb_ref[...])
    o_ref[0, :, GMLP_WIDTH:GMLP_WIDTH + CONV_CH] = (dn * jax.nn.sigmoid(dn)).astype(BF16)


def _odd_mix(x, norm_g, w_in, b_in, ln_g, ln_b, w_s, bs_tile, conv_w, conv_b, cln_g, cln_b):
    B, S, D = x.shape
    tm = 512
    const2 = lambda b, t: (0, 0)
    full = lambda a: pl.BlockSpec(a.shape, (lambda b, t: (0,) * a.ndim))
    n_out = GMLP_WIDTH + CONV_CH
    span = tm + 8 * ((CONV_KERNEL - 1) // 8)
    args = (x, norm_g, w_in, b_in, ln_g, ln_b, w_s, bs_tile, conv_w, conv_b, cln_g, cln_b)
    return pl.pallas_call(
        functools.partial(_odd_mix_kernel, tm=tm),
        grid=(B, S // tm),
        in_specs=[pl.BlockSpec((1, tm, D), lambda b, t: (b, t, 0))] + [full(a) for a in args[1:]],
        out_specs=pl.BlockSpec((1, tm, n_out), lambda b, t: (b, t, 0)),
        out_shape=jax.ShapeDtypeStruct((B, S, n_out), BF16),
        scratch_shapes=[pltpu.VMEM((span + 8, CONV_CH), F32)],
        compiler_params=pltpu.CompilerParams(
            dimension_semantics=("arbitrary", "arbitrary"),
            vmem_limit_bytes=VMEM_LIMIT_BYTES),
        name="odd_mix",
    )(*args)


def _rope_tables(seq, dim, seg_per_block):
    inv = 1.0 / (ROPE_THETA ** (jnp.arange(0, dim, 2, dtype=F32) / dim))
    ang = jnp.arange(seq, dtype=F32)[:, None] * inv[None, :]
    cos, sin = jnp.cos(ang), jnp.sin(ang)
    cos = jnp.tile(jnp.concatenate([cos, cos], axis=1), (1, seg_per_block))
    sin = jnp.tile(jnp.concatenate([-sin, sin], axis=1), (1, seg_per_block))
    return cos, sin


def _block_diag_ones(seg):
    idx = jnp.arange(BD) // seg
    return (idx[:, None] == idx[None, :]).astype(BF16)


def _rotate_half_matrix(seg):
    dst = jnp.arange(BD)
    src = jnp.where(dst % seg < seg // 2, dst + seg // 2, dst - seg // 2)
    return (jnp.arange(BD)[:, None] == src[None, :]).astype(BF16)


def kernel(x, attn_norm_g, ffn_norm_g, ffn_w_in, ffn_w_out, even_w_in, even_w_out,
           diff_q_norm_g, diff_k_norm_g, diff_lambda_q1, diff_lambda_k1,
           diff_lambda_q2, diff_lambda_k2, diff_subln_g, moba_q_norm_g, moba_k_norm_g,
           odd_w_in, odd_b_in, odd_w_out, gmlp_ln_g, gmlp_ln_b, gmlp_w_s, gmlp_b_s,
           conv_w, conv_b, conv_ln_g, conv_ln_b):
    B, S, D = x.shape
    assert S % EVEN_PROJ_ROWS == 0 and S // MOBA_BLOCK <= GATE_SEG
    row = lambda v: v.reshape(1, -1).astype(F32)
    ffn_w_in_bf = ffn_w_in.astype(BF16)
    ffn_w_out_bf = ffn_w_out.astype(BF16)

    lambda_init = 0.8 - 0.6 * math.exp(-0.3 * 0)
    cosd, sind = _rope_tables(S, DIFF_HALF, LANES // DIFF_HALF)
    cosm, sinm = _rope_tables(S, HEAD_DIM, LANES // HEAD_DIM)
    gains = jnp.stack([
        jnp.tile(diff_q_norm_g[0], SECTION // DIFF_HALF),
        jnp.tile(diff_k_norm_g[0], SECTION // DIFF_HALF),
        jnp.tile(moba_q_norm_g[0], SECTION // HEAD_DIM),
        jnp.tile(moba_k_norm_g[0], SECTION // HEAD_DIM)]).astype(F32)
    qk, vt, bias = _even_proj(x, row(attn_norm_g[0]), even_w_in[0].astype(BF16),
                              _block_diag_ones(DIFF_HALF), _block_diag_ones(HEAD_DIM),
                              _rotate_half_matrix(DIFF_HALF), _rotate_half_matrix(HEAD_DIM), gains,
                              cosd, sind, cosm, sinm)
    key_block = jnp.arange(S)[:, None] // MOBA_BLOCK
    ind = (key_block == (jnp.arange(LANES)[None, :] % GATE_SEG)).astype(BF16)
    lam_params = jnp.stack([diff_lambda_q1[0], diff_lambda_k1[0],
                            diff_lambda_q2[0], diff_lambda_k2[0]]).astype(F32)
    subln_tile = jnp.tile(diff_subln_g[0], LANES // HEAD_DIM).reshape(1, LANES).astype(F32)
    def score_bound(d, gq, gk):
        return (d * d ** -0.5 * LOG2E * (1.0 + 2.0 ** -7)
                * jnp.max(jnp.abs(gq.astype(F32))) * jnp.max(jnp.abs(gk.astype(F32))))

    bounded = jnp.logical_and(score_bound(DIFF_HALF, diff_q_norm_g[0], diff_k_norm_g[0]) <= SCORE_BOUND,
                              score_bound(HEAD_DIM, moba_q_norm_g[0], moba_k_norm_g[0]) <= SCORE_BOUND)
    attend = lambda flag: (lambda *a: _attention(*a, lambda_init, flag))
    diff, moba = lax.cond(bounded, attend(True), attend(False), qk, vt, bias, ind, lam_params, subln_tile)
    x = _out_ffn(diff, 0, moba, 0, x, even_w_out[0].astype(BF16), row(ffn_norm_g[0]),
                 ffn_w_in_bf, ffn_w_out_bf, 0)

    bs_tile = jnp.repeat(gmlp_b_s[0].T, GMLP_WIDTH // GMLP_GROUPS, axis=1).astype(F32)
    mix = _odd_mix(x, row(attn_norm_g[1]), odd_w_in[0].astype(BF16), row(odd_b_in[0]),
                   row(gmlp_ln_g[0]), row(gmlp_ln_b[0]), gmlp_w_s[0].astype(F32), bs_tile,
                   conv_w[0].astype(F32), row(conv_b[0]), row(conv_ln_g[0]), row(conv_ln_b[0]))
    x = _out_ffn(mix, 0, mix, 1, x, odd_w_out[0].astype(BF16), row(ffn_norm_g[1]),
                 ffn_w_in_bf, ffn_w_out_bf, 1)
    return x
```

```python
import functools
import math

import jax
import jax.numpy as jnp
from jax import lax
from jax.experimental import pallas as pl
from jax.experimental.pallas import tpu as pltpu

F32 = jnp.float32
BF16 = jnp.bfloat16

LANES = 128
VMEM_LIMIT_BYTES = 56 * 1024 * 1024

HEAD_DIM = 64
DIFF_HALF = 32
N_HEADS_DIFF = 8
N_HEADS_MOBA = 8
MOBA_BLOCK = 256
MOBA_TOPK = 3
GMLP_CHUNK = 128
GMLP_GROUPS = 8
GMLP_WIDTH = 512
CONV_CH = 512
CONV_KERNEL = 31
ROPE_THETA = 10000.0
EPS = 1e-6

SECTION = 512
BD = 256
EVEN_PROJ_ROWS = 1024
ATTN_STREAMS = 2
MOBA_STREAMS_BOUNDED = 4
SCORE_BOUND = 40.0
ONES_ROWS = 16
GATE_SEG = LANES // N_HEADS_MOBA
CONV_HALO = 32
MASKED_BIAS = -1e30
M_INIT = -1e29
LOG2E = math.log2(math.e)

_NT = (((1,), (1,)), ((), ()))


def _rms(x, g):
    return x * lax.rsqrt(jnp.mean(x * x, axis=-1, keepdims=True) + EPS) * g


def _layer_norm(x, g, b):
    mu = jnp.mean(x, axis=-1, keepdims=True)
    xc = x - mu
    return xc * lax.rsqrt(jnp.mean(xc * xc, axis=-1, keepdims=True) + EPS) * g + b


def _even_proj_kernel(x_ref, g_ref, w_ref, bdd_ref, bdm_ref, pd_ref, pm_ref, gain_ref,
                      cosd_ref, sind_ref, cosm_ref, sinm_ref,
                      qk_ref, vt_ref, bias_ref, kbt_ref, *, tm):
    blocks = tm // MOBA_BLOCK
    first_blk = pl.program_id(1) * blocks
    h = _rms(x_ref[0], g_ref[...]).astype(BF16)

    def tiles_dot(a, m_ref):
        return jnp.concatenate(
            [jnp.dot(a[:, c * BD:(c + 1) * BD], m_ref[...], preferred_element_type=F32)
             for c in range(SECTION // BD)], axis=1)

    def qk_section(sec, bd_ref, perm_ref, seg, cos_ref, sin_ref):
        y = jnp.dot(h, w_ref[:, sec * SECTION:(sec + 1) * SECTION], preferred_element_type=F32)
        ss = tiles_dot((y * y).astype(BF16), bd_ref)
        yn = y * lax.rsqrt(ss * (1.0 / seg) + EPS) * gain_ref[sec_row[sec]:sec_row[sec] + 1, :]
        rot = tiles_dot(yn.astype(BF16), perm_ref)
        cos = cos_ref[...]
        sin = sin_ref[...]
        return [yn[:, c * LANES:(c + 1) * LANES] * cos + rot[:, c * LANES:(c + 1) * LANES] * sin
                for c in range(SECTION // LANES)]

    def store(out_sec, chunks, scale):
        for c, yc in enumerate(chunks):
            lo = out_sec * SECTION + c * LANES
            qk_ref[0, :, lo:lo + LANES] = (yc * scale).astype(BF16)

    sec_row = {0: 0, 1: 1, 3: 2, 4: 3}
    store(0, qk_section(0, bdd_ref, pd_ref, DIFF_HALF, cosd_ref, sind_ref), DIFF_HALF ** -0.5 * LOG2E)
    store(1, qk_section(1, bdd_ref, pd_ref, DIFF_HALF, cosd_ref, sind_ref), 1.0)
    mq = qk_section(3, bdm_ref, pm_ref, HEAD_DIM, cosm_ref, sinm_ref)
    store(2, mq, HEAD_DIM ** -0.5 * LOG2E)
    mk = qk_section(4, bdm_ref, pm_ref, HEAD_DIM, cosm_ref, sinm_ref)
    store(3, mk, 1.0)
    for out_sec, sec in enumerate((2, 5)):
        lo = sec * SECTION
        vt = jnp.dot(h, w_ref[:, lo:lo + SECTION], preferred_element_type=F32).T.astype(BF16)
        for sb in range(blocks):
            vt_ref[0, sb, out_sec * SECTION:(out_sec + 1) * SECTION, :] = (
                vt[:, sb * MOBA_BLOCK:(sb + 1) * MOBA_BLOCK])

    @pl.when(first_blk == 0)
    def _():
        kbt_ref[...] = jnp.zeros_like(kbt_ref)

    mk_full = jnp.concatenate(mk, axis=1)
    head_of_lane = lax.broadcasted_iota(jnp.int32, (1, SECTION), 1) // HEAD_DIM
    for sb in range(blocks):
        kbar = jnp.mean(mk_full[sb * MOBA_BLOCK:(sb + 1) * MOBA_BLOCK], axis=0, keepdims=True)
        for hh in range(N_HEADS_MOBA):
            kbt_ref[pl.ds(hh * GATE_SEG + first_blk + sb, 1), :] = jnp.where(head_of_lane == hh, kbar, 0.0)

    def split(a):
        hi = a.astype(BF16)
        return hi, (a - hi.astype(F32)).astype(BF16)

    q_hi, q_lo = split(jnp.concatenate(mq, axis=1))
    k_hi, k_lo = split(kbt_ref[...])
    gate = lax.dot_general(jnp.concatenate([k_hi, k_hi, k_lo], axis=1),
                           jnp.concatenate([q_hi, q_lo, q_hi], axis=1), _NT,
                           preferred_element_type=F32)
    blk = lax.broadcasted_iota(jnp.int32, (GATE_SEG, tm), 0)
    blk_f = blk.astype(F32)
    neg_inf = jnp.float32(-jnp.inf)
    own = first_blk + lax.broadcasted_iota(jnp.int32, (GATE_SEG, tm), 1) // MOBA_BLOCK
    bias_rows = []
    for hh in range(N_HEADS_MOBA):
        avail = jnp.where(blk < own, gate[hh * GATE_SEG:(hh + 1) * GATE_SEG], neg_inf)
        chosen = blk == own
        for _ in range(MOBA_TOPK):
            best = jnp.max(avail, axis=0, keepdims=True)
            cand = jnp.logical_and(avail == best, avail > neg_inf)
            first = jnp.min(jnp.where(cand, blk_f, float(GATE_SEG)), axis=0, keepdims=True)
            pick = blk_f == first
            chosen = jnp.logical_or(chosen, pick)
            avail = jnp.where(pick, neg_inf, avail)
        bias_rows.append(jnp.where(chosen, 0.0, MASKED_BIAS))
    bias_ref[0] = jnp.concatenate(bias_rows, axis=0).T.astype(BF16)


def _even_proj(x, norm_g, w_in, bdd, bdm, pd, pm, gains, cosd, sind, cosm, sinm):
    B, S, D = x.shape
    tm = EVEN_PROJ_ROWS
    blocks = tm // MOBA_BLOCK
    n_out = w_in.shape[1]
    n_qk = 4 * SECTION
    n_v = 2 * SECTION
    const = lambda b, t: (0, 0)
    tab = pl.BlockSpec((tm, LANES), lambda b, t: (t, 0))
    return pl.pallas_call(
        functools.partial(_even_proj_kernel, tm=tm),
        grid=(B, S // tm),
        in_specs=[
            pl.BlockSpec((1, tm, D), lambda b, t: (b, t, 0)),
            pl.BlockSpec((1, D), const),
            pl.BlockSpec((D, n_out), const, pipeline_mode=pl.Buffered(1)),
            pl.BlockSpec((BD, BD), const),
            pl.BlockSpec((BD, BD), const),
            pl.BlockSpec((BD, BD), const),
            pl.BlockSpec((BD, BD), const),
            pl.BlockSpec((4, SECTION), const),
            tab, tab, tab, tab,
        ],
        out_specs=[
            pl.BlockSpec((1, tm, n_qk), lambda b, t: (b, t, 0)),
            pl.BlockSpec((1, blocks, n_v, MOBA_BLOCK), lambda b, t: (b, t, 0, 0)),
            pl.BlockSpec((1, tm, LANES), lambda b, t: (b, t, 0)),
        ],
        out_shape=[
            jax.ShapeDtypeStruct((B, S, n_qk), BF16),
            jax.ShapeDtypeStruct((B, S // MOBA_BLOCK, n_v, MOBA_BLOCK), BF16),
            jax.ShapeDtypeStruct((B, S, LANES), BF16),
        ],
        scratch_shapes=[pltpu.VMEM((LANES, SECTION), F32)],
        compiler_params=pltpu.CompilerParams(
            dimension_semantics=("arbitrary", "arbitrary"),
            vmem_limit_bytes=VMEM_LIMIT_BYTES),
        name="even_proj",
    )(x, norm_g, w_in, bdd, bdm, pd, pm, gains, cosd, sind, cosm, sinm)


def _flash_head_pair(make_q, n, v_rows, k_ref, vt_ref, s_ref, p_ref, nq, t, emit, k_aug_ref=None):
    streams = k_ref.shape[0]
    per_stream = n // streams

    def scores(q_all, j):
        start = pl.multiple_of(j * t, t)
        out = []
        for g in range(streams):
            k = k_ref[g, pl.ds(start, t), :]
            if k_aug_ref is not None:
                k = jnp.concatenate([k, k_aug_ref[pl.ds(start, t), :]], axis=1)
            out.append(lax.dot_general(k, q_all[g], _NT, preferred_element_type=F32))
        return jnp.concatenate(out, axis=1)

    def softmax(s_all, stats, diagonal):
        ps, new_stats, alphas = [], [], []
        for i in range(n):
            m, l = stats[i]
            s = s_all[:, i * t:(i + 1) * t]
            if diagonal:
                key = lax.broadcasted_iota(jnp.int32, (t, t), 0)
                qry = lax.broadcasted_iota(jnp.int32, (t, t), 1)
                s = jnp.where(key <= qry, s, -jnp.inf)
            m_new = jnp.maximum(m, jnp.max(s, axis=0, keepdims=True))
            p = jnp.exp2(s - m_new)
            alpha = jnp.exp2(m - m_new)
            ps.append(p.astype(BF16))
            new_stats.append((m_new, alpha * l + jnp.sum(p, axis=0, keepdims=True)))
            alphas.append(alpha)
        return tuple(ps), tuple(new_stats), alphas

    def weighted_values(j, ps):
        out = []
        for g in range(streams):
            vt = vt_ref[g, j]
            for li, (lo, hi) in enumerate(v_rows):
                out.append(jnp.dot(vt[lo:hi], ps[g * per_stream + li], preferred_element_type=F32))
        return out

    def step(q_all, j, cur, nxt, carry):
        stats, accs = carry
        s_ref[nxt] = scores(q_all, j + 1)
        p_prev = tuple(p_ref[nxt, i] for i in range(n))
        pv = weighted_values(jnp.maximum(j - 1, 0), p_prev)
        pv = [jnp.where(j > 0, x, 0.0) for x in pv]
        p_cur, stats, alphas = softmax(s_ref[cur], stats, False)
        for i in range(n):
            p_ref[cur, i] = p_cur[i]
        accs = tuple(a * (acc + x) for a, acc, x in zip(alphas, accs, pv))
        return stats, accs

    def query_tile(qi, first, second, odd, next_first):
        q_all = make_q(qi)
        init = (tuple((jnp.full((1, t), M_INIT, F32), jnp.zeros((1, t), F32)) for _ in range(n)),
                tuple(jnp.zeros((hi - lo, t), F32) for _ in range(streams) for lo, hi in v_rows))
        carry = lax.fori_loop(
            0, qi // 2,
            lambda i, c: step(q_all, 2 * i + 1, second, first, step(q_all, 2 * i, first, second, c)), init)
        if odd:
            carry = step(q_all, qi - 1, first, second, carry)
        diag, prev = (second, first) if odd else (first, second)
        stats, accs = carry
        s_ref[next_first] = scores(make_q(jnp.minimum(qi + 1, nq - 1)), 0)
        pv_prev = weighted_values(jnp.maximum(qi - 1, 0), tuple(p_ref[prev, i] for i in range(n)))
        pv_prev = [jnp.where(qi > 0, x, 0.0) for x in pv_prev]
        p_cur, stats, alphas = softmax(s_ref[diag], stats, True)
        pv_cur = weighted_values(qi, p_cur)
        emit(qi, [(a * (acc + x) + y, l)
                  for a, acc, x, y, (_, l) in zip(alphas, accs, pv_prev, pv_cur, stats)])

    p_ref[...] = jnp.zeros(p_ref.shape, BF16)
    s_ref[0] = scores(make_q(0), 0)

    def pair_of_tiles(u, _):
        query_tile(2 * u, 0, 1, False, 2)
        query_tile(2 * u + 1, 2, 1, True, 0)
        return 0

    lax.fori_loop(0, nq // 2, pair_of_tiles, 0)


def _flash_bounded(make_q, n, v_rows, k_ref, vt_ref, s_ref, nq, t, emit, k_aug_ref=None):
    streams = k_ref.shape[0]
    per_stream = n // streams
    ones = jnp.ones((ONES_ROWS, t), BF16)
    width = per_stream * t
    key = lax.broadcasted_iota(jnp.int32, (t, width), 0)
    qry = lax.broadcasted_iota(jnp.int32, (t, width), 1) & (t - 1)

    def scores(q_all, j, slot):
        start = pl.multiple_of(j * t, t)
        for g in range(streams):
            k = k_ref[g, pl.ds(start, t), :]
            if k_aug_ref is not None:
                k = jnp.concatenate([k, k_aug_ref[pl.ds(start, t), :]], axis=1)
            s_ref[slot, g] = lax.dot_general(k, q_all[g], _NT, preferred_element_type=F32)

    def accumulate(j, slot, accs, diagonal):
        out = []
        for g in range(streams):
            p = jnp.exp2(s_ref[slot, g])
            if diagonal:
                p = jnp.where(key <= qry, p, 0.0)
            p = p.astype(BF16)
            vt = vt_ref[g, j]
            for li, (lo, hi) in enumerate(v_rows):
                lhs = jnp.concatenate([vt[lo:hi], ones], axis=0)
                out.append(accs[g * per_stream + li]
                           + jnp.dot(lhs, p[:, li * t:(li + 1) * t], preferred_element_type=F32))
        return tuple(out)

    def step(q_all, j, cur, nxt, accs):
        scores(q_all, j + 1, nxt)
        return accumulate(j, cur, accs, False)

    rows = [hi - lo for _ in range(streams) for lo, hi in v_rows]

    def query_tile(u, first, second, odd, next_first):
        qi = 2 * u + odd
        q_all = make_q(qi)

        def four_steps(i, a):
            j = 4 * i
            a = step(q_all, j, first, second, a)
            a = step(q_all, j + 1, second, first, a)
            a = step(q_all, j + 2, first, second, a)
            return step(q_all, j + 3, second, first, a)

        init = tuple(jnp.zeros((r + ONES_ROWS, t), F32) for r in rows)
        accs = lax.fori_loop(0, qi // 4, four_steps, init)

        def tail(accs, two_more):
            j = 4 * (qi // 4)
            if two_more:
                accs = step(q_all, j, first, second, accs)
                accs = step(q_all, j + 1, second, first, accs)
                j = j + 2
            if odd:
                accs = step(q_all, j, first, second, accs)
            scores(make_q(jnp.minimum(qi + 1, nq - 1)), 0, next_first)
            accs = accumulate(qi, second if odd else first, accs, True)
            emit(qi, [(a[:r], a[r:r + 1]) for a, r in zip(accs, rows)])

        pl.when((u & 1) == 0)(lambda: tail(accs, False))
        pl.when((u & 1) == 1)(lambda: tail(accs, True))

    scores(make_q(0), 0, 0)

    def pair_of_tiles(u, _):
        query_tile(u, 0, 1, 0, 2)
        query_tile(u, 2, 1, 1, 0)
        return 0

    lax.fori_loop(0, nq // 2, pair_of_tiles, 0)


def _diff_attn_kernel(lam_ref, q_ref, k_ref, vt_ref, sg_ref, o_ref, *scratch, t, lambda_init, bounded):
    nq = q_ref.shape[1] // t
    seg = lax.broadcasted_iota(jnp.int32, (t, LANES), 1) // DIFF_HALF
    lam_p = lam_ref[...]
    lam = (jnp.exp(jnp.sum(lam_p[0:1] * lam_p[1:2], axis=1, keepdims=True))
           - jnp.exp(jnp.sum(lam_p[2:3] * lam_p[3:4], axis=1, keepdims=True)) + lambda_init)

    streams = q_ref.shape[0]

    def make_q(qi):
        out = []
        for g in range(streams):
            q = q_ref[g, pl.ds(pl.multiple_of(qi * t, t), t), :]
            out.append(jnp.concatenate([jnp.where(seg == i, q, jnp.zeros_like(q)) for i in range(4)], axis=0))
        return out

    def emit(qi, res):
        a = [acc * (1.0 / l) for acc, l in res]
        for g in range(streams):
            heads = []
            for hh in range(2):
                o = a[4 * g + 2 * hh] - lam * a[4 * g + 2 * hh + 1]
                heads.append(o * lax.rsqrt(jnp.mean(o * o, axis=0, keepdims=True) + EPS))
            o = jnp.concatenate(heads, axis=0).T
            o_ref[g, pl.ds(pl.multiple_of(qi * t, t), t), :] = (
                o * sg_ref[...] * (1.0 - lambda_init)).astype(BF16)

    v_rows = [(HEAD_DIM * (i // 2), HEAD_DIM * (i // 2 + 1)) for i in range(4)]
    if bounded:
        _flash_bounded(make_q, 4 * streams, v_rows, k_ref, vt_ref, *scratch, nq, t, emit)
    else:
        _flash_head_pair(make_q, 4 * streams, v_rows, k_ref, vt_ref, *scratch, nq, t, emit)


def _moba_attn_kernel(q_ref, bias_ref, k_ref, vt_ref, ind_ref, o_ref, *scratch, t, bounded):
    hp = pl.program_id(1)
    nq = q_ref.shape[1] // t
    lane = lax.broadcasted_iota(jnp.int32, (t, LANES), 1)

    streams = q_ref.shape[0]

    def make_q(qi):
        rows = pl.ds(pl.multiple_of(qi * t, t), t)
        out = []
        for g in range(streams):
            q = q_ref[g, rows, :]
            bias = bias_ref[g, rows, :]
            strips = []
            for hh in range(2):
                qz = jnp.where(lane // HEAD_DIM == hh, q, jnp.zeros_like(q))
                bz = jnp.where(lane // GATE_SEG == 2 * hp + hh, bias, jnp.zeros_like(bias))
                strips.append(jnp.concatenate([qz, bz], axis=1))
            out.append(jnp.concatenate(strips, axis=0))
        return out

    def emit(qi, res):
        for g in range(streams):
            o = jnp.concatenate([acc * (1.0 / l) for acc, l in res[2 * g:2 * g + 2]], axis=0)
            o_ref[g, pl.ds(pl.multiple_of(qi * t, t), t), :] = o.T.astype(BF16)

    v_rows = [(0, HEAD_DIM), (HEAD_DIM, 2 * HEAD_DIM)]
    if bounded:
        _flash_bounded(make_q, 2 * streams, v_rows, k_ref, vt_ref, *scratch, nq, t, emit, k_aug_ref=ind_ref)
    else:
        _flash_head_pair(make_q, 2 * streams, v_rows, k_ref, vt_ref, *scratch, nq, t, emit, k_aug_ref=ind_ref)


def _attention(qk, vt, bias, ind, lam_params, subln_tile, lambda_init, bounded):
    B, S, _ = qk.shape
    t = MOBA_BLOCK
    n_pairs = SECTION // LANES
    assert (S // t) % 2 == 0
    params = pltpu.CompilerParams(
        dimension_semantics=("arbitrary", "arbitrary"),
        vmem_limit_bytes=VMEM_LIMIT_BYTES)
    out_shape = jax.ShapeDtypeStruct((B, S, SECTION), BF16)

    def streams(wanted):
        return wanted if B % wanted == 0 else 1

    def seq_spec(g, col):
        return pl.BlockSpec((g, S, LANES), lambda b, p: (b, 0, col(p)))

    def vt_spec(g, sec):
        return pl.BlockSpec((g, S // t, LANES, t), lambda b, p: (b, 0, sec * n_pairs + p, 0))

    def pipeline_scratch(g, n):
        if bounded:
            return [pltpu.VMEM((3, g, t, n * t), F32)]
        return [pltpu.VMEM((3, t, g * n * t), F32), pltpu.VMEM((3, g * n, t, t), BF16)]

    g = streams(ATTN_STREAMS)
    diff = pl.pallas_call(
        functools.partial(_diff_attn_kernel, t=t, lambda_init=lambda_init, bounded=bounded),
        grid=(B // g, n_pairs),
        in_specs=[pl.BlockSpec((4, DIFF_HALF), lambda b, p: (0, 0)),
                  seq_spec(g, lambda p: p), seq_spec(g, lambda p: n_pairs + p), vt_spec(g, 0),
                  pl.BlockSpec((1, LANES), lambda b, p: (0, 0))],
        out_specs=seq_spec(g, lambda p: p), out_shape=out_shape, compiler_params=params,
        scratch_shapes=pipeline_scratch(g, 4),
        name="diff_attn",
    )(lam_params, qk, qk, vt, subln_tile)

    g = streams(MOBA_STREAMS_BOUNDED if bounded else ATTN_STREAMS)
    moba = pl.pallas_call(
        functools.partial(_moba_attn_kernel, t=t, bounded=bounded),
        grid=(B // g, n_pairs),
        in_specs=[seq_spec(g, lambda p: 2 * n_pairs + p),
                  seq_spec(g, lambda p: 0),
                  seq_spec(g, lambda p: 3 * n_pairs + p), vt_spec(g, 1),
                  pl.BlockSpec((S, LANES), lambda b, p: (0, 0))],
        out_specs=seq_spec(g, lambda p: p), out_shape=out_shape, compiler_params=params,
        scratch_shapes=pipeline_scratch(g, 2),
        name="moba_attn",
    )(qk, bias, qk, vt, ind)
    return diff, moba


def _out_ffn_kernel(mixa_ref, mixb_ref, x_ref, wo_ref, g_ref, wi_ref, w2_ref, o_ref, *, d_ff, chunk):
    mix = jnp.concatenate([mixa_ref[0], mixb_ref[0]], axis=1)
    x1 = x_ref[0] + jnp.dot(mix, wo_ref[...].astype(BF16), preferred_element_type=F32)
    hn = _rms(x1, g_ref[...]).astype(BF16)
    acc = x1
    for c in range(d_ff // chunk):
        lo = c * chunk
        g = jnp.dot(hn, wi_ref[0, :, lo:lo + chunk].astype(BF16), preferred_element_type=F32)
        u = jnp.dot(hn, wi_ref[0, :, d_ff + lo:d_ff + lo + chunk].astype(BF16), preferred_element_type=F32)
        a = (g * jax.nn.sigmoid(g) * u).astype(BF16)
        acc = acc + jnp.dot(a, w2_ref[0, lo:lo + chunk, :].astype(BF16), preferred_element_type=F32)
    o_ref[0] = acc


def _out_ffn(mix_a, col_a, mix_b, col_b, x, w_out, norm_g, w_in, w2, layer):
    B, S, D = x.shape
    d_ff = w2.shape[1]
    tm = 512
    half = w_out.shape[0] // 2
    const = lambda b, t: (0, 0)
    resident = lambda shape: pl.BlockSpec(shape, const, pipeline_mode=pl.Buffered(1))
    slab = lambda a: pl.BlockSpec((1,) + a.shape[1:], lambda b, t: (layer, 0, 0), pipeline_mode=pl.Buffered(1))
    tile = pl.BlockSpec((1, tm, D), lambda b, t: (b, t, 0))
    mix_spec = lambda col: pl.BlockSpec((1, tm, half), lambda b, t: (b, t, col))
    return pl.pallas_call(
        functools.partial(_out_ffn_kernel, d_ff=d_ff, chunk=256),
        grid=(B, S // tm),
        in_specs=[mix_spec(col_a), mix_spec(col_b), tile, resident(w_out.shape), pl.BlockSpec((1, D), const),
                  slab(w_in), slab(w2)],
        out_specs=tile,
        out_shape=jax.ShapeDtypeStruct((B, S, D), F32),
        compiler_params=pltpu.CompilerParams(
            dimension_semantics=("arbitrary", "arbitrary"),
            vmem_limit_bytes=VMEM_LIMIT_BYTES),
        name="out_ffn",
    )(mix_a, mix_b, x, w_out, norm_g, w_in, w2)


def _odd_mix_kernel(x_ref, g_ref, w_ref, b_ref, lng_ref, lnb_ref, ws_ref, bs_ref,
                    cw_ref, cb_ref, clg_ref, clb_ref, o_ref, ext_ref, *, tm):
    ti = pl.program_id(1)
    h = _rms(x_ref[0], g_ref[...]).astype(BF16)

    def proj(lo, width):
        return jnp.dot(h, w_ref[:, lo:lo + width], preferred_element_type=F32) + b_ref[:, lo:lo + width]

    gu = jax.nn.gelu(proj(0, GMLP_WIDTH), approximate=True)
    gv = jax.nn.gelu(proj(GMLP_WIDTH, GMLP_WIDTH), approximate=True)
    gv = _layer_norm(gv, lng_ref[...], lnb_ref[...]).astype(BF16)
    T = GMLP_CHUNK
    tri = lax.broadcasted_iota(jnp.int32, (T, T), 1) <= lax.broadcasted_iota(jnp.int32, (T, T), 0)
    lane = lax.broadcasted_iota(jnp.int32, (T, LANES), 1)
    first = lane < (GMLP_WIDTH // GMLP_GROUPS)
    zero = jnp.zeros((T, LANES), BF16)
    for lb in range(GMLP_WIDTH // LANES):
        wpair = jnp.concatenate(
            [jnp.where(tri, ws_ref[2 * lb + i], 0.0).astype(BF16) for i in range(2)], axis=1)
        bs = bs_ref[:, lb * LANES:(lb + 1) * LANES]
        for c in range(tm // T):
            vc = gv[c * T:(c + 1) * T, lb * LANES:(lb + 1) * LANES]
            vpair = jnp.concatenate([jnp.where(first, vc, zero), jnp.where(first, zero, vc)], axis=0)
            sg = jnp.dot(wpair, vpair, preferred_element_type=F32) + bs
            o_ref[0, c * T:(c + 1) * T, lb * LANES:(lb + 1) * LANES] = (
                gu[c * T:(c + 1) * T, lb * LANES:(lb + 1) * LANES] * sg).astype(BF16)

    lo = 2 * GMLP_WIDTH
    cval = proj(lo, CONV_CH) * jax.nn.sigmoid(proj(lo + CONV_CH, CONV_CH))

    @pl.when(ti == 0)
    def _():
        ext_ref[0:CONV_HALO, :] = jnp.zeros((CONV_HALO, CONV_CH), F32)

    ext_ref[CONV_HALO:CONV_HALO + tm, :] = cval
    first_tap = CONV_HALO - (CONV_KERNEL - 1)
    ext = ext_ref[...]
    rows_total = ext.shape[0]
    conv = jnp.zeros((tm, CONV_CH), F32) + cb_ref[...]
    for r in range(8):
        rolled = ext if r == 0 else pltpu.roll(ext, rows_total - r, 0)
        for k in range(CONV_KERNEL):
            if (first_tap + k) % 8 == r:
                off = first_tap + k - r
                conv = conv + rolled[off:off + tm, :] * cw_ref[k:k + 1, :]
    ext_ref[0:CONV_HALO, :] = ext_ref[tm:tm + CONV_HALO, :]
    dn = _layer_norm(conv, clg_ref[...], clb_ref[...])
    o_ref[0, :, GMLP_WIDTH:GMLP_WIDTH + CONV_CH] = (dn * jax.nn.sigmoid(dn)).astype(BF16)


def _odd_mix(x, norm_g, w_in, b_in, ln_g, ln_b, w_s, bs_tile, conv_w, conv_b, cln_g, cln_b):
    B, S, D = x.shape
    tm = 512
    const2 = lambda b, t: (0, 0)
    full = lambda a: pl.BlockSpec(a.shape, (lambda b, t: (0,) * a.ndim))
    n_out = GMLP_WIDTH + CONV_CH
    span = tm + 8 * ((CONV_KERNEL - 1) // 8)
    args = (x, norm_g, w_in, b_in, ln_g, ln_b, w_s, bs_tile, conv_w, conv_b, cln_g, cln_b)
    return pl.pallas_call(
        functools.partial(_odd_mix_kernel, tm=tm),
        grid=(B, S // tm),
        in_specs=[pl.BlockSpec((1, tm, D), lambda b, t: (b, t, 0))] + [full(a) for a in args[1:]],
        out_specs=pl.BlockSpec((1, tm, n_out), lambda b, t: (b, t, 0)),
        out_shape=jax.ShapeDtypeStruct((B, S, n_out), BF16),
        scratch_shapes=[pltpu.VMEM((span + 8, CONV_CH), F32)],
        compiler_params=pltpu.CompilerParams(
            dimension_semantics=("arbitrary", "arbitrary"),
            vmem_limit_bytes=VMEM_LIMIT_BYTES),
        name="odd_mix",
    )(*args)


def _rope_tables(seq, dim, seg_per_block):
    inv = 1.0 / (ROPE_THETA ** (jnp.arange(0, dim, 2, dtype=F32) / dim))
    ang = jnp.arange(seq, dtype=F32)[:, None] * inv[None, :]
    cos, sin = jnp.cos(ang), jnp.sin(ang)
    cos = jnp.tile(jnp.concatenate([cos, cos], axis=1), (1, seg_per_block))
    sin = jnp.tile(jnp.concatenate([-sin, sin], axis=1), (1, seg_per_block))
    return cos, sin


def _block_diag_ones(seg):
    idx = jnp.arange(BD) // seg
    return (idx[:, None] == idx[None, :]).astype(BF16)


def _rotate_half_matrix(seg):
    dst = jnp.arange(BD)
    src = jnp.where(dst % seg < seg // 2, dst + seg // 2, dst - seg // 2)
    return (jnp.arange(BD)[:, None] == src[None, :]).astype(BF16)


def kernel(x, attn_norm_g, ffn_norm_g, ffn_w_in, ffn_w_out, even_w_in, even_w_out,
           diff_q_norm_g, diff_k_norm_g, diff_lambda_q1, diff_lambda_k1,
           diff_lambda_q2, diff_lambda_k2, diff_subln_g, moba_q_norm_g, moba_k_norm_g,
           odd_w_in, odd_b_in, odd_w_out, gmlp_ln_g, gmlp_ln_b, gmlp_w_s, gmlp_b_s,
           conv_w, conv_b, conv_ln_g, conv_ln_b):
    B, S, D = x.shape
    assert S % EVEN_PROJ_ROWS == 0 and S // MOBA_BLOCK <= GATE_SEG
    row = lambda v: v.reshape(1, -1).astype(F32)

    lambda_init = 0.8 - 0.6 * math.exp(-0.3 * 0)
    cosd, sind = _rope_tables(S, DIFF_HALF, LANES // DIFF_HALF)
    cosm, sinm = _rope_tables(S, HEAD_DIM, LANES // HEAD_DIM)
    gains = jnp.stack([
        jnp.tile(diff_q_norm_g[0], SECTION // DIFF_HALF),
        jnp.tile(diff_k_norm_g[0], SECTION // DIFF_HALF),
        jnp.tile(moba_q_norm_g[0], SECTION // HEAD_DIM),
        jnp.tile(moba_k_norm_g[0], SECTION // HEAD_DIM)]).astype(F32)
    qk, vt, bias = _even_proj(x, row(attn_norm_g[0]), even_w_in[0].astype(BF16),
                              _block_diag_ones(DIFF_HALF), _block_diag_ones(HEAD_DIM),
                              _rotate_half_matrix(DIFF_HALF), _rotate_half_matrix(HEAD_DIM), gains,
                              cosd, sind, cosm, sinm)
    key_block = jnp.arange(S)[:, None] // MOBA_BLOCK
    ind = (key_block == (jnp.arange(LANES)[None, :] % GATE_SEG)).astype(BF16)
    lam_params = jnp.stack([diff_lambda_q1[0], diff_lambda_k1[0],
                            diff_lambda_q2[0], diff_lambda_k2[0]]).astype(F32)
    subln_tile = jnp.tile(diff_subln_g[0], LANES // HEAD_DIM).reshape(1, LANES).astype(F32)
    def score_bound(d, gq, gk):
        return (d * d ** -0.5 * LOG2E * (1.0 + 2.0 ** -7)
                * jnp.max(jnp.abs(gq.astype(F32))) * jnp.max(jnp.abs(gk.astype(F32))))

    bounded = jnp.logical_and(score_bound(DIFF_HALF, diff_q_norm_g[0], diff_k_norm_g[0]) <= SCORE_BOUND,
                              score_bound(HEAD_DIM, moba_q_norm_g[0], moba_k_norm_g[0]) <= SCORE_BOUND)
    attend = lambda flag: (lambda *a: _attention(*a, lambda_init, flag))
    diff, moba = lax.cond(bounded, attend(True), attend(False), qk, vt, bias, ind, lam_params, subln_tile)
    x = _out_ffn(diff, 0, moba, 0, x, even_w_out[0], row(ffn_norm_g[0]), ffn_w_in, ffn_w_out, 0)

    bs_tile = jnp.repeat(gmlp_b_s[0].T, GMLP_WIDTH // GMLP_GROUPS, axis=1).astype(F32)
    mix = _odd_mix(x, row(attn_norm_g[1]), odd_w_in[0].astype(BF16), row(odd_b_in[0]),
                   row(gmlp_ln_g[0]), row(gmlp_ln_b[0]), gmlp_w_s[0].astype(F32), bs_tile,
                   conv_w[0].astype(F32), row(conv_b[0]), row(conv_ln_g[0]), row(conv_ln_b[0]))
    x = _out_ffn(mix, 0, mix, 1, x, odd_w_out[0], row(ffn_norm_g[1]), ffn_w_in, ffn_w_out, 1)
    return x
```

```python
import functools
import math

import jax
import jax.numpy as jnp
from jax import lax
from jax.experimental import pallas as pl
from jax.experimental.pallas import tpu as pltpu

F32 = jnp.float32
BF16 = jnp.bfloat16

LANES = 128
VMEM_LIMIT_BYTES = 56 * 1024 * 1024

HEAD_DIM = 64
DIFF_HALF = 32
N_HEADS_DIFF = 8
N_HEADS_MOBA = 8
MOBA_BLOCK = 256
MOBA_TOPK = 3
GMLP_CHUNK = 128
GMLP_GROUPS = 8
GMLP_WIDTH = 512
CONV_CH = 512
CONV_KERNEL = 31
ROPE_THETA = 10000.0
EPS = 1e-6

SECTION = 512
BD = 256
EVEN_PROJ_ROWS = 1024
ATTN_STREAMS = 2
MOBA_STREAMS_BOUNDED = 4
SCORE_BOUND = 40.0
ONES_ROWS = 16
GATE_SEG = LANES // N_HEADS_MOBA
CONV_HALO = 32
MASKED_BIAS = -1e30
M_INIT = -1e29
LOG2E = math.log2(math.e)

_NT = (((1,), (1,)), ((), ()))


def _rms(x, g):
    return x * lax.rsqrt(jnp.mean(x * x, axis=-1, keepdims=True) + EPS) * g


def _layer_norm(x, g, b):
    mu = jnp.mean(x, axis=-1, keepdims=True)
    xc = x - mu
    return xc * lax.rsqrt(jnp.mean(xc * xc, axis=-1, keepdims=True) + EPS) * g + b


def _even_proj_kernel(x_ref, g_ref, w_ref, bdd_ref, bdm_ref, gain_ref,
                      cosd_ref, sind_ref, cosm_ref, sinm_ref,
                      qk_ref, vt_ref, bias_ref, kbt_ref, *, tm):
    blocks = tm // MOBA_BLOCK
    first_blk = pl.program_id(1) * blocks
    h = _rms(x_ref[0], g_ref[...]).astype(BF16)

    def qk_section(sec, bd_ref, seg, cos_ref, sin_ref):
        y = jnp.dot(h, w_ref[:, sec * SECTION:(sec + 1) * SECTION], preferred_element_type=F32)
        y2 = (y * y).astype(BF16)
        ss = jnp.concatenate(
            [jnp.dot(y2[:, c * BD:(c + 1) * BD], bd_ref[...], preferred_element_type=F32)
             for c in range(SECTION // BD)], axis=1)
        yn = y * lax.rsqrt(ss * (1.0 / seg) + EPS) * gain_ref[sec_row[sec]:sec_row[sec] + 1, :]
        half = seg // 2
        lower = (lax.broadcasted_iota(jnp.int32, (tm, LANES), 1) & (seg - 1)) < half
        cos = cos_ref[...]
        sin = sin_ref[...]
        out = []
        for c in range(SECTION // LANES):
            yc = yn[:, c * LANES:(c + 1) * LANES]
            rot = jnp.where(lower, pltpu.roll(yc, LANES - half, 1), pltpu.roll(yc, half, 1))
            out.append(yc * cos + rot * sin)
        return out

    def store(out_sec, chunks, scale):
        for c, yc in enumerate(chunks):
            lo = out_sec * SECTION + c * LANES
            qk_ref[0, :, lo:lo + LANES] = (yc * scale).astype(BF16)

    sec_row = {0: 0, 1: 1, 3: 2, 4: 3}
    store(0, qk_section(0, bdd_ref, DIFF_HALF, cosd_ref, sind_ref), DIFF_HALF ** -0.5 * LOG2E)
    store(1, qk_section(1, bdd_ref, DIFF_HALF, cosd_ref, sind_ref), 1.0)
    mq = qk_section(3, bdm_ref, HEAD_DIM, cosm_ref, sinm_ref)
    store(2, mq, HEAD_DIM ** -0.5 * LOG2E)
    mk = qk_section(4, bdm_ref, HEAD_DIM, cosm_ref, sinm_ref)
    store(3, mk, 1.0)
    for out_sec, sec in enumerate((2, 5)):
        lo = sec * SECTION
        vt = jnp.dot(h, w_ref[:, lo:lo + SECTION], preferred_element_type=F32).T.astype(BF16)
        for sb in range(blocks):
            vt_ref[0, sb, out_sec * SECTION:(out_sec + 1) * SECTION, :] = (
                vt[:, sb * MOBA_BLOCK:(sb + 1) * MOBA_BLOCK])

    @pl.when(first_blk == 0)
    def _():
        kbt_ref[...] = jnp.zeros_like(kbt_ref)

    mk_full = jnp.concatenate(mk, axis=1)
    head_of_lane = lax.broadcasted_iota(jnp.int32, (1, SECTION), 1) // HEAD_DIM
    for sb in range(blocks):
        kbar = jnp.mean(mk_full[sb * MOBA_BLOCK:(sb + 1) * MOBA_BLOCK], axis=0, keepdims=True)
        for hh in range(N_HEADS_MOBA):
            kbt_ref[pl.ds(hh * GATE_SEG + first_blk + sb, 1), :] = jnp.where(head_of_lane == hh, kbar, 0.0)

    def split(a):
        hi = a.astype(BF16)
        return hi, (a - hi.astype(F32)).astype(BF16)

    q_hi, q_lo = split(jnp.concatenate(mq, axis=1))
    k_hi, k_lo = split(kbt_ref[...])
    gate = lax.dot_general(jnp.concatenate([k_hi, k_hi, k_lo], axis=1),
                           jnp.concatenate([q_hi, q_lo, q_hi], axis=1), _NT,
                           preferred_element_type=F32)
    blk = lax.broadcasted_iota(jnp.int32, (GATE_SEG, tm), 0)
    blk_f = blk.astype(F32)
    neg_inf = jnp.float32(-jnp.inf)
    own = first_blk + lax.broadcasted_iota(jnp.int32, (GATE_SEG, tm), 1) // MOBA_BLOCK
    bias_rows = []
    for hh in range(N_HEADS_MOBA):
        avail = jnp.where(blk < own, gate[hh * GATE_SEG:(hh + 1) * GATE_SEG], neg_inf)
        chosen = blk == own
        for _ in range(MOBA_TOPK):
            best = jnp.max(avail, axis=0, keepdims=True)
            cand = jnp.logical_and(avail == best, avail > neg_inf)
            first = jnp.min(jnp.where(cand, blk_f, float(GATE_SEG)), axis=0, keepdims=True)
            pick = blk_f == first
            chosen = jnp.logical_or(chosen, pick)
            avail = jnp.where(pick, neg_inf, avail)
        bias_rows.append(jnp.where(chosen, 0.0, MASKED_BIAS))
    bias_ref[0] = jnp.concatenate(bias_rows, axis=0).T.astype(BF16)


def _even_proj(x, norm_g, w_in, bdd, bdm, gains, cosd, sind, cosm, sinm):
    B, S, D = x.shape
    tm = EVEN_PROJ_ROWS
    blocks = tm // MOBA_BLOCK
    n_out = w_in.shape[1]
    n_qk = 4 * SECTION
    n_v = 2 * SECTION
    const = lambda b, t: (0, 0)
    tab = pl.BlockSpec((tm, LANES), lambda b, t: (t, 0))
    return pl.pallas_call(
        functools.partial(_even_proj_kernel, tm=tm),
        grid=(B, S // tm),
        in_specs=[
            pl.BlockSpec((1, tm, D), lambda b, t: (b, t, 0)),
            pl.BlockSpec((1, D), const),
            pl.BlockSpec((D, n_out), const, pipeline_mode=pl.Buffered(1)),
            pl.BlockSpec((BD, BD), const),
            pl.BlockSpec((BD, BD), const),
            pl.BlockSpec((4, SECTION), const),
            tab, tab, tab, tab,
        ],
        out_specs=[
            pl.BlockSpec((1, tm, n_qk), lambda b, t: (b, t, 0)),
            pl.BlockSpec((1, blocks, n_v, MOBA_BLOCK), lambda b, t: (b, t, 0, 0)),
            pl.BlockSpec((1, tm, LANES), lambda b, t: (b, t, 0)),
        ],
        out_shape=[
            jax.ShapeDtypeStruct((B, S, n_qk), BF16),
            jax.ShapeDtypeStruct((B, S // MOBA_BLOCK, n_v, MOBA_BLOCK), BF16),
            jax.ShapeDtypeStruct((B, S, LANES), BF16),
        ],
        scratch_shapes=[pltpu.VMEM((LANES, SECTION), F32)],
        compiler_params=pltpu.CompilerParams(
            dimension_semantics=("arbitrary", "arbitrary"),
            vmem_limit_bytes=VMEM_LIMIT_BYTES),
        name="even_proj",
    )(x, norm_g, w_in, bdd, bdm, gains, cosd, sind, cosm, sinm)


def _flash_head_pair(make_q, n, v_rows, k_ref, vt_ref, s_ref, p_ref, nq, t, emit, k_aug_ref=None):
    streams = k_ref.shape[0]
    per_stream = n // streams

    def scores(q_all, j):
        start = pl.multiple_of(j * t, t)
        out = []
        for g in range(streams):
            k = k_ref[g, pl.ds(start, t), :]
            if k_aug_ref is not None:
                k = jnp.concatenate([k, k_aug_ref[pl.ds(start, t), :]], axis=1)
            out.append(lax.dot_general(k, q_all[g], _NT, preferred_element_type=F32))
        return jnp.concatenate(out, axis=1)

    def softmax(s_all, stats, diagonal):
        ps, new_stats, alphas = [], [], []
        for i in range(n):
            m, l = stats[i]
            s = s_all[:, i * t:(i + 1) * t]
            if diagonal:
                key = lax.broadcasted_iota(jnp.int32, (t, t), 0)
                qry = lax.broadcasted_iota(jnp.int32, (t, t), 1)
                s = jnp.where(key <= qry, s, -jnp.inf)
            m_new = jnp.maximum(m, jnp.max(s, axis=0, keepdims=True))
            p = jnp.exp2(s - m_new)
            alpha = jnp.exp2(m - m_new)
            ps.append(p.astype(BF16))
            new_stats.append((m_new, alpha * l + jnp.sum(p, axis=0, keepdims=True)))
            alphas.append(alpha)
        return tuple(ps), tuple(new_stats), alphas

    def weighted_values(j, ps):
        out = []
        for g in range(streams):
            vt = vt_ref[g, j]
            for li, (lo, hi) in enumerate(v_rows):
                out.append(jnp.dot(vt[lo:hi], ps[g * per_stream + li], preferred_element_type=F32))
        return out

    def step(q_all, j, cur, nxt, carry):
        stats, accs = carry
        s_ref[nxt] = scores(q_all, j + 1)
        p_prev = tuple(p_ref[nxt, i] for i in range(n))
        pv = weighted_values(jnp.maximum(j - 1, 0), p_prev)
        pv = [jnp.where(j > 0, x, 0.0) for x in pv]
        p_cur, stats, alphas = softmax(s_ref[cur], stats, False)
        for i in range(n):
            p_ref[cur, i] = p_cur[i]
        accs = tuple(a * (acc + x) for a, acc, x in zip(alphas, accs, pv))
        return stats, accs

    def query_tile(qi, first, second, odd, next_first):
        q_all = make_q(qi)
        init = (tuple((jnp.full((1, t), M_INIT, F32), jnp.zeros((1, t), F32)) for _ in range(n)),
                tuple(jnp.zeros((hi - lo, t), F32) for _ in range(streams) for lo, hi in v_rows))
        carry = lax.fori_loop(
            0, qi // 2,
            lambda i, c: step(q_all, 2 * i + 1, second, first, step(q_all, 2 * i, first, second, c)), init)
        if odd:
            carry = step(q_all, qi - 1, first, second, carry)
        diag, prev = (second, first) if odd else (first, second)
        stats, accs = carry
        s_ref[next_first] = scores(make_q(jnp.minimum(qi + 1, nq - 1)), 0)
        pv_prev = weighted_values(jnp.maximum(qi - 1, 0), tuple(p_ref[prev, i] for i in range(n)))
        pv_prev = [jnp.where(qi > 0, x, 0.0) for x in pv_prev]
        p_cur, stats, alphas = softmax(s_ref[diag], stats, True)
        pv_cur = weighted_values(qi, p_cur)
        emit(qi, [(a * (acc + x) + y, l)
                  for a, acc, x, y, (_, l) in zip(alphas, accs, pv_prev, pv_cur, stats)])

    p_ref[...] = jnp.zeros(p_ref.shape, BF16)
    s_ref[0] = scores(make_q(0), 0)

    def pair_of_tiles(u, _):
        query_tile(2 * u, 0, 1, False, 2)
        query_tile(2 * u + 1, 2, 1, True, 0)
        return 0

    lax.fori_loop(0, nq // 2, pair_of_tiles, 0)


def _flash_bounded(make_q, n, v_rows, k_ref, vt_ref, s_ref, nq, t, emit, k_aug_ref=None):
    streams = k_ref.shape[0]
    per_stream = n // streams
    ones = jnp.ones((ONES_ROWS, t), BF16)
    width = per_stream * t
    key = lax.broadcasted_iota(jnp.int32, (t, width), 0)
    qry = lax.broadcasted_iota(jnp.int32, (t, width), 1) & (t - 1)

    def scores(q_all, j, slot):
        start = pl.multiple_of(j * t, t)
        for g in range(streams):
            k = k_ref[g, pl.ds(start, t), :]
            if k_aug_ref is not None:
                k = jnp.concatenate([k, k_aug_ref[pl.ds(start, t), :]], axis=1)
            s_ref[slot, g] = lax.dot_general(k, q_all[g], _NT, preferred_element_type=F32)

    def accumulate(j, slot, accs, diagonal):
        out = []
        for g in range(streams):
            p = jnp.exp2(s_ref[slot, g])
            if diagonal:
                p = jnp.where(key <= qry, p, 0.0)
            p = p.astype(BF16)
            vt = vt_ref[g, j]
            for li, (lo, hi) in enumerate(v_rows):
                lhs = jnp.concatenate([vt[lo:hi], ones], axis=0)
                out.append(accs[g * per_stream + li]
                           + jnp.dot(lhs, p[:, li * t:(li + 1) * t], preferred_element_type=F32))
        return tuple(out)

    def step(q_all, j, cur, nxt, accs):
        scores(q_all, j + 1, nxt)
        return accumulate(j, cur, accs, False)

    rows = [hi - lo for _ in range(streams) for lo, hi in v_rows]

    def query_tile(u, first, second, odd, next_first):
        qi = 2 * u + odd
        q_all = make_q(qi)

        def four_steps(i, a):
            j = 4 * i
            a = step(q_all, j, first, second, a)
            a = step(q_all, j + 1, second, first, a)
            a = step(q_all, j + 2, first, second, a)
            return step(q_all, j + 3, second, first, a)

        init = tuple(jnp.zeros((r + ONES_ROWS, t), F32) for r in rows)
        accs = lax.fori_loop(0, qi // 4, four_steps, init)

        def tail(accs, two_more):
            j = 4 * (qi // 4)
            if two_more:
                accs = step(q_all, j, first, second, accs)
                accs = step(q_all, j + 1, second, first, accs)
                j = j + 2
            if odd:
                accs = step(q_all, j, first, second, accs)
            scores(make_q(jnp.minimum(qi + 1, nq - 1)), 0, next_first)
            accs = accumulate(qi, second if odd else first, accs, True)
            emit(qi, [(a[:r], a[r:r + 1]) for a, r in zip(accs, rows)])

        pl.when((u & 1) == 0)(lambda: tail(accs, False))
        pl.when((u & 1) == 1)(lambda: tail(accs, True))

    scores(make_q(0), 0, 0)

    def pair_of_tiles(u, _):
        query_tile(u, 0, 1, 0, 2)
        query_tile(u, 2, 1, 1, 0)
        return 0

    lax.fori_loop(0, nq // 2, pair_of_tiles, 0)


def _diff_attn_kernel(lam_ref, q_ref, k_ref, vt_ref, sg_ref, o_ref, *scratch, t, lambda_init, bounded):
    nq = q_ref.shape[1] // t
    seg = lax.broadcasted_iota(jnp.int32, (t, LANES), 1) // DIFF_HALF
    lam_p = lam_ref[...]
    lam = (jnp.exp(jnp.sum(lam_p[0:1] * lam_p[1:2], axis=1, keepdims=True))
           - jnp.exp(jnp.sum(lam_p[2:3] * lam_p[3:4], axis=1, keepdims=True)) + lambda_init)

    streams = q_ref.shape[0]

    def make_q(qi):
        out = []
        for g in range(streams):
            q = q_ref[g, pl.ds(pl.multiple_of(qi * t, t), t), :]
            out.append(jnp.concatenate([jnp.where(seg == i, q, jnp.zeros_like(q)) for i in range(4)], axis=0))
        return out

    def emit(qi, res):
        a = [acc * (1.0 / l) for acc, l in res]
        for g in range(streams):
            heads = []
            for hh in range(2):
                o = a[4 * g + 2 * hh] - lam * a[4 * g + 2 * hh + 1]
                heads.append(o * lax.rsqrt(jnp.mean(o * o, axis=0, keepdims=True) + EPS))
            o = jnp.concatenate(heads, axis=0).T
            o_ref[g, pl.ds(pl.multiple_of(qi * t, t), t), :] = (
                o * sg_ref[...] * (1.0 - lambda_init)).astype(BF16)

    v_rows = [(HEAD_DIM * (i // 2), HEAD_DIM * (i // 2 + 1)) for i in range(4)]
    if bounded:
        _flash_bounded(make_q, 4 * streams, v_rows, k_ref, vt_ref, *scratch, nq, t, emit)
    else:
        _flash_head_pair(make_q, 4 * streams, v_rows, k_ref, vt_ref, *scratch, nq, t, emit)


def _moba_attn_kernel(q_ref, bias_ref, k_ref, vt_ref, ind_ref, o_ref, *scratch, t, bounded):
    hp = pl.program_id(1)
    nq = q_ref.shape[1] // t
    lane = lax.broadcasted_iota(jnp.int32, (t, LANES), 1)

    streams = q_ref.shape[0]

    def make_q(qi):
        rows = pl.ds(pl.multiple_of(qi * t, t), t)
        out = []
        for g in range(streams):
            q = q_ref[g, rows, :]
            bias = bias_ref[g, rows, :]
            strips = []
            for hh in range(2):
                qz = jnp.where(lane // HEAD_DIM == hh, q, jnp.zeros_like(q))
                bz = jnp.where(lane // GATE_SEG == 2 * hp + hh, bias, jnp.zeros_like(bias))
                strips.append(jnp.concatenate([qz, bz], axis=1))
            out.append(jnp.concatenate(strips, axis=0))
        return out

    def emit(qi, res):
        for g in range(streams):
            o = jnp.concatenate([acc * (1.0 / l) for acc, l in res[2 * g:2 * g + 2]], axis=0)
            o_ref[g, pl.ds(pl.multiple_of(qi * t, t), t), :] = o.T.astype(BF16)

    v_rows = [(0, HEAD_DIM), (HEAD_DIM, 2 * HEAD_DIM)]
    if bounded:
        _flash_bounded(make_q, 2 * streams, v_rows, k_ref, vt_ref, *scratch, nq, t, emit, k_aug_ref=ind_ref)
    else:
        _flash_head_pair(make_q, 2 * streams, v_rows, k_ref, vt_ref, *scratch, nq, t, emit, k_aug_ref=ind_ref)


def _attention(qk, vt, bias, ind, lam_params, subln_tile, lambda_init, bounded):
    B, S, _ = qk.shape
    t = MOBA_BLOCK
    n_pairs = SECTION // LANES
    assert (S // t) % 2 == 0
    params = pltpu.CompilerParams(
        dimension_semantics=("arbitrary", "arbitrary"),
        vmem_limit_bytes=VMEM_LIMIT_BYTES)
    out_shape = jax.ShapeDtypeStruct((B, S, SECTION), BF16)

    def streams(wanted):
        return wanted if B % wanted == 0 else 1

    def seq_spec(g, col):
        return pl.BlockSpec((g, S, LANES), lambda b, p: (b, 0, col(p)))

    def vt_spec(g, sec):
        return pl.BlockSpec((g, S // t, LANES, t), lambda b, p: (b, 0, sec * n_pairs + p, 0))

    def pipeline_scratch(g, n):
        if bounded:
            return [pltpu.VMEM((3, g, t, n * t), F32)]
        return [pltpu.VMEM((3, t, g * n * t), F32), pltpu.VMEM((3, g * n, t, t), BF16)]

    g = streams(ATTN_STREAMS)
    diff = pl.pallas_call(
        functools.partial(_diff_attn_kernel, t=t, lambda_init=lambda_init, bounded=bounded),
        grid=(B // g, n_pairs),
        in_specs=[pl.BlockSpec((4, DIFF_HALF), lambda b, p: (0, 0)),
                  seq_spec(g, lambda p: p), seq_spec(g, lambda p: n_pairs + p), vt_spec(g, 0),
                  pl.BlockSpec((1, LANES), lambda b, p: (0, 0))],
        out_specs=seq_spec(g, lambda p: p), out_shape=out_shape, compiler_params=params,
        scratch_shapes=pipeline_scratch(g, 4),
        name="diff_attn",
    )(lam_params, qk, qk, vt, subln_tile)

    g = streams(MOBA_STREAMS_BOUNDED if bounded else ATTN_STREAMS)
    moba = pl.pallas_call(
        functools.partial(_moba_attn_kernel, t=t, bounded=bounded),
        grid=(B // g, n_pairs),
        in_specs=[seq_spec(g, lambda p: 2 * n_pairs + p),
                  seq_spec(g, lambda p: 0),
                  seq_spec(g, lambda p: 3 * n_pairs + p), vt_spec(g, 1),
                  pl.BlockSpec((S, LANES), lambda b, p: (0, 0))],
        out_specs=seq_spec(g, lambda p: p), out_shape=out_shape, compiler_params=params,
        scratch_shapes=pipeline_scratch(g, 2),
        name="moba_attn",
    )(qk, bias, qk, vt, ind)
    return diff, moba


def _out_ffn_kernel(mixa_ref, mixb_ref, x_ref, wo_ref, g_ref, wi_ref, w2_ref, o_ref, *, d_ff, chunk):
    mix = jnp.concatenate([mixa_ref[0], mixb_ref[0]], axis=1)
    x1 = x_ref[0] + jnp.dot(mix, wo_ref[...].astype(BF16), preferred_element_type=F32)
    hn = _rms(x1, g_ref[...]).astype(BF16)
    acc = x1
    for c in range(d_ff // chunk):
        lo = c * chunk
        g = jnp.dot(hn, wi_ref[0, :, lo:lo + chunk].astype(BF16), preferred_element_type=F32)
        u = jnp.dot(hn, wi_ref[0, :, d_ff + lo:d_ff + lo + chunk].astype(BF16), preferred_element_type=F32)
        a = (g * jax.nn.sigmoid(g) * u).astype(BF16)
        acc = acc + jnp.dot(a, w2_ref[0, lo:lo + chunk, :].astype(BF16), preferred_element_type=F32)
    o_ref[0] = acc


def _out_ffn(mix_a, col_a, mix_b, col_b, x, w_out, norm_g, w_in, w2, layer):
    B, S, D = x.shape
    d_ff = w2.shape[1]
    tm = 512
    half = w_out.shape[0] // 2
    const = lambda b, t: (0, 0)
    resident = lambda shape: pl.BlockSpec(shape, const, pipeline_mode=pl.Buffered(1))
    slab = lambda a: pl.BlockSpec((1,) + a.shape[1:], lambda b, t: (layer, 0, 0), pipeline_mode=pl.Buffered(1))
    tile = pl.BlockSpec((1, tm, D), lambda b, t: (b, t, 0))
    mix_spec = lambda col: pl.BlockSpec((1, tm, half), lambda b, t: (b, t, col))
    return pl.pallas_call(
        functools.partial(_out_ffn_kernel, d_ff=d_ff, chunk=256),
        grid=(B, S // tm),
        in_specs=[mix_spec(col_a), mix_spec(col_b), tile, resident(w_out.shape), pl.BlockSpec((1, D), const),
                  slab(w_in), slab(w2)],
        out_specs=tile,
        out_shape=jax.ShapeDtypeStruct((B, S, D), F32),
        compiler_params=pltpu.CompilerParams(
            dimension_semantics=("arbitrary", "arbitrary"),
            vmem_limit_bytes=VMEM_LIMIT_BYTES),
        name="out_ffn",
    )(mix_a, mix_b, x, w_out, norm_g, w_in, w2)


def _odd_mix_kernel(x_ref, g_ref, w_ref, b_ref, lng_ref, lnb_ref, ws_ref, bs_ref,
                    cw_ref, cb_ref, clg_ref, clb_ref, o_ref, ext_ref, *, tm):
    ti = pl.program_id(1)
    h = _rms(x_ref[0], g_ref[...]).astype(BF16)

    def proj(lo, width):
        return jnp.dot(h, w_ref[:, lo:lo + width], preferred_element_type=F32) + b_ref[:, lo:lo + width]

    gu = jax.nn.gelu(proj(0, GMLP_WIDTH), approximate=True)
    gv = jax.nn.gelu(proj(GMLP_WIDTH, GMLP_WIDTH), approximate=True)
    gv = _layer_norm(gv, lng_ref[...], lnb_ref[...]).astype(BF16)
    T = GMLP_CHUNK
    tri = lax.broadcasted_iota(jnp.int32, (T, T), 1) <= lax.broadcasted_iota(jnp.int32, (T, T), 0)
    lane = lax.broadcasted_iota(jnp.int32, (T, LANES), 1)
    first = lane < (GMLP_WIDTH // GMLP_GROUPS)
    zero = jnp.zeros((T, LANES), BF16)
    for lb in range(GMLP_WIDTH // LANES):
        wpair = jnp.concatenate(
            [jnp.where(tri, ws_ref[2 * lb + i], 0.0).astype(BF16) for i in range(2)], axis=1)
        bs = bs_ref[:, lb * LANES:(lb + 1) * LANES]
        for c in range(tm // T):
            vc = gv[c * T:(c + 1) * T, lb * LANES:(lb + 1) * LANES]
            vpair = jnp.concatenate([jnp.where(first, vc, zero), jnp.where(first, zero, vc)], axis=0)
            sg = jnp.dot(wpair, vpair, preferred_element_type=F32) + bs
            o_ref[0, c * T:(c + 1) * T, lb * LANES:(lb + 1) * LANES] = (
                gu[c * T:(c + 1) * T, lb * LANES:(lb + 1) * LANES] * sg).astype(BF16)

    lo = 2 * GMLP_WIDTH
    cval = proj(lo, CONV_CH) * jax.nn.sigmoid(proj(lo + CONV_CH, CONV_CH))

    @pl.when(ti == 0)
    def _():
        ext_ref[0:CONV_HALO, :] = jnp.zeros((CONV_HALO, CONV_CH), F32)

    ext_ref[CONV_HALO:CONV_HALO + tm, :] = cval
    first_tap = CONV_HALO - (CONV_KERNEL - 1)
    ext = ext_ref[...]
    rows_total = ext.shape[0]
    conv = jnp.zeros((tm, CONV_CH), F32) + cb_ref[...]
    for r in range(8):
        rolled = ext if r == 0 else pltpu.roll(ext, rows_total - r, 0)
        for k in range(CONV_KERNEL):
            if (first_tap + k) % 8 == r:
                off = first_tap + k - r
                conv = conv + rolled[off:off + tm, :] * cw_ref[k:k + 1, :]
    ext_ref[0:CONV_HALO, :] = ext_ref[tm:tm + CONV_HALO, :]
    dn = _layer_norm(conv, clg_ref[...], clb_ref[...])
    o_ref[0, :, GMLP_WIDTH:GMLP_WIDTH + CONV_CH] = (dn * jax.nn.sigmoid(dn)).astype(BF16)


def _odd_mix(x, norm_g, w_in, b_in, ln_g, ln_b, w_s, bs_tile, conv_w, conv_b, cln_g, cln_b):
    B, S, D = x.shape
    tm = 512
    const2 = lambda b, t: (0, 0)
    full = lambda a: pl.BlockSpec(a.shape, (lambda b, t: (0,) * a.ndim))
    n_out = GMLP_WIDTH + CONV_CH
    span = tm + 8 * ((CONV_KERNEL - 1) // 8)
    args = (x, norm_g, w_in, b_in, ln_g, ln_b, w_s, bs_tile, conv_w, conv_b, cln_g, cln_b)
    return pl.pallas_call(
        functools.partial(_odd_mix_kernel, tm=tm),
        grid=(B, S // tm),
        in_specs=[pl.BlockSpec((1, tm, D), lambda b, t: (b, t, 0))] + [full(a) for a in args[1:]],
        out_specs=pl.BlockSpec((1, tm, n_out), lambda b, t: (b, t, 0)),
        out_shape=jax.ShapeDtypeStruct((B, S, n_out), BF16),
        scratch_shapes=[pltpu.VMEM((span + 8, CONV_CH), F32)],
        compiler_params=pltpu.CompilerParams(
            dimension_semantics=("arbitrary", "arbitrary"),
            vmem_limit_bytes=VMEM_LIMIT_BYTES),
        name="odd_mix",
    )(*args)


def _rope_tables(seq, dim, seg_per_block):
    inv = 1.0 / (ROPE_THETA ** (jnp.arange(0, dim, 2, dtype=F32) / dim))
    ang = jnp.arange(seq, dtype=F32)[:, None] * inv[None, :]
    cos, sin = jnp.cos(ang), jnp.sin(ang)
    cos = jnp.tile(jnp.concatenate([cos, cos], axis=1), (1, seg_per_block))
    sin = jnp.tile(jnp.concatenate([-sin, sin], axis=1), (1, seg_per_block))
    return cos, sin


def _block_diag_ones(seg):
    idx = jnp.arange(BD) // seg
    return (idx[:, None] == idx[None, :]).astype(BF16)


def kernel(x, attn_norm_g, ffn_norm_g, ffn_w_in, ffn_w_out, even_w_in, even_w_out,
           diff_q_norm_g, diff_k_norm_g, diff_lambda_q1, diff_lambda_k1,
           diff_lambda_q2, diff_lambda_k2, diff_subln_g, moba_q_norm_g, moba_k_norm_g,
           odd_w_in, odd_b_in, odd_w_out, gmlp_ln_g, gmlp_ln_b, gmlp_w_s, gmlp_b_s,
           conv_w, conv_b, conv_ln_g, conv_ln_b):
    B, S, D = x.shape
    assert S % EVEN_PROJ_ROWS == 0 and S // MOBA_BLOCK <= GATE_SEG
    row = lambda v: v.reshape(1, -1).astype(F32)

    lambda_init = 0.8 - 0.6 * math.exp(-0.3 * 0)
    cosd, sind = _rope_tables(S, DIFF_HALF, LANES // DIFF_HALF)
    cosm, sinm = _rope_tables(S, HEAD_DIM, LANES // HEAD_DIM)
    gains = jnp.stack([
        jnp.tile(diff_q_norm_g[0], SECTION // DIFF_HALF),
        jnp.tile(diff_k_norm_g[0], SECTION // DIFF_HALF),
        jnp.tile(moba_q_norm_g[0], SECTION // HEAD_DIM),
        jnp.tile(moba_k_norm_g[0], SECTION // HEAD_DIM)]).astype(F32)
    qk, vt, bias = _even_proj(x, row(attn_norm_g[0]), even_w_in[0].astype(BF16),
                              _block_diag_ones(DIFF_HALF), _block_diag_ones(HEAD_DIM), gains,
                              cosd, sind, cosm, sinm)
    key_block = jnp.arange(S)[:, None] // MOBA_BLOCK
    ind = (key_block == (jnp.arange(LANES)[None, :] % GATE_SEG)).astype(BF16)
    lam_params = jnp.stack([diff_lambda_q1[0], diff_lambda_k1[0],
                            diff_lambda_q2[0], diff_lambda_k2[0]]).astype(F32)
    subln_tile = jnp.tile(diff_subln_g[0], LANES // HEAD_DIM).reshape(1, LANES).astype(F32)
    def score_bound(d, gq, gk):
        return (d * d ** -0.5 * LOG2E * (1.0 + 2.0 ** -7)
                * jnp.max(jnp.abs(gq.astype(F32))) * jnp.max(jnp.abs(gk.astype(F32))))

    bounded = jnp.logical_and(score_bound(DIFF_HALF, diff_q_norm_g[0], diff_k_norm_g[0]) <= SCORE_BOUND,
                              score_bound(HEAD_DIM, moba_q_norm_g[0], moba_k_norm_g[0]) <= SCORE_BOUND)
    attend = lambda flag: (lambda *a: _attention(*a, lambda_init, flag))
    diff, moba = lax.cond(bounded, attend(True), attend(False), qk, vt, bias, ind, lam_params, subln_tile)
    x = _out_ffn(diff, 0, moba, 0, x, even_w_out[0], row(ffn_norm_g[0]), ffn_w_in, ffn_w_out, 0)

    bs_tile = jnp.repeat(gmlp_b_s[0].T, GMLP_WIDTH // GMLP_GROUPS, axis=1).astype(F32)
    mix = _odd_mix(x, row(attn_norm_g[1]), odd_w_in[0].astype(BF16), row(odd_b_in[0]),
                   row(gmlp_ln_g[0]), row(gmlp_ln_b[0]), gmlp_w_s[0].astype(F32), bs_tile,
                   conv_w[0].astype(F32), row(conv_b[0]), row(conv_ln_g[0]), row(conv_ln_b[0]))
    x = _out_ffn(mix, 0, mix, 1, x, odd_w_out[0], row(ffn_norm_g[1]), ffn_w_in, ffn_w_out, 1)
    return x
```

```python
import functools
import math

import jax
import jax.numpy as jnp
from jax import lax
from jax.experimental import pallas as pl
from jax.experimental.pallas import tpu as pltpu

F32 = jnp.float32
BF16 = jnp.bfloat16

LANES = 128
VMEM_LIMIT_BYTES = 56 * 1024 * 1024

HEAD_DIM = 64
DIFF_HALF = 32
N_HEADS_MOBA = 8
MOBA_BLOCK = 256
MOBA_TOPK = 3
GMLP_CHUNK = 128
GMLP_GROUPS = 8
GMLP_WIDTH = 512
CONV_CH = 512
CONV_KERNEL = 31
ROPE_THETA = 10000.0
EPS = 1e-6

SECTION = 512
BD = 256
EVEN_PROJ_ROWS = 1024
ODD_MIX_ROWS = 1024
OUT_FFN_ROWS = 512
ATTN_STREAMS = 2
MOBA_STREAMS_BOUNDED = 4
SCORE_BOUND = 40.0
ONES_ROWS = 16
GATE_SEG = LANES // N_HEADS_MOBA
CONV_HALO = 32
MASKED_BIAS = -1e30
M_INIT = -1e29
LOG2E = math.log2(math.e)

_NT = (((1,), (1,)), ((), ()))


def _rms(x, g):
    return x * lax.rsqrt(jnp.mean(x * x, axis=-1, keepdims=True) + EPS) * g


def _layer_norm(x, g, b):
    mu = jnp.mean(x, axis=-1, keepdims=True)
    xc = x - mu
    return xc * lax.rsqrt(jnp.mean(xc * xc, axis=-1, keepdims=True) + EPS) * g + b


def _even_proj_kernel(x_ref, g_ref, w_ref, bdd_ref, bdm_ref, gain_ref,
                      cosd_ref, sind_ref, cosm_ref, sinm_ref,
                      qk_ref, vt_ref, bias_ref, kbt_ref, *, tm):
    blocks = tm // MOBA_BLOCK
    first_blk = pl.program_id(1) * blocks
    h = _rms(x_ref[0], g_ref[...]).astype(BF16)

    def qk_section(sec, bd_ref, seg, cos_ref, sin_ref):
        y = jnp.dot(h, w_ref[:, sec * SECTION:(sec + 1) * SECTION], preferred_element_type=F32)
        y2 = (y * y).astype(BF16)
        ss = jnp.concatenate(
            [jnp.dot(y2[:, c * BD:(c + 1) * BD], bd_ref[...], preferred_element_type=F32)
             for c in range(SECTION // BD)], axis=1)
        yn = y * lax.rsqrt(ss * (1.0 / seg) + EPS) * gain_ref[sec_row[sec]:sec_row[sec] + 1, :]
        half = seg // 2
        lower = (lax.broadcasted_iota(jnp.int32, (tm, LANES), 1) & (seg - 1)) < half
        cos = cos_ref[...]
        sin = sin_ref[...]
        out = []
        for c in range(SECTION // LANES):
            yc = yn[:, c * LANES:(c + 1) * LANES]
            rot = jnp.where(lower, pltpu.roll(yc, LANES - half, 1), pltpu.roll(yc, half, 1))
            out.append(yc * cos + rot * sin)
        return out

    def store(out_sec, chunks, scale):
        for c, yc in enumerate(chunks):
            lo = out_sec * SECTION + c * LANES
            qk_ref[0, :, lo:lo + LANES] = (yc * scale).astype(BF16)

    sec_row = {0: 0, 1: 1, 3: 2, 4: 3}
    store(0, qk_section(0, bdd_ref, DIFF_HALF, cosd_ref, sind_ref), DIFF_HALF ** -0.5 * LOG2E)
    store(1, qk_section(1, bdd_ref, DIFF_HALF, cosd_ref, sind_ref), 1.0)
    mq = qk_section(3, bdm_ref, HEAD_DIM, cosm_ref, sinm_ref)
    store(2, mq, HEAD_DIM ** -0.5 * LOG2E)
    mk = qk_section(4, bdm_ref, HEAD_DIM, cosm_ref, sinm_ref)
    store(3, mk, 1.0)
    for out_sec, sec in enumerate((2, 5)):
        lo = sec * SECTION
        vt = jnp.dot(h, w_ref[:, lo:lo + SECTION], preferred_element_type=F32).T.astype(BF16)
        for sb in range(blocks):
            vt_ref[0, sb, out_sec * SECTION:(out_sec + 1) * SECTION, :] = (
                vt[:, sb * MOBA_BLOCK:(sb + 1) * MOBA_BLOCK])

    @pl.when(first_blk == 0)
    def _():
        kbt_ref[...] = jnp.zeros_like(kbt_ref)

    mk_full = jnp.concatenate(mk, axis=1)
    head_of_lane = lax.broadcasted_iota(jnp.int32, (1, SECTION), 1) // HEAD_DIM
    for sb in range(blocks):
        kbar = jnp.mean(mk_full[sb * MOBA_BLOCK:(sb + 1) * MOBA_BLOCK], axis=0, keepdims=True)
        for hh in range(N_HEADS_MOBA):
            kbt_ref[pl.ds(hh * GATE_SEG + first_blk + sb, 1), :] = jnp.where(head_of_lane == hh, kbar, 0.0)

    def split(a):
        hi = a.astype(BF16)
        return hi, (a - hi.astype(F32)).astype(BF16)

    q_hi, q_lo = split(jnp.concatenate(mq, axis=1))
    k_hi, k_lo = split(kbt_ref[...])
    gate = lax.dot_general(jnp.concatenate([k_hi, k_hi, k_lo], axis=1),
                           jnp.concatenate([q_hi, q_lo, q_hi], axis=1), _NT,
                           preferred_element_type=F32)
    blk = lax.broadcasted_iota(jnp.int32, (GATE_SEG, tm), 0)
    blk_f = blk.astype(F32)
    neg_inf = jnp.float32(-jnp.inf)
    own = first_blk + lax.broadcasted_iota(jnp.int32, (GATE_SEG, tm), 1) // MOBA_BLOCK
    bias_rows = []
    for hh in range(N_HEADS_MOBA):
        avail = jnp.where(blk < own, gate[hh * GATE_SEG:(hh + 1) * GATE_SEG], neg_inf)
        chosen = blk == own
        for _ in range(MOBA_TOPK):
            best = jnp.max(avail, axis=0, keepdims=True)
            cand = jnp.logical_and(avail == best, avail > neg_inf)
            first = jnp.min(jnp.where(cand, blk_f, float(GATE_SEG)), axis=0, keepdims=True)
            pick = blk_f == first
            chosen = jnp.logical_or(chosen, pick)
            avail = jnp.where(pick, neg_inf, avail)
        bias_rows.append(jnp.where(chosen, 0.0, MASKED_BIAS))
    bias_ref[0] = jnp.concatenate(bias_rows, axis=0).T.astype(BF16)


def _even_proj(x, norm_g, w_in, bdd, bdm, gains, cosd, sind, cosm, sinm):
    B, S, D = x.shape
    tm = EVEN_PROJ_ROWS
    blocks = tm // MOBA_BLOCK
    n_out = w_in.shape[1]
    n_qk = 4 * SECTION
    n_v = 2 * SECTION
    const = lambda b, t: (0, 0)
    tab = pl.BlockSpec((tm, LANES), lambda b, t: (t, 0))
    return pl.pallas_call(
        functools.partial(_even_proj_kernel, tm=tm),
        grid=(B, S // tm),
        in_specs=[
            pl.BlockSpec((1, tm, D), lambda b, t: (b, t, 0)),
            pl.BlockSpec((1, D), const),
            pl.BlockSpec((D, n_out), const, pipeline_mode=pl.Buffered(1)),
            pl.BlockSpec((BD, BD), const),
            pl.BlockSpec((BD, BD), const),
            pl.BlockSpec((4, SECTION), const),
            tab, tab, tab, tab,
        ],
        out_specs=[
            pl.BlockSpec((1, tm, n_qk), lambda b, t: (b, t, 0)),
            pl.BlockSpec((1, blocks, n_v, MOBA_BLOCK), lambda b, t: (b, t, 0, 0)),
            pl.BlockSpec((1, tm, LANES), lambda b, t: (b, t, 0)),
        ],
        out_shape=[
            jax.ShapeDtypeStruct((B, S, n_qk), BF16),
            jax.ShapeDtypeStruct((B, S // MOBA_BLOCK, n_v, MOBA_BLOCK), BF16),
            jax.ShapeDtypeStruct((B, S, LANES), BF16),
        ],
        scratch_shapes=[pltpu.VMEM((LANES, SECTION), F32)],
        compiler_params=pltpu.CompilerParams(
            dimension_semantics=("arbitrary", "arbitrary"),
            vmem_limit_bytes=VMEM_LIMIT_BYTES),
        name="even_proj",
    )(x, norm_g, w_in, bdd, bdm, gains, cosd, sind, cosm, sinm)


def _flash_head_pair(make_q, n, v_rows, k_ref, vt_ref, s_ref, p_ref, nq, t, emit, k_aug_ref=None):
    streams = k_ref.shape[0]
    per_stream = n // streams

    def scores(q_all, j):
        start = pl.multiple_of(j * t, t)
        out = []
        for g in range(streams):
            k = k_ref[g, pl.ds(start, t), :]
            if k_aug_ref is not None:
                k = jnp.concatenate([k, k_aug_ref[pl.ds(start, t), :]], axis=1)
            out.append(lax.dot_general(k, q_all[g], _NT, preferred_element_type=F32))
        return jnp.concatenate(out, axis=1)

    def softmax(s_all, stats, diagonal):
        ps, new_stats, alphas = [], [], []
        for i in range(n):
            m, l = stats[i]
            s = s_all[:, i * t:(i + 1) * t]
            if diagonal:
                key = lax.broadcasted_iota(jnp.int32, (t, t), 0)
                qry = lax.broadcasted_iota(jnp.int32, (t, t), 1)
                s = jnp.where(key <= qry, s, -jnp.inf)
            m_new = jnp.maximum(m, jnp.max(s, axis=0, keepdims=True))
            p = jnp.exp2(s - m_new)
            alpha = jnp.exp2(m - m_new)
            ps.append(p.astype(BF16))
            new_stats.append((m_new, alpha * l + jnp.sum(p, axis=0, keepdims=True)))
            alphas.append(alpha)
        return tuple(ps), tuple(new_stats), alphas

    def weighted_values(j, ps):
        out = []
        for g in range(streams):
            vt = vt_ref[g, j]
            for li, (lo, hi) in enumerate(v_rows):
                out.append(jnp.dot(vt[lo:hi], ps[g * per_stream + li], preferred_element_type=F32))
        return out

    def step(q_all, j, cur, nxt, carry):
        stats, accs = carry
        s_ref[nxt] = scores(q_all, j + 1)
        p_prev = tuple(p_ref[nxt, i] for i in range(n))
        pv = weighted_values(jnp.maximum(j - 1, 0), p_prev)
        pv = [jnp.where(j > 0, x, 0.0) for x in pv]
        p_cur, stats, alphas = softmax(s_ref[cur], stats, False)
        for i in range(n):
            p_ref[cur, i] = p_cur[i]
        accs = tuple(a * (acc + x) for a, acc, x in zip(alphas, accs, pv))
        return stats, accs

    def query_tile(qi, first, second, odd, next_first):
        q_all = make_q(qi)
        init = (tuple((jnp.full((1, t), M_INIT, F32), jnp.zeros((1, t), F32)) for _ in range(n)),
                tuple(jnp.zeros((hi - lo, t), F32) for _ in range(streams) for lo, hi in v_rows))
        carry = lax.fori_loop(
            0, qi // 2,
            lambda i, c: step(q_all, 2 * i + 1, second, first, step(q_all, 2 * i, first, second, c)), init)
        if odd:
            carry = step(q_all, qi - 1, first, second, carry)
        diag, prev = (second, first) if odd else (first, second)
        stats, accs = carry
        s_ref[next_first] = scores(make_q(jnp.minimum(qi + 1, nq - 1)), 0)
        pv_prev = weighted_values(jnp.maximum(qi - 1, 0), tuple(p_ref[prev, i] for i in range(n)))
        pv_prev = [jnp.where(qi > 0, x, 0.0) for x in pv_prev]
        p_cur, stats, alphas = softmax(s_ref[diag], stats, True)
        pv_cur = weighted_values(qi, p_cur)
        emit(qi, [(a * (acc + x) + y, l)
                  for a, acc, x, y, (_, l) in zip(alphas, accs, pv_prev, pv_cur, stats)])

    p_ref[...] = jnp.zeros(p_ref.shape, BF16)
    s_ref[0] = scores(make_q(0), 0)

    def pair_of_tiles(u, _):
        query_tile(2 * u, 0, 1, False, 2)
        query_tile(2 * u + 1, 2, 1, True, 0)
        return 0

    lax.fori_loop(0, nq // 2, pair_of_tiles, 0)


def _flash_bounded(make_q, n, v_rows, k_ref, vt_ref, s_ref, nq, t, emit, k_aug_ref=None):
    streams = k_ref.shape[0]
    per_stream = n // streams
    ones = jnp.ones((ONES_ROWS, t), BF16)
    width = per_stream * t
    key = lax.broadcasted_iota(jnp.int32, (t, width), 0)
    qry = lax.broadcasted_iota(jnp.int32, (t, width), 1) & (t - 1)

    def scores(q_all, j, slot):
        start = pl.multiple_of(j * t, t)
        for g in range(streams):
            k = k_ref[g, pl.ds(start, t), :]
            if k_aug_ref is not None:
                k = jnp.concatenate([k, k_aug_ref[pl.ds(start, t), :]], axis=1)
            s_ref[slot, g] = lax.dot_general(k, q_all[g], _NT, preferred_element_type=F32)

    def accumulate(j, slot, accs, diagonal):
        out = []
        for g in range(streams):
            p = jnp.exp2(s_ref[slot, g])
            if diagonal:
                p = jnp.where(key <= qry, p, 0.0)
            p = p.astype(BF16)
            vt = vt_ref[g, j]
            for li, (lo, hi) in enumerate(v_rows):
                lhs = jnp.concatenate([vt[lo:hi], ones], axis=0)
                out.append(accs[g * per_stream + li]
                           + jnp.dot(lhs, p[:, li * t:(li + 1) * t], preferred_element_type=F32))
        return tuple(out)

    def step(q_all, j, cur, nxt, accs):
        scores(q_all, j + 1, nxt)
        return accumulate(j, cur, accs, False)

    rows = [hi - lo for _ in range(streams) for lo, hi in v_rows]

    def query_tile(u, first, second, odd, next_first):
        qi = 2 * u + odd
        q_all = make_q(qi)

        def four_steps(i, a):
            j = 4 * i
            a = step(q_all, j, first, second, a)
            a = step(q_all, j + 1, second, first, a)
            a = step(q_all, j + 2, first, second, a)
            return step(q_all, j + 3, second, first, a)

        init = tuple(jnp.zeros((r + ONES_ROWS, t), F32) for r in rows)
        accs = lax.fori_loop(0, qi // 4, four_steps, init)

        def tail(accs, two_more):
            j = 4 * (qi // 4)
            if two_more:
                accs = step(q_all, j, first, second, accs)
                accs = step(q_all, j + 1, second, first, accs)
                j = j + 2
            if odd:
                accs = step(q_all, j, first, second, accs)
            scores(make_q(jnp.minimum(qi + 1, nq - 1)), 0, next_first)
            accs = accumulate(qi, second if odd else first, accs, True)
            emit(qi, [(a[:r], a[r:r + 1]) for a, r in zip(accs, rows)])

        pl.when((u & 1) == 0)(lambda: tail(accs, False))
        pl.when((u & 1) == 1)(lambda: tail(accs, True))

    scores(make_q(0), 0, 0)

    def pair_of_tiles(u, _):
        query_tile(u, 0, 1, 0, 2)
        query_tile(u, 2, 1, 1, 0)
        return 0

    lax.fori_loop(0, nq // 2, pair_of_tiles, 0)


def _diff_attn_kernel(lam_ref, q_ref, k_ref, vt_ref, sg_ref, o_ref, *scratch, t, lambda_init, bounded):
    nq = q_ref.shape[1] // t
    seg = lax.broadcasted_iota(jnp.int32, (t, LANES), 1) // DIFF_HALF
    lam_p = lam_ref[...]
    lam = (jnp.exp(jnp.sum(lam_p[0:1] * lam_p[1:2], axis=1, keepdims=True))
           - jnp.exp(jnp.sum(lam_p[2:3] * lam_p[3:4], axis=1, keepdims=True)) + lambda_init)

    streams = q_ref.shape[0]

    def make_q(qi):
        out = []
        for g in range(streams):
            q = q_ref[g, pl.ds(pl.multiple_of(qi * t, t), t), :]
            out.append(jnp.concatenate([jnp.where(seg == i, q, jnp.zeros_like(q)) for i in range(4)], axis=0))
        return out

    def emit(qi, res):
        a = [acc * (1.0 / l) for acc, l in res]
        for g in range(streams):
            heads = []
            for hh in range(2):
                o = a[4 * g + 2 * hh] - lam * a[4 * g + 2 * hh + 1]
                heads.append(o * lax.rsqrt(jnp.mean(o * o, axis=0, keepdims=True) + EPS))
            o = jnp.concatenate(heads, axis=0).T
            o_ref[g, pl.ds(pl.multiple_of(qi * t, t), t), :] = (
                o * sg_ref[...] * (1.0 - lambda_init)).astype(BF16)

    v_rows = [(HEAD_DIM * (i // 2), HEAD_DIM * (i // 2 + 1)) for i in range(4)]
    if bounded:
        _flash_bounded(make_q, 4 * streams, v_rows, k_ref, vt_ref, *scratch, nq, t, emit)
    else:
        _flash_head_pair(make_q, 4 * streams, v_rows, k_ref, vt_ref, *scratch, nq, t, emit)


def _moba_attn_kernel(q_ref, bias_ref, k_ref, vt_ref, ind_ref, o_ref, *scratch, t, bounded):
    hp = pl.program_id(1)
    nq = q_ref.shape[1] // t
    lane = lax.broadcasted_iota(jnp.int32, (t, LANES), 1)

    streams = q_ref.shape[0]

    def make_q(qi):
        rows = pl.ds(pl.multiple_of(qi * t, t), t)
        out = []
        for g in range(streams):
            q = q_ref[g, rows, :]
            bias = bias_ref[g, rows, :]
            strips = []
            for hh in range(2):
                qz = jnp.where(lane // HEAD_DIM == hh, q, jnp.zeros_like(q))
                bz = jnp.where(lane // GATE_SEG == 2 * hp + hh, bias, jnp.zeros_like(bias))
                strips.append(jnp.concatenate([qz, bz], axis=1))
            out.append(jnp.concatenate(strips, axis=0))
        return out

    def emit(qi, res):
        for g in range(streams):
            o = jnp.concatenate([acc * (1.0 / l) for acc, l in res[2 * g:2 * g + 2]], axis=0)
            o_ref[g, pl.ds(pl.multiple_of(qi * t, t), t), :] = o.T.astype(BF16)

    v_rows = [(0, HEAD_DIM), (HEAD_DIM, 2 * HEAD_DIM)]
    if bounded:
        _flash_bounded(make_q, 2 * streams, v_rows, k_ref, vt_ref, *scratch, nq, t, emit, k_aug_ref=ind_ref)
    else:
        _flash_head_pair(make_q, 2 * streams, v_rows, k_ref, vt_ref, *scratch, nq, t, emit, k_aug_ref=ind_ref)


def _attention(qk, vt, bias, ind, lam_params, subln_tile, lambda_init, bounded):
    B, S, _ = qk.shape
    t = MOBA_BLOCK
    n_pairs = SECTION // LANES
    assert (S // t) % 2 == 0
    params = pltpu.CompilerParams(
        dimension_semantics=("arbitrary", "arbitrary"),
        vmem_limit_bytes=VMEM_LIMIT_BYTES)
    out_shape = jax.ShapeDtypeStruct((B, S, SECTION), BF16)

    def streams(wanted):
        return wanted if B % wanted == 0 else 1

    def seq_spec(g, col):
        return pl.BlockSpec((g, S, LANES), lambda b, p: (b, 0, col(p)))

    def vt_spec(g, sec):
        return pl.BlockSpec((g, S // t, LANES, t), lambda b, p: (b, 0, sec * n_pairs + p, 0))

    def pipeline_scratch(g, n):
        if bounded:
            return [pltpu.VMEM((3, g, t, n * t), F32)]
        return [pltpu.VMEM((3, t, g * n * t), F32), pltpu.VMEM((3, g * n, t, t), BF16)]

    g = streams(ATTN_STREAMS)
    diff = pl.pallas_call(
        functools.partial(_diff_attn_kernel, t=t, lambda_init=lambda_init, bounded=bounded),
        grid=(B // g, n_pairs),
        in_specs=[pl.BlockSpec((4, DIFF_HALF), lambda b, p: (0, 0)),
                  seq_spec(g, lambda p: p), seq_spec(g, lambda p: n_pairs + p), vt_spec(g, 0),
                  pl.BlockSpec((1, LANES), lambda b, p: (0, 0))],
        out_specs=seq_spec(g, lambda p: p), out_shape=out_shape, compiler_params=params,
        scratch_shapes=pipeline_scratch(g, 4),
        name="diff_attn",
    )(lam_params, qk, qk, vt, subln_tile)

    g = streams(MOBA_STREAMS_BOUNDED if bounded else ATTN_STREAMS)
    moba = pl.pallas_call(
        functools.partial(_moba_attn_kernel, t=t, bounded=bounded),
        grid=(B // g, n_pairs),
        in_specs=[seq_spec(g, lambda p: 2 * n_pairs + p),
                  seq_spec(g, lambda p: 0),
                  seq_spec(g, lambda p: 3 * n_pairs + p), vt_spec(g, 1),
                  pl.BlockSpec((S, LANES), lambda b, p: (0, 0))],
        out_specs=seq_spec(g, lambda p: p), out_shape=out_shape, compiler_params=params,
        scratch_shapes=pipeline_scratch(g, 2),
        name="moba_attn",
    )(qk, bias, qk, vt, ind)
    return diff, moba


def _out_ffn_kernel(mixa_ref, mixb_ref, x_ref, wo_ref, g_ref, wi_ref, w2_ref, o_ref, *, d_ff, chunk):
    mix = jnp.concatenate([mixa_ref[0], mixb_ref[0]], axis=1)
    x1 = x_ref[0] + jnp.dot(mix, wo_ref[...].astype(BF16), preferred_element_type=F32)
    hn = _rms(x1, g_ref[...]).astype(BF16)
    acc = x1
    for c in range(d_ff // chunk):
        lo = c * chunk
        g = jnp.dot(hn, wi_ref[0, :, lo:lo + chunk].astype(BF16), preferred_element_type=F32)
        u = jnp.dot(hn, wi_ref[0, :, d_ff + lo:d_ff + lo + chunk].astype(BF16), preferred_element_type=F32)
        a = (g * jax.nn.sigmoid(g) * u).astype(BF16)
        acc = acc + jnp.dot(a, w2_ref[0, lo:lo + chunk, :].astype(BF16), preferred_element_type=F32)
    o_ref[0] = acc


def _out_ffn(mix_a, col_a, mix_b, col_b, x, w_out, norm_g, w_in, w2, layer):
    B, S, D = x.shape
    d_ff = w2.shape[1]
    tm = OUT_FFN_ROWS
    half = w_out.shape[0] // 2
    const = lambda b, t: (0, 0)
    resident = lambda shape: pl.BlockSpec(shape, const, pipeline_mode=pl.Buffered(1))
    slab = lambda a: pl.BlockSpec((1,) + a.shape[1:], lambda b, t: (layer, 0, 0), pipeline_mode=pl.Buffered(1))
    tile = pl.BlockSpec((1, tm, D), lambda b, t: (b, t, 0))
    mix_spec = lambda col: pl.BlockSpec((1, tm, half), lambda b, t: (b, t, col))
    return pl.pallas_call(
        functools.partial(_out_ffn_kernel, d_ff=d_ff, chunk=256),
        grid=(B, S // tm),
        in_specs=[mix_spec(col_a), mix_spec(col_b), tile, resident(w_out.shape), pl.BlockSpec((1, D), const),
                  slab(w_in), slab(w2)],
        out_specs=tile,
        out_shape=jax.ShapeDtypeStruct((B, S, D), F32),
        compiler_params=pltpu.CompilerParams(
            dimension_semantics=("arbitrary", "arbitrary"),
            vmem_limit_bytes=VMEM_LIMIT_BYTES),
        name="out_ffn",
    )(mix_a, mix_b, x, w_out, norm_g, w_in, w2)


def _odd_mix_kernel(x_ref, g_ref, w_ref, b_ref, lng_ref, lnb_ref, ws_ref, bs_ref,
                    cw_ref, cb_ref, clg_ref, clb_ref, o_ref, ext_ref, *, tm):
    ti = pl.program_id(1)
    h = _rms(x_ref[0], g_ref[...]).astype(BF16)

    def proj(lo, width):
        return jnp.dot(h, w_ref[:, lo:lo + width], preferred_element_type=F32) + b_ref[:, lo:lo + width]

    gu = jax.nn.gelu(proj(0, GMLP_WIDTH), approximate=True)
    gv = jax.nn.gelu(proj(GMLP_WIDTH, GMLP_WIDTH), approximate=True)
    gv = _layer_norm(gv, lng_ref[...], lnb_ref[...]).astype(BF16)
    T = GMLP_CHUNK
    tri = lax.broadcasted_iota(jnp.int32, (T, T), 1) <= lax.broadcasted_iota(jnp.int32, (T, T), 0)
    lane = lax.broadcasted_iota(jnp.int32, (T, LANES), 1)
    first = lane < (GMLP_WIDTH // GMLP_GROUPS)
    zero = jnp.zeros((T, LANES), BF16)
    for lb in range(GMLP_WIDTH // LANES):
        wpair = jnp.concatenate(
            [jnp.where(tri, ws_ref[2 * lb + i], 0.0).astype(BF16) for i in range(2)], axis=1)
        bs = bs_ref[:, lb * LANES:(lb + 1) * LANES]
        for c in range(tm // T):
            vc = gv[c * T:(c + 1) * T, lb * LANES:(lb + 1) * LANES]
            vpair = jnp.concatenate([jnp.where(first, vc, zero), jnp.where(first, zero, vc)], axis=0)
            sg = jnp.dot(wpair, vpair, preferred_element_type=F32) + bs
            o_ref[0, c * T:(c + 1) * T, lb * LANES:(lb + 1) * LANES] = (
                gu[c * T:(c + 1) * T, lb * LANES:(lb + 1) * LANES] * sg).astype(BF16)

    lo = 2 * GMLP_WIDTH
    cval = proj(lo, CONV_CH) * jax.nn.sigmoid(proj(lo + CONV_CH, CONV_CH))

    @pl.when(ti == 0)
    def _():
        ext_ref[0:CONV_HALO, :] = jnp.zeros((CONV_HALO, CONV_CH), F32)

    ext_ref[CONV_HALO:CONV_HALO + tm, :] = cval
    first_tap = CONV_HALO - (CONV_KERNEL - 1)
    ext = ext_ref[...]
    rows_total = ext.shape[0]
    conv = jnp.zeros((tm, CONV_CH), F32) + cb_ref[...]
    for r in range(8):
        rolled = ext if r == 0 else pltpu.roll(ext, rows_total - r, 0)
        for k in range(CONV_KERNEL):
            if (first_tap + k) % 8 == r:
                off = first_tap + k - r
                conv = conv + rolled[off:off + tm, :] * cw_ref[k:k + 1, :]
    ext_ref[0:CONV_HALO, :] = ext_ref[tm:tm + CONV_HALO, :]
    dn = _layer_norm(conv, clg_ref[...], clb_ref[...])
    o_ref[0, :, GMLP_WIDTH:GMLP_WIDTH + CONV_CH] = (dn * jax.nn.sigmoid(dn)).astype(BF16)


def _odd_mix(x, norm_g, w_in, b_in, ln_g, ln_b, w_s, bs_tile, conv_w, conv_b, cln_g, cln_b):
    B, S, D = x.shape
    tm = ODD_MIX_ROWS
    const2 = lambda b, t: (0, 0)
    full = lambda a: pl.BlockSpec(a.shape, (lambda b, t: (0,) * a.ndim))
    n_out = GMLP_WIDTH + CONV_CH
    span = tm + 8 * ((CONV_KERNEL - 1) // 8)
    args = (x, norm_g, w_in, b_in, ln_g, ln_b, w_s, bs_tile, conv_w, conv_b, cln_g, cln_b)
    return pl.pallas_call(
        functools.partial(_odd_mix_kernel, tm=tm),
        grid=(B, S // tm),
        in_specs=[pl.BlockSpec((1, tm, D), lambda b, t: (b, t, 0))] + [full(a) for a in args[1:]],
        out_specs=pl.BlockSpec((1, tm, n_out), lambda b, t: (b, t, 0)),
        out_shape=jax.ShapeDtypeStruct((B, S, n_out), BF16),
        scratch_shapes=[pltpu.VMEM((span + 8, CONV_CH), F32)],
        compiler_params=pltpu.CompilerParams(
            dimension_semantics=("arbitrary", "arbitrary"),
            vmem_limit_bytes=VMEM_LIMIT_BYTES),
        name="odd_mix",
    )(*args)


def _rope_tables(seq, dim, seg_per_block):
    inv = 1.0 / (ROPE_THETA ** (jnp.arange(0, dim, 2, dtype=F32) / dim))
    ang = jnp.arange(seq, dtype=F32)[:, None] * inv[None, :]
    cos, sin = jnp.cos(ang), jnp.sin(ang)
    cos = jnp.tile(jnp.concatenate([cos, cos], axis=1), (1, seg_per_block))
    sin = jnp.tile(jnp.concatenate([-sin, sin], axis=1), (1, seg_per_block))
    return cos, sin


def _block_diag_ones(seg):
    idx = jnp.arange(BD) // seg
    return (idx[:, None] == idx[None, :]).astype(BF16)


def kernel(x, attn_norm_g, ffn_norm_g, ffn_w_in, ffn_w_out, even_w_in, even_w_out,
           diff_q_norm_g, diff_k_norm_g, diff_lambda_q1, diff_lambda_k1,
           diff_lambda_q2, diff_lambda_k2, diff_subln_g, moba_q_norm_g, moba_k_norm_g,
           odd_w_in, odd_b_in, odd_w_out, gmlp_ln_g, gmlp_ln_b, gmlp_w_s, gmlp_b_s,
           conv_w, conv_b, conv_ln_g, conv_ln_b):
    B, S, D = x.shape
    assert S % max(EVEN_PROJ_ROWS, ODD_MIX_ROWS, OUT_FFN_ROWS) == 0 and S // MOBA_BLOCK <= GATE_SEG
    row = lambda v: v.reshape(1, -1).astype(F32)

    lambda_init = 0.8 - 0.6 * math.exp(-0.3 * 0)
    cosd, sind = _rope_tables(S, DIFF_HALF, LANES // DIFF_HALF)
    cosm, sinm = _rope_tables(S, HEAD_DIM, LANES // HEAD_DIM)
    gains = jnp.stack([
        jnp.tile(diff_q_norm_g[0], SECTION // DIFF_HALF),
        jnp.tile(diff_k_norm_g[0], SECTION // DIFF_HALF),
        jnp.tile(moba_q_norm_g[0], SECTION // HEAD_DIM),
        jnp.tile(moba_k_norm_g[0], SECTION // HEAD_DIM)]).astype(F32)
    qk, vt, bias = _even_proj(x, row(attn_norm_g[0]), even_w_in[0].astype(BF16),
                              _block_diag_ones(DIFF_HALF), _block_diag_ones(HEAD_DIM), gains,
                              cosd, sind, cosm, sinm)
    key_block = jnp.arange(S)[:, None] // MOBA_BLOCK
    ind = (key_block == (jnp.arange(LANES)[None, :] % GATE_SEG)).astype(BF16)
    lam_params = jnp.stack([diff_lambda_q1[0], diff_lambda_k1[0],
                            diff_lambda_q2[0], diff_lambda_k2[0]]).astype(F32)
    subln_tile = jnp.tile(diff_subln_g[0], LANES // HEAD_DIM).reshape(1, LANES).astype(F32)
    def score_bound(d, gq, gk):
        return (d * d ** -0.5 * LOG2E * (1.0 + 2.0 ** -7)
                * jnp.max(jnp.abs(gq.astype(F32))) * jnp.max(jnp.abs(gk.astype(F32))))

    bounded = jnp.logical_and(score_bound(DIFF_HALF, diff_q_norm_g[0], diff_k_norm_g[0]) <= SCORE_BOUND,
                              score_bound(HEAD_DIM, moba_q_norm_g[0], moba_k_norm_g[0]) <= SCORE_BOUND)
    attend = lambda flag: (lambda *a: _attention(*a, lambda_init, flag))
    diff, moba = lax.cond(bounded, attend(True), attend(False), qk, vt, bias, ind, lam_params, subln_tile)
    x = _out_ffn(diff, 0, moba, 0, x, even_w_out[0], row(ffn_norm_g[0]), ffn_w_in, ffn_w_out, 0)

    bs_tile = jnp.repeat(gmlp_b_s[0].T, GMLP_WIDTH // GMLP_GROUPS, axis=1).astype(F32)
    mix = _odd_mix(x, row(attn_norm_g[1]), odd_w_in[0].astype(BF16), row(odd_b_in[0]),
                   row(gmlp_ln_g[0]), row(gmlp_ln_b[0]), gmlp_w_s[0].astype(F32), bs_tile,
                   conv_w[0].astype(F32), row(conv_b[0]), row(conv_ln_g[0]), row(conv_ln_b[0]))
    x = _out_ffn(mix, 0, mix, 1, x, odd_w_out[0], row(ffn_norm_g[1]), ffn_w_in, ffn_w_out, 1)
    return x
```

```python
import functools
import math

import jax
import jax.numpy as jnp
from jax import lax
from jax.experimental import pallas as pl
from jax.experimental.pallas import tpu as pltpu

F32 = jnp.float32
BF16 = jnp.bfloat16

LANES = 128
VMEM_LIMIT_BYTES = 56 * 1024 * 1024

HEAD_DIM = 64
DIFF_HALF = 32
N_HEADS_MOBA = 8
MOBA_BLOCK = 256
MOBA_TOPK = 3
GMLP_CHUNK = 128
GMLP_GROUPS = 8
GMLP_WIDTH = 512
CONV_CH = 512
CONV_KERNEL = 31
ROPE_THETA = 10000.0
EPS = 1e-6

SECTION = 512
BD = 256
EVEN_PROJ_ROWS = 1024
ODD_MIX_ROWS = 1024
OUT_FFN_ROWS = 512
ATTN_STREAMS = 2
MOBA_STREAMS_BOUNDED = 4
SCORE_BOUND = 40.0
ONES_ROWS = 16
GATE_SEG = LANES // N_HEADS_MOBA
CONV_HALO = 32
MASKED_BIAS = -1e30
M_INIT = -1e29
LOG2E = math.log2(math.e)

_NT = (((1,), (1,)), ((), ()))


def _rms(x, g):
    return x * lax.rsqrt(jnp.mean(x * x, axis=-1, keepdims=True) + EPS) * g


def _layer_norm(x, g, b):
    mu = jnp.mean(x, axis=-1, keepdims=True)
    xc = x - mu
    return xc * lax.rsqrt(jnp.mean(xc * xc, axis=-1, keepdims=True) + EPS) * g + b


def _even_proj_kernel(x_ref, g_ref, w_ref, bdd_ref, bdm_ref, gain_ref,
                      cosd_ref, sind_ref, cosm_ref, sinm_ref,
                      qk_ref, vt_ref, bias_ref, kbt_ref, *, tm):
    blocks = tm // MOBA_BLOCK
    first_blk = pl.program_id(1) * blocks
    h = _rms(x_ref[0], g_ref[...]).astype(BF16)

    def qk_section(sec, bd_ref, seg, cos_ref, sin_ref):
        y = jnp.dot(h, w_ref[:, sec * SECTION:(sec + 1) * SECTION], preferred_element_type=F32)
        y2 = (y * y).astype(BF16)
        ss = jnp.concatenate(
            [jnp.dot(y2[:, c * BD:(c + 1) * BD], bd_ref[...], preferred_element_type=F32)
             for c in range(SECTION // BD)], axis=1)
        yn = y * lax.rsqrt(ss * (1.0 / seg) + EPS) * gain_ref[sec_row[sec]:sec_row[sec] + 1, :]
        half = seg // 2
        lower = (lax.broadcasted_iota(jnp.int32, (tm, LANES), 1) & (seg - 1)) < half
        cos = cos_ref[...]
        sin = sin_ref[...]
        out = []
        for c in range(SECTION // LANES):
            yc = yn[:, c * LANES:(c + 1) * LANES]
            rot = jnp.where(lower, pltpu.roll(yc, LANES - half, 1), pltpu.roll(yc, half, 1))
            out.append(yc * cos + rot * sin)
        return out

    def store(out_sec, chunks, scale):
        for c, yc in enumerate(chunks):
            lo = out_sec * SECTION + c * LANES
            qk_ref[0, :, lo:lo + LANES] = (yc * scale).astype(BF16)

    sec_row = {0: 0, 1: 1, 3: 2, 4: 3}
    store(0, qk_section(0, bdd_ref, DIFF_HALF, cosd_ref, sind_ref), DIFF_HALF ** -0.5 * LOG2E)
    store(1, qk_section(1, bdd_ref, DIFF_HALF, cosd_ref, sind_ref), 1.0)
    mq = qk_section(3, bdm_ref, HEAD_DIM, cosm_ref, sinm_ref)
    store(2, mq, HEAD_DIM ** -0.5 * LOG2E)
    mk = qk_section(4, bdm_ref, HEAD_DIM, cosm_ref, sinm_ref)
    store(3, mk, 1.0)
    for out_sec, sec in enumerate((2, 5)):
        lo = sec * SECTION
        vt = jnp.dot(h, w_ref[:, lo:lo + SECTION], preferred_element_type=F32).T.astype(BF16)
        for sb in range(blocks):
            vt_ref[0, sb, out_sec * SECTION:(out_sec + 1) * SECTION, :] = (
                vt[:, sb * MOBA_BLOCK:(sb + 1) * MOBA_BLOCK])

    @pl.when(first_blk == 0)
    def _():
        kbt_ref[...] = jnp.zeros_like(kbt_ref)

    mk_full = jnp.concatenate(mk, axis=1)
    head_of_lane = lax.broadcasted_iota(jnp.int32, (1, SECTION), 1) // HEAD_DIM
    for sb in range(blocks):
        kbar = jnp.mean(mk_full[sb * MOBA_BLOCK:(sb + 1) * MOBA_BLOCK], axis=0, keepdims=True)
        for hh in range(N_HEADS_MOBA):
            kbt_ref[pl.ds(hh * GATE_SEG + first_blk + sb, 1), :] = jnp.where(head_of_lane == hh, kbar, 0.0)

    def split(a):
        hi = a.astype(BF16)
        return hi, (a - hi.astype(F32)).astype(BF16)

    q_hi, q_lo = split(jnp.concatenate(mq, axis=1))
    k_hi, k_lo = split(kbt_ref[...])
    gate = lax.dot_general(jnp.concatenate([k_hi, k_hi, k_lo], axis=1),
                           jnp.concatenate([q_hi, q_lo, q_hi], axis=1), _NT,
                           preferred_element_type=F32)
    blk = lax.broadcasted_iota(jnp.int32, (GATE_SEG, tm), 0)
    blk_f = blk.astype(F32)
    neg_inf = jnp.float32(-jnp.inf)
    own = first_blk + lax.broadcasted_iota(jnp.int32, (GATE_SEG, tm), 1) // MOBA_BLOCK
    bias_rows = []
    for hh in range(N_HEADS_MOBA):
        avail = jnp.where(blk < own, gate[hh * GATE_SEG:(hh + 1) * GATE_SEG], neg_inf)
        chosen = blk == own
        for _ in range(MOBA_TOPK):
            best = jnp.max(avail, axis=0, keepdims=True)
            cand = jnp.logical_and(avail == best, avail > neg_inf)
            first = jnp.min(jnp.where(cand, blk_f, float(GATE_SEG)), axis=0, keepdims=True)
            pick = blk_f == first
            chosen = jnp.logical_or(chosen, pick)
            avail = jnp.where(pick, neg_inf, avail)
        bias_rows.append(jnp.where(chosen, 0.0, MASKED_BIAS))
    bias_ref[0] = jnp.concatenate(bias_rows, axis=0).T.astype(BF16)


def _even_proj(x, norm_g, w_in, bdd, bdm, gains, cosd, sind, cosm, sinm):
    B, S, D = x.shape
    tm = EVEN_PROJ_ROWS
    blocks = tm // MOBA_BLOCK
    n_out = w_in.shape[1]
    n_qk = 4 * SECTION
    n_v = 2 * SECTION
    const = lambda b, t: (0, 0)
    tab = pl.BlockSpec((tm, LANES), lambda b, t: (t, 0))
    return pl.pallas_call(
        functools.partial(_even_proj_kernel, tm=tm),
        grid=(B, S // tm),
        in_specs=[
            pl.BlockSpec((1, tm, D), lambda b, t: (b, t, 0)),
            pl.BlockSpec((1, D), const),
            pl.BlockSpec((D, n_out), const, pipeline_mode=pl.Buffered(1)),
            pl.BlockSpec((BD, BD), const),
            pl.BlockSpec((BD, BD), const),
            pl.BlockSpec((4, SECTION), const),
            tab, tab, tab, tab,
        ],
        out_specs=[
            pl.BlockSpec((1, tm, n_qk), lambda b, t: (b, t, 0)),
            pl.BlockSpec((1, blocks, n_v, MOBA_BLOCK), lambda b, t: (b, t, 0, 0)),
            pl.BlockSpec((1, tm, LANES), lambda b, t: (b, t, 0)),
        ],
        out_shape=[
            jax.ShapeDtypeStruct((B, S, n_qk), BF16),
            jax.ShapeDtypeStruct((B, S // MOBA_BLOCK, n_v, MOBA_BLOCK), BF16),
            jax.ShapeDtypeStruct((B, S, LANES), BF16),
        ],
        scratch_shapes=[pltpu.VMEM((LANES, SECTION), F32)],
        compiler_params=pltpu.CompilerParams(
            dimension_semantics=("arbitrary", "arbitrary"),
            vmem_limit_bytes=VMEM_LIMIT_BYTES),
        name="even_proj",
    )(x, norm_g, w_in, bdd, bdm, gains, cosd, sind, cosm, sinm)


def _flash_head_pair(make_q, n, v_rows, k_ref, vt_ref, s_ref, p_ref, nq, t, emit, k_aug_ref=None):
    streams = k_ref.shape[0]
    per_stream = n // streams

    def scores(q_all, j):
        start = pl.multiple_of(j * t, t)
        out = []
        for g in range(streams):
            k = k_ref[g, pl.ds(start, t), :]
            if k_aug_ref is not None:
                k = jnp.concatenate([k, k_aug_ref[pl.ds(start, t), :]], axis=1)
            out.append(lax.dot_general(k, q_all[g], _NT, preferred_element_type=F32))
        return jnp.concatenate(out, axis=1)

    def softmax(s_all, stats, diagonal):
        ps, new_stats, alphas = [], [], []
        for i in range(n):
            m, l = stats[i]
            s = s_all[:, i * t:(i + 1) * t]
            if diagonal:
                key = lax.broadcasted_iota(jnp.int32, (t, t), 0)
                qry = lax.broadcasted_iota(jnp.int32, (t, t), 1)
                s = jnp.where(key <= qry, s, -jnp.inf)
            m_new = jnp.maximum(m, jnp.max(s, axis=0, keepdims=True))
            p = jnp.exp2(s - m_new)
            alpha = jnp.exp2(m - m_new)
            ps.append(p.astype(BF16))
            new_stats.append((m_new, alpha * l + jnp.sum(p, axis=0, keepdims=True)))
            alphas.append(alpha)
        return tuple(ps), tuple(new_stats), alphas

    def weighted_values(j, ps):
        out = []
        for g in range(streams):
            vt = vt_ref[g, j]
            for li, (lo, hi) in enumerate(v_rows):
                out.append(jnp.dot(vt[lo:hi], ps[g * per_stream + li], preferred_element_type=F32))
        return out

    def step(q_all, j, cur, nxt, carry):
        stats, accs = carry
        s_ref[nxt] = scores(q_all, j + 1)
        p_prev = tuple(p_ref[nxt, i] for i in range(n))
        pv = weighted_values(jnp.maximum(j - 1, 0), p_prev)
        pv = [jnp.where(j > 0, x, 0.0) for x in pv]
        p_cur, stats, alphas = softmax(s_ref[cur], stats, False)
        for i in range(n):
            p_ref[cur, i] = p_cur[i]
        accs = tuple(a * (acc + x) for a, acc, x in zip(alphas, accs, pv))
        return stats, accs

    def query_tile(qi, first, second, odd, next_first):
        q_all = make_q(qi)
        init = (tuple((jnp.full((1, t), M_INIT, F32), jnp.zeros((1, t), F32)) for _ in range(n)),
                tuple(jnp.zeros((hi - lo, t), F32) for _ in range(streams) for lo, hi in v_rows))
        carry = lax.fori_loop(
            0, qi // 2,
            lambda i, c: step(q_all, 2 * i + 1, second, first, step(q_all, 2 * i, first, second, c)), init)
        if odd:
            carry = step(q_all, qi - 1, first, second, carry)
        diag, prev = (second, first) if odd else (first, second)
        stats, accs = carry
        s_ref[next_first] = scores(make_q(jnp.minimum(qi + 1, nq - 1)), 0)
        pv_prev = weighted_values(jnp.maximum(qi - 1, 0), tuple(p_ref[prev, i] for i in range(n)))
        pv_prev = [jnp.where(qi > 0, x, 0.0) for x in pv_prev]
        p_cur, stats, alphas = softmax(s_ref[diag], stats, True)
        pv_cur = weighted_values(qi, p_cur)
        emit(qi, [(a * (acc + x) + y, l)
                  for a, acc, x, y, (_, l) in zip(alphas, accs, pv_prev, pv_cur, stats)])

    p_ref[...] = jnp.zeros(p_ref.shape, BF16)
    s_ref[0] = scores(make_q(0), 0)

    def pair_of_tiles(u, _):
        query_tile(2 * u, 0, 1, False, 2)
        query_tile(2 * u + 1, 2, 1, True, 0)
        return 0

    lax.fori_loop(0, nq // 2, pair_of_tiles, 0)


def _flash_bounded(make_q, n, v_rows, k_ref, vt_ref, s_ref, nq, t, emit, group, k_aug_ref=None):
    streams = k_ref.shape[0]
    per_stream = n // streams
    ones = jnp.ones((ONES_ROWS, t), BF16)
    width = per_stream * t
    key = lax.broadcasted_iota(jnp.int32, (t, width), 0)
    qry = lax.broadcasted_iota(jnp.int32, (t, width), 1) & (t - 1)

    def scores(q_all, j, slot, only=None):
        start = pl.multiple_of(j * t, t)
        for g in (range(streams) if only is None else only):
            k = k_ref[g, pl.ds(start, t), :]
            if k_aug_ref is not None:
                k = jnp.concatenate([k, k_aug_ref[pl.ds(start, t), :]], axis=1)
            s_ref[slot, g] = lax.dot_general(k, q_all[g], _NT, preferred_element_type=F32)

    def accumulate(j, slot, accs, diagonal, only=None):
        out = list(accs)
        for g in (range(streams) if only is None else only):
            p = jnp.exp2(s_ref[slot, g])
            if diagonal:
                p = jnp.where(key <= qry, p, 0.0)
            p = p.astype(BF16)
            vt = vt_ref[g, j]
            for li, (lo, hi) in enumerate(v_rows):
                lhs = jnp.concatenate([vt[lo:hi], ones], axis=0)
                i = g * per_stream + li
                out[i] = accs[i] + jnp.dot(lhs, p[:, li * t:(li + 1) * t], preferred_element_type=F32)
        return tuple(out)

    def issue(q_src, j_src, dst, j_acc, src, accs, diagonal):
        for g0 in range(0, streams, group):
            members = range(g0, min(g0 + group, streams))
            scores(q_src, j_src, dst, only=members)
            accs = accumulate(j_acc, src, accs, diagonal, only=members)
        return accs

    def step(q_all, j, cur, nxt, accs):
        return issue(q_all, j + 1, nxt, j, cur, accs, False)

    rows = [hi - lo for _ in range(streams) for lo, hi in v_rows]

    def query_tile(u, first, second, odd, next_first):
        qi = 2 * u + odd
        q_all = make_q(qi)

        def four_steps(i, a):
            j = 4 * i
            a = step(q_all, j, first, second, a)
            a = step(q_all, j + 1, second, first, a)
            a = step(q_all, j + 2, first, second, a)
            return step(q_all, j + 3, second, first, a)

        init = tuple(jnp.zeros((r + ONES_ROWS, t), F32) for r in rows)
        accs = lax.fori_loop(0, qi // 4, four_steps, init)

        def tail(accs, two_more):
            j = 4 * (qi // 4)
            if two_more:
                accs = step(q_all, j, first, second, accs)
                accs = step(q_all, j + 1, second, first, accs)
                j = j + 2
            if odd:
                accs = step(q_all, j, first, second, accs)
            accs = issue(make_q(jnp.minimum(qi + 1, nq - 1)), 0, next_first,
                         qi, second if odd else first, accs, True)
            emit(qi, [(a[:r], a[r:r + 1]) for a, r in zip(accs, rows)])

        pl.when((u & 1) == 0)(lambda: tail(accs, False))
        pl.when((u & 1) == 1)(lambda: tail(accs, True))

    scores(make_q(0), 0, 0)

    def pair_of_tiles(u, _):
        query_tile(u, 0, 1, 0, 2)
        query_tile(u, 2, 1, 1, 0)
        return 0

    lax.fori_loop(0, nq // 2, pair_of_tiles, 0)


def _diff_attn_kernel(lam_ref, q_ref, k_ref, vt_ref, sg_ref, o_ref, *scratch, t, lambda_init, bounded):
    nq = q_ref.shape[1] // t
    seg = lax.broadcasted_iota(jnp.int32, (t, LANES), 1) // DIFF_HALF
    lam_p = lam_ref[...]
    lam = (jnp.exp(jnp.sum(lam_p[0:1] * lam_p[1:2], axis=1, keepdims=True))
           - jnp.exp(jnp.sum(lam_p[2:3] * lam_p[3:4], axis=1, keepdims=True)) + lambda_init)

    streams = q_ref.shape[0]

    def make_q(qi):
        out = []
        for g in range(streams):
            q = q_ref[g, pl.ds(pl.multiple_of(qi * t, t), t), :]
            out.append(jnp.concatenate([jnp.where(seg == i, q, jnp.zeros_like(q)) for i in range(4)], axis=0))
        return out

    def emit(qi, res):
        a = [acc * (1.0 / l) for acc, l in res]
        for g in range(streams):
            heads = []
            for hh in range(2):
                o = a[4 * g + 2 * hh] - lam * a[4 * g + 2 * hh + 1]
                heads.append(o * lax.rsqrt(jnp.mean(o * o, axis=0, keepdims=True) + EPS))
            o = jnp.concatenate(heads, axis=0).T
            o_ref[g, pl.ds(pl.multiple_of(qi * t, t), t), :] = (
                o * sg_ref[...] * (1.0 - lambda_init)).astype(BF16)

    v_rows = [(HEAD_DIM * (i // 2), HEAD_DIM * (i // 2 + 1)) for i in range(4)]
    if bounded:
        _flash_bounded(make_q, 4 * streams, v_rows, k_ref, vt_ref, *scratch, nq, t, emit, 1)
    else:
        _flash_head_pair(make_q, 4 * streams, v_rows, k_ref, vt_ref, *scratch, nq, t, emit)


def _moba_attn_kernel(q_ref, bias_ref, k_ref, vt_ref, ind_ref, o_ref, *scratch, t, bounded):
    hp = pl.program_id(1)
    nq = q_ref.shape[1] // t
    lane = lax.broadcasted_iota(jnp.int32, (t, LANES), 1)

    streams = q_ref.shape[0]

    def make_q(qi):
        rows = pl.ds(pl.multiple_of(qi * t, t), t)
        out = []
        for g in range(streams):
            q = q_ref[g, rows, :]
            bias = bias_ref[g, rows, :]
            strips = []
            for hh in range(2):
                qz = jnp.where(lane // HEAD_DIM == hh, q, jnp.zeros_like(q))
                bz = jnp.where(lane // GATE_SEG == 2 * hp + hh, bias, jnp.zeros_like(bias))
                strips.append(jnp.concatenate([qz, bz], axis=1))
            out.append(jnp.concatenate(strips, axis=0))
        return out

    def emit(qi, res):
        for g in range(streams):
            o = jnp.concatenate([acc * (1.0 / l) for acc, l in res[2 * g:2 * g + 2]], axis=0)
            o_ref[g, pl.ds(pl.multiple_of(qi * t, t), t), :] = o.T.astype(BF16)

    v_rows = [(0, HEAD_DIM), (HEAD_DIM, 2 * HEAD_DIM)]
    if bounded:
        _flash_bounded(make_q, 2 * streams, v_rows, k_ref, vt_ref, *scratch, nq, t, emit, streams,
                       k_aug_ref=ind_ref)
    else:
        _flash_head_pair(make_q, 2 * streams, v_rows, k_ref, vt_ref, *scratch, nq, t, emit, k_aug_ref=ind_ref)


def _attention(qk, vt, bias, ind, lam_params, subln_tile, lambda_init, bounded):
    B, S, _ = qk.shape
    t = MOBA_BLOCK
    n_pairs = SECTION // LANES
    assert (S // t) % 2 == 0
    params = pltpu.CompilerParams(
        dimension_semantics=("arbitrary", "arbitrary"),
        vmem_limit_bytes=VMEM_LIMIT_BYTES)
    out_shape = jax.ShapeDtypeStruct((B, S, SECTION), BF16)

    def streams(wanted):
        return wanted if B % wanted == 0 else 1

    def seq_spec(g, col):
        return pl.BlockSpec((g, S, LANES), lambda b, p: (b, 0, col(p)))

    def vt_spec(g, sec):
        return pl.BlockSpec((g, S // t, LANES, t), lambda b, p: (b, 0, sec * n_pairs + p, 0))

    def pipeline_scratch(g, n):
        if bounded:
            return [pltpu.VMEM((3, g, t, n * t), F32)]
        return [pltpu.VMEM((3, t, g * n * t), F32), pltpu.VMEM((3, g * n, t, t), BF16)]

    g = streams(ATTN_STREAMS)
    diff = pl.pallas_call(
        functools.partial(_diff_attn_kernel, t=t, lambda_init=lambda_init, bounded=bounded),
        grid=(B // g, n_pairs),
        in_specs=[pl.BlockSpec((4, DIFF_HALF), lambda b, p: (0, 0)),
                  seq_spec(g, lambda p: p), seq_spec(g, lambda p: n_pairs + p), vt_spec(g, 0),
                  pl.BlockSpec((1, LANES), lambda b, p: (0, 0))],
        out_specs=seq_spec(g, lambda p: p), out_shape=out_shape, compiler_params=params,
        scratch_shapes=pipeline_scratch(g, 4),
        name="diff_attn",
    )(lam_params, qk, qk, vt, subln_tile)

    g = streams(MOBA_STREAMS_BOUNDED if bounded else ATTN_STREAMS)
    moba = pl.pallas_call(
        functools.partial(_moba_attn_kernel, t=t, bounded=bounded),
        grid=(B // g, n_pairs),
        in_specs=[seq_spec(g, lambda p: 2 * n_pairs + p),
                  seq_spec(g, lambda p: 0),
                  seq_spec(g, lambda p: 3 * n_pairs + p), vt_spec(g, 1),
                  pl.BlockSpec((S, LANES), lambda b, p: (0, 0))],
        out_specs=seq_spec(g, lambda p: p), out_shape=out_shape, compiler_params=params,
        scratch_shapes=pipeline_scratch(g, 2),
        name="moba_attn",
    )(qk, bias, qk, vt, ind)
    return diff, moba


def _out_ffn_kernel(mixa_ref, mixb_ref, x_ref, wo_ref, g_ref, wi_ref, w2_ref, o_ref, *, d_ff, chunk):
    mix = jnp.concatenate([mixa_ref[0], mixb_ref[0]], axis=1)
    x1 = x_ref[0] + jnp.dot(mix, wo_ref[...].astype(BF16), preferred_element_type=F32)
    hn = _rms(x1, g_ref[...]).astype(BF16)
    acc = x1
    for c in range(d_ff // chunk):
        lo = c * chunk
        g = jnp.dot(hn, wi_ref[0, :, lo:lo + chunk].astype(BF16), preferred_element_type=F32)
        u = jnp.dot(hn, wi_ref[0, :, d_ff + lo:d_ff + lo + chunk].astype(BF16), preferred_element_type=F32)
        a = (g * jax.nn.sigmoid(g) * u).astype(BF16)
        acc = acc + jnp.dot(a, w2_ref[0, lo:lo + chunk, :].astype(BF16), preferred_element_type=F32)
    o_ref[0] = acc


def _out_ffn(mix_a, col_a, mix_b, col_b, x, w_out, norm_g, w_in, w2, layer):
    B, S, D = x.shape
    d_ff = w2.shape[1]
    tm = OUT_FFN_ROWS
    half = w_out.shape[0] // 2
    const = lambda b, t: (0, 0)
    resident = lambda shape: pl.BlockSpec(shape, const, pipeline_mode=pl.Buffered(1))
    slab = lambda a: pl.BlockSpec((1,) + a.shape[1:], lambda b, t: (layer, 0, 0), pipeline_mode=pl.Buffered(1))
    tile = pl.BlockSpec((1, tm, D), lambda b, t: (b, t, 0))
    mix_spec = lambda col: pl.BlockSpec((1, tm, half), lambda b, t: (b, t, col))
    return pl.pallas_call(
        functools.partial(_out_ffn_kernel, d_ff=d_ff, chunk=256),
        grid=(B, S // tm),
        in_specs=[mix_spec(col_a), mix_spec(col_b), tile, resident(w_out.shape), pl.BlockSpec((1, D), const),
                  slab(w_in), slab(w2)],
        out_specs=tile,
        out_shape=jax.ShapeDtypeStruct((B, S, D), F32),
        compiler_params=pltpu.CompilerParams(
            dimension_semantics=("arbitrary", "arbitrary"),
            vmem_limit_bytes=VMEM_LIMIT_BYTES),
        name="out_ffn",
    )(mix_a, mix_b, x, w_out, norm_g, w_in, w2)


def _odd_mix_kernel(x_ref, g_ref, w_ref, b_ref, lng_ref, lnb_ref, ws_ref, bs_ref,
                    cw_ref, cb_ref, clg_ref, clb_ref, o_ref, ext_ref, *, tm):
    ti = pl.program_id(1)
    h = _rms(x_ref[0], g_ref[...]).astype(BF16)

    def proj(lo, width):
        return jnp.dot(h, w_ref[:, lo:lo + width], preferred_element_type=F32) + b_ref[:, lo:lo + width]

    gu = jax.nn.gelu(proj(0, GMLP_WIDTH), approximate=True)
    gv = jax.nn.gelu(proj(GMLP_WIDTH, GMLP_WIDTH), approximate=True)
    gv = _layer_norm(gv, lng_ref[...], lnb_ref[...]).astype(BF16)
    T = GMLP_CHUNK
    tri = lax.broadcasted_iota(jnp.int32, (T, T), 1) <= lax.broadcasted_iota(jnp.int32, (T, T), 0)
    lane = lax.broadcasted_iota(jnp.int32, (T, LANES), 1)
    first = lane < (GMLP_WIDTH // GMLP_GROUPS)
    zero = jnp.zeros((T, LANES), BF16)
    for lb in range(GMLP_WIDTH // LANES):
        wpair = jnp.concatenate(
            [jnp.where(tri, ws_ref[2 * lb + i], 0.0).astype(BF16) for i in range(2)], axis=1)
        bs = bs_ref[:, lb * LANES:(lb + 1) * LANES]
        for c in range(tm // T):
            vc = gv[c * T:(c + 1) * T, lb * LANES:(lb + 1) * LANES]
            vpair = jnp.concatenate([jnp.where(first, vc, zero), jnp.where(first, zero, vc)], axis=0)
            sg = jnp.dot(wpair, vpair, preferred_element_type=F32) + bs
            o_ref[0, c * T:(c + 1) * T, lb * LANES:(lb + 1) * LANES] = (
                gu[c * T:(c + 1) * T, lb * LANES:(lb + 1) * LANES] * sg).astype(BF16)

    lo = 2 * GMLP_WIDTH
    cval = proj(lo, CONV_CH) * jax.nn.sigmoid(proj(lo + CONV_CH, CONV_CH))

    @pl.when(ti == 0)
    def _():
        ext_ref[0:CONV_HALO, :] = jnp.zeros((CONV_HALO, CONV_CH), F32)

    ext_ref[CONV_HALO:CONV_HALO + tm, :] = cval
    first_tap = CONV_HALO - (CONV_KERNEL - 1)
    ext = ext_ref[...]
    rows_total = ext.shape[0]
    conv = jnp.zeros((tm, CONV_CH), F32) + cb_ref[...]
    for r in range(8):
        rolled = ext if r == 0 else pltpu.roll(ext, rows_total - r, 0)
        for k in range(CONV_KERNEL):
            if (first_tap + k) % 8 == r:
                off = first_tap + k - r
                conv = conv + rolled[off:off + tm, :] * cw_ref[k:k + 1, :]
    ext_ref[0:CONV_HALO, :] = ext_ref[tm:tm + CONV_HALO, :]
    dn = _layer_norm(conv, clg_ref[...], clb_ref[...])
    o_ref[0, :, GMLP_WIDTH:GMLP_WIDTH + CONV_CH] = (dn * jax.nn.sigmoid(dn)).astype(BF16)


def _odd_mix(x, norm_g, w_in, b_in, ln_g, ln_b, w_s, bs_tile, conv_w, conv_b, cln_g, cln_b):
    B, S, D = x.shape
    tm = ODD_MIX_ROWS
    const2 = lambda b, t: (0, 0)
    full = lambda a: pl.BlockSpec(a.shape, (lambda b, t: (0,) * a.ndim))
    n_out = GMLP_WIDTH + CONV_CH
    span = tm + 8 * ((CONV_KERNEL - 1) // 8)
    args = (x, norm_g, w_in, b_in, ln_g, ln_b, w_s, bs_tile, conv_w, conv_b, cln_g, cln_b)
    return pl.pallas_call(
        functools.partial(_odd_mix_kernel, tm=tm),
        grid=(B, S // tm),
        in_specs=[pl.BlockSpec((1, tm, D), lambda b, t: (b, t, 0))] + [full(a) for a in args[1:]],
        out_specs=pl.BlockSpec((1, tm, n_out), lambda b, t: (b, t, 0)),
        out_shape=jax.ShapeDtypeStruct((B, S, n_out), BF16),
        scratch_shapes=[pltpu.VMEM((span + 8, CONV_CH), F32)],
        compiler_params=pltpu.CompilerParams(
            dimension_semantics=("arbitrary", "arbitrary"),
            vmem_limit_bytes=VMEM_LIMIT_BYTES),
        name="odd_mix",
    )(*args)


def _rope_tables(seq, dim, seg_per_block):
    inv = 1.0 / (ROPE_THETA ** (jnp.arange(0, dim, 2, dtype=F32) / dim))
    ang = jnp.arange(seq, dtype=F32)[:, None] * inv[None, :]
    cos, sin = jnp.cos(ang), jnp.sin(ang)
    cos = jnp.tile(jnp.concatenate([cos, cos], axis=1), (1, seg_per_block))
    sin = jnp.tile(jnp.concatenate([-sin, sin], axis=1), (1, seg_per_block))
    return cos, sin


def _block_diag_ones(seg):
    idx = jnp.arange(BD) // seg
    return (idx[:, None] == idx[None, :]).astype(BF16)


def kernel(x, attn_norm_g, ffn_norm_g, ffn_w_in, ffn_w_out, even_w_in, even_w_out,
           diff_q_norm_g, diff_k_norm_g, diff_lambda_q1, diff_lambda_k1,
           diff_lambda_q2, diff_lambda_k2, diff_subln_g, moba_q_norm_g, moba_k_norm_g,
           odd_w_in, odd_b_in, odd_w_out, gmlp_ln_g, gmlp_ln_b, gmlp_w_s, gmlp_b_s,
           conv_w, conv_b, conv_ln_g, conv_ln_b):
    B, S, D = x.shape
    assert S % max(EVEN_PROJ_ROWS, ODD_MIX_ROWS, OUT_FFN_ROWS) == 0 and S // MOBA_BLOCK <= GATE_SEG
    row = lambda v: v.reshape(1, -1).astype(F32)

    lambda_init = 0.8 - 0.6 * math.exp(-0.3 * 0)
    cosd, sind = _rope_tables(S, DIFF_HALF, LANES // DIFF_HALF)
    cosm, sinm = _rope_tables(S, HEAD_DIM, LANES // HEAD_DIM)
    gains = jnp.stack([
        jnp.tile(diff_q_norm_g[0], SECTION // DIFF_HALF),
        jnp.tile(diff_k_norm_g[0], SECTION // DIFF_HALF),
        jnp.tile(moba_q_norm_g[0], SECTION // HEAD_DIM),
        jnp.tile(moba_k_norm_g[0], SECTION // HEAD_DIM)]).astype(F32)
    qk, vt, bias = _even_proj(x, row(attn_norm_g[0]), even_w_in[0].astype(BF16),
                              _block_diag_ones(DIFF_HALF), _block_diag_ones(HEAD_DIM), gains,
                              cosd, sind, cosm, sinm)
    key_block = jnp.arange(S)[:, None] // MOBA_BLOCK
    ind = (key_block == (jnp.arange(LANES)[None, :] % GATE_SEG)).astype(BF16)
    lam_params = jnp.stack([diff_lambda_q1[0], diff_lambda_k1[0],
                            diff_lambda_q2[0], diff_lambda_k2[0]]).astype(F32)
    subln_tile = jnp.tile(diff_subln_g[0], LANES // HEAD_DIM).reshape(1, LANES).astype(F32)
    def score_bound(d, gq, gk):
        return (d * d ** -0.5 * LOG2E * (1.0 + 2.0 ** -7)
                * jnp.max(jnp.abs(gq.astype(F32))) * jnp.max(jnp.abs(gk.astype(F32))))

    bounded = jnp.logical_and(score_bound(DIFF_HALF, diff_q_norm_g[0], diff_k_norm_g[0]) <= SCORE_BOUND,
                              score_bound(HEAD_DIM, moba_q_norm_g[0], moba_k_norm_g[0]) <= SCORE_BOUND)
    attend = lambda flag: (lambda *a: _attention(*a, lambda_init, flag))
    diff, moba = lax.cond(bounded, attend(True), attend(False), qk, vt, bias, ind, lam_params, subln_tile)
    x = _out_ffn(diff, 0, moba, 0, x, even_w_out[0], row(ffn_norm_g[0]), ffn_w_in, ffn_w_out, 0)

    bs_tile = jnp.repeat(gmlp_b_s[0].T, GMLP_WIDTH // GMLP_GROUPS, axis=1).astype(F32)
    mix = _odd_mix(x, row(attn_norm_g[1]), odd_w_in[0].astype(BF16), row(odd_b_in[0]),
                   row(gmlp_ln_g[0]), row(gmlp_ln_b[0]), gmlp_w_s[0].astype(F32), bs_tile,
                   conv_w[0].astype(F32), row(conv_b[0]), row(conv_ln_g[0]), row(conv_ln_b[0]))
    x = _out_ffn(mix, 0, mix, 1, x, odd_w_out[0], row(ffn_norm_g[1]), ffn_w_in, ffn_w_out, 1)
    return x
```

```python
import functools
import math

import jax
import jax.numpy as jnp
from jax import lax
from jax.experimental import pallas as pl
from jax.experimental.pallas import tpu as pltpu

F32 = jnp.float32
BF16 = jnp.bfloat16

LANES = 128
VMEM_LIMIT_BYTES = 56 * 1024 * 1024

HEAD_DIM = 64
DIFF_HALF = 32
N_HEADS_MOBA = 8
MOBA_BLOCK = 256
MOBA_TOPK = 3
GMLP_CHUNK = 128
GMLP_GROUPS = 8
GMLP_WIDTH = 512
CONV_CH = 512
CONV_KERNEL = 31
ROPE_THETA = 10000.0
EPS = 1e-6

SECTION = 512
BD = 256
EVEN_PROJ_ROWS = 1024
ODD_MIX_ROWS = 1024
OUT_FFN_ROWS = 512
ATTN_STREAMS = 2
BOUNDED_STREAMS = 4
SCORE_BOUND = 40.0
ONES_ROWS = 16
GATE_SEG = LANES // N_HEADS_MOBA
CONV_HALO = 32
MASKED_BIAS = -1e30
M_INIT = -1e29
LOG2E = math.log2(math.e)

_NT = (((1,), (1,)), ((), ()))


def _rms(x, g):
    return x * lax.rsqrt(jnp.mean(x * x, axis=-1, keepdims=True) + EPS) * g


def _layer_norm(x, g, b):
    mu = jnp.mean(x, axis=-1, keepdims=True)
    xc = x - mu
    return xc * lax.rsqrt(jnp.mean(xc * xc, axis=-1, keepdims=True) + EPS) * g + b


def _even_proj_kernel(x_ref, g_ref, w_ref, bdd_ref, bdm_ref, gain_ref,
                      cosd_ref, sind_ref, cosm_ref, sinm_ref,
                      qk_ref, vt_ref, bias_ref, kbt_ref, *, tm):
    blocks = tm // MOBA_BLOCK
    first_blk = pl.program_id(1) * blocks
    h = _rms(x_ref[0], g_ref[...]).astype(BF16)

    def qk_section(sec, bd_ref, seg, cos_ref, sin_ref):
        y = jnp.dot(h, w_ref[:, sec * SECTION:(sec + 1) * SECTION], preferred_element_type=F32)
        y2 = (y * y).astype(BF16)
        ss = jnp.concatenate(
            [jnp.dot(y2[:, c * BD:(c + 1) * BD], bd_ref[...], preferred_element_type=F32)
             for c in range(SECTION // BD)], axis=1)
        yn = y * lax.rsqrt(ss * (1.0 / seg) + EPS) * gain_ref[sec_row[sec]:sec_row[sec] + 1, :]
        half = seg // 2
        lower = (lax.broadcasted_iota(jnp.int32, (tm, LANES), 1) & (seg - 1)) < half
        cos = cos_ref[...]
        sin = sin_ref[...]
        out = []
        for c in range(SECTION // LANES):
            yc = yn[:, c * LANES:(c + 1) * LANES]
            rot = jnp.where(lower, pltpu.roll(yc, LANES - half, 1), pltpu.roll(yc, half, 1))
            out.append(yc * cos + rot * sin)
        return out

    def store(out_sec, chunks, scale):
        for c, yc in enumerate(chunks):
            lo = out_sec * SECTION + c * LANES
            qk_ref[0, :, lo:lo + LANES] = (yc * scale).astype(BF16)

    sec_row = {0: 0, 1: 1, 3: 2, 4: 3}
    store(0, qk_section(0, bdd_ref, DIFF_HALF, cosd_ref, sind_ref), DIFF_HALF ** -0.5 * LOG2E)
    store(1, qk_section(1, bdd_ref, DIFF_HALF, cosd_ref, sind_ref), 1.0)
    mq = qk_section(3, bdm_ref, HEAD_DIM, cosm_ref, sinm_ref)
    store(2, mq, HEAD_DIM ** -0.5 * LOG2E)
    mk = qk_section(4, bdm_ref, HEAD_DIM, cosm_ref, sinm_ref)
    store(3, mk, 1.0)
    for out_sec, sec in enumerate((2, 5)):
        lo = sec * SECTION
        vt = jnp.dot(h, w_ref[:, lo:lo + SECTION], preferred_element_type=F32).T.astype(BF16)
        for sb in range(blocks):
            vt_ref[0, sb, out_sec * SECTION:(out_sec + 1) * SECTION, :] = (
                vt[:, sb * MOBA_BLOCK:(sb + 1) * MOBA_BLOCK])

    @pl.when(first_blk == 0)
    def _():
        kbt_ref[...] = jnp.zeros_like(kbt_ref)

    mk_full = jnp.concatenate(mk, axis=1)
    head_of_lane = lax.broadcasted_iota(jnp.int32, (1, SECTION), 1) // HEAD_DIM
    for sb in range(blocks):
        kbar = jnp.mean(mk_full[sb * MOBA_BLOCK:(sb + 1) * MOBA_BLOCK], axis=0, keepdims=True)
        for hh in range(N_HEADS_MOBA):
            kbt_ref[pl.ds(hh * GATE_SEG + first_blk + sb, 1), :] = jnp.where(head_of_lane == hh, kbar, 0.0)

    def split(a):
        hi = a.astype(BF16)
        return hi, (a - hi.astype(F32)).astype(BF16)

    q_hi, q_lo = split(jnp.concatenate(mq, axis=1))
    k_hi, k_lo = split(kbt_ref[...])
    gate = lax.dot_general(jnp.concatenate([k_hi, k_hi, k_lo], axis=1),
                           jnp.concatenate([q_hi, q_lo, q_hi], axis=1), _NT,
                           preferred_element_type=F32)
    blk = lax.broadcasted_iota(jnp.int32, (GATE_SEG, tm), 0)
    blk_f = blk.astype(F32)
    neg_inf = jnp.float32(-jnp.inf)
    own = first_blk + lax.broadcasted_iota(jnp.int32, (GATE_SEG, tm), 1) // MOBA_BLOCK
    bias_rows = []
    for hh in range(N_HEADS_MOBA):
        avail = jnp.where(blk < own, gate[hh * GATE_SEG:(hh + 1) * GATE_SEG], neg_inf)
        chosen = blk == own
        for _ in range(MOBA_TOPK):
            best = jnp.max(avail, axis=0, keepdims=True)
            cand = jnp.logical_and(avail == best, avail > neg_inf)
            first = jnp.min(jnp.where(cand, blk_f, float(GATE_SEG)), axis=0, keepdims=True)
            pick = blk_f == first
            chosen = jnp.logical_or(chosen, pick)
            avail = jnp.where(pick, neg_inf, avail)
        bias_rows.append(jnp.where(chosen, 0.0, MASKED_BIAS))
    bias_ref[0] = jnp.concatenate(bias_rows, axis=0).T.astype(BF16)


def _even_proj(x, norm_g, w_in, bdd, bdm, gains, cosd, sind, cosm, sinm):
    B, S, D = x.shape
    tm = EVEN_PROJ_ROWS
    blocks = tm // MOBA_BLOCK
    n_out = w_in.shape[1]
    n_qk = 4 * SECTION
    n_v = 2 * SECTION
    const = lambda b, t: (0, 0)
    tab = pl.BlockSpec((tm, LANES), lambda b, t: (t, 0))
    return pl.pallas_call(
        functools.partial(_even_proj_kernel, tm=tm),
        grid=(B, S // tm),
        in_specs=[
            pl.BlockSpec((1, tm, D), lambda b, t: (b, t, 0)),
            pl.BlockSpec((1, D), const),
            pl.BlockSpec((D, n_out), const, pipeline_mode=pl.Buffered(1)),
            pl.BlockSpec((BD, BD), const),
            pl.BlockSpec((BD, BD), const),
            pl.BlockSpec((4, SECTION), const),
            tab, tab, tab, tab,
        ],
        out_specs=[
            pl.BlockSpec((1, tm, n_qk), lambda b, t: (b, t, 0)),
            pl.BlockSpec((1, blocks, n_v, MOBA_BLOCK), lambda b, t: (b, t, 0, 0)),
            pl.BlockSpec((1, tm, LANES), lambda b, t: (b, t, 0)),
        ],
        out_shape=[
            jax.ShapeDtypeStruct((B, S, n_qk), BF16),
            jax.ShapeDtypeStruct((B, S // MOBA_BLOCK, n_v, MOBA_BLOCK), BF16),
            jax.ShapeDtypeStruct((B, S, LANES), BF16),
        ],
        scratch_shapes=[pltpu.VMEM((LANES, SECTION), F32)],
        compiler_params=pltpu.CompilerParams(
            dimension_semantics=("arbitrary", "arbitrary"),
            vmem_limit_bytes=VMEM_LIMIT_BYTES),
        name="even_proj",
    )(x, norm_g, w_in, bdd, bdm, gains, cosd, sind, cosm, sinm)


def _flash_head_pair(make_q, n, v_rows, k_ref, vt_ref, s_ref, p_ref, nq, t, emit, k_aug_ref=None):
    streams = k_ref.shape[0]
    per_stream = n // streams

    def scores(q_all, j):
        start = pl.multiple_of(j * t, t)
        out = []
        for g in range(streams):
            k = k_ref[g, pl.ds(start, t), :]
            if k_aug_ref is not None:
                k = jnp.concatenate([k, k_aug_ref[pl.ds(start, t), :]], axis=1)
            out.append(lax.dot_general(k, q_all[g], _NT, preferred_element_type=F32))
        return jnp.concatenate(out, axis=1)

    def softmax(s_all, stats, diagonal):
        ps, new_stats, alphas = [], [], []
        for i in range(n):
            m, l = stats[i]
            s = s_all[:, i * t:(i + 1) * t]
            if diagonal:
                key = lax.broadcasted_iota(jnp.int32, (t, t), 0)
                qry = lax.broadcasted_iota(jnp.int32, (t, t), 1)
                s = jnp.where(key <= qry, s, -jnp.inf)
            m_new = jnp.maximum(m, jnp.max(s, axis=0, keepdims=True))
            p = jnp.exp2(s - m_new)
            alpha = jnp.exp2(m - m_new)
            ps.append(p.astype(BF16))
            new_stats.append((m_new, alpha * l + jnp.sum(p, axis=0, keepdims=True)))
            alphas.append(alpha)
        return tuple(ps), tuple(new_stats), alphas

    def weighted_values(j, ps):
        out = []
        for g in range(streams):
            vt = vt_ref[g, j]
            for li, (lo, hi) in enumerate(v_rows):
                out.append(jnp.dot(vt[lo:hi], ps[g * per_stream + li], preferred_element_type=F32))
        return out

    def step(q_all, j, cur, nxt, carry):
        stats, accs = carry
        s_ref[nxt] = scores(q_all, j + 1)
        p_prev = tuple(p_ref[nxt, i] for i in range(n))
        pv = weighted_values(jnp.maximum(j - 1, 0), p_prev)
        pv = [jnp.where(j > 0, x, 0.0) for x in pv]
        p_cur, stats, alphas = softmax(s_ref[cur], stats, False)
        for i in range(n):
            p_ref[cur, i] = p_cur[i]
        accs = tuple(a * (acc + x) for a, acc, x in zip(alphas, accs, pv))
        return stats, accs

    def query_tile(qi, first, second, odd, next_first):
        q_all = make_q(qi)
        init = (tuple((jnp.full((1, t), M_INIT, F32), jnp.zeros((1, t), F32)) for _ in range(n)),
                tuple(jnp.zeros((hi - lo, t), F32) for _ in range(streams) for lo, hi in v_rows))
        carry = lax.fori_loop(
            0, qi // 2,
            lambda i, c: step(q_all, 2 * i + 1, second, first, step(q_all, 2 * i, first, second, c)), init)
        if odd:
            carry = step(q_all, qi - 1, first, second, carry)
        diag, prev = (second, first) if odd else (first, second)
        stats, accs = carry
        s_ref[next_first] = scores(make_q(jnp.minimum(qi + 1, nq - 1)), 0)
        pv_prev = weighted_values(jnp.maximum(qi - 1, 0), tuple(p_ref[prev, i] for i in range(n)))
        pv_prev = [jnp.where(qi > 0, x, 0.0) for x in pv_prev]
        p_cur, stats, alphas = softmax(s_ref[diag], stats, True)
        pv_cur = weighted_values(qi, p_cur)
        emit(qi, [(a * (acc + x) + y, l)
                  for a, acc, x, y, (_, l) in zip(alphas, accs, pv_prev, pv_cur, stats)])

    p_ref[...] = jnp.zeros(p_ref.shape, BF16)
    s_ref[0] = scores(make_q(0), 0)

    def pair_of_tiles(u, _):
        query_tile(2 * u, 0, 1, False, 2)
        query_tile(2 * u + 1, 2, 1, True, 0)
        return 0

    lax.fori_loop(0, nq // 2, pair_of_tiles, 0)


def _flash_bounded(make_q, n, v_rows, k_ref, vt_ref, s_ref, nq, t, emit, k_aug_ref=None):
    streams = k_ref.shape[0]
    per_stream = n // streams
    ones = jnp.ones((ONES_ROWS, t), BF16)
    width = per_stream * t
    key = lax.broadcasted_iota(jnp.int32, (t, width), 0)
    qry = lax.broadcasted_iota(jnp.int32, (t, width), 1) & (t - 1)

    def scores(q_all, j, slot):
        start = pl.multiple_of(j * t, t)
        for g in range(streams):
            k = k_ref[g, pl.ds(start, t), :]
            if k_aug_ref is not None:
                k = jnp.concatenate([k, k_aug_ref[pl.ds(start, t), :]], axis=1)
            s_ref[slot, g] = lax.dot_general(k, q_all[g], _NT, preferred_element_type=F32)

    def accumulate(j, slot, accs, diagonal):
        out = []
        for g in range(streams):
            p = jnp.exp2(s_ref[slot, g])
            if diagonal:
                p = jnp.where(key <= qry, p, 0.0)
            p = p.astype(BF16)
            vt = vt_ref[g, j]
            for li, (lo, hi) in enumerate(v_rows):
                lhs = jnp.concatenate([vt[lo:hi], ones], axis=0)
                out.append(accs[g * per_stream + li]
                           + jnp.dot(lhs, p[:, li * t:(li + 1) * t], preferred_element_type=F32))
        return tuple(out)

    def step(q_all, j, cur, nxt, accs):
        scores(q_all, j + 1, nxt)
        return accumulate(j, cur, accs, False)

    rows = [hi - lo for _ in range(streams) for lo, hi in v_rows]

    def query_tile(u, first, second, odd, next_first):
        qi = 2 * u + odd
        q_all = make_q(qi)

        def four_steps(i, a):
            j = 4 * i
            a = step(q_all, j, first, second, a)
            a = step(q_all, j + 1, second, first, a)
            a = step(q_all, j + 2, first, second, a)
            return step(q_all, j + 3, second, first, a)

        init = tuple(jnp.zeros((r + ONES_ROWS, t), F32) for r in rows)
        accs = lax.fori_loop(0, qi // 4, four_steps, init)

        def tail(accs, two_more):
            j = 4 * (qi // 4)
            if two_more:
                accs = step(q_all, j, first, second, accs)
                accs = step(q_all, j + 1, second, first, accs)
                j = j + 2
            if odd:
                accs = step(q_all, j, first, second, accs)
            scores(make_q(jnp.minimum(qi + 1, nq - 1)), 0, next_first)
            accs = accumulate(qi, second if odd else first, accs, True)
            emit(qi, [(a[:r], a[r:r + 1]) for a, r in zip(accs, rows)])

        pl.when((u & 1) == 0)(lambda: tail(accs, False))
        pl.when((u & 1) == 1)(lambda: tail(accs, True))

    scores(make_q(0), 0, 0)

    def pair_of_tiles(u, _):
        query_tile(u, 0, 1, 0, 2)
        query_tile(u, 2, 1, 1, 0)
        return 0

    lax.fori_loop(0, nq // 2, pair_of_tiles, 0)


def _diff_attn_kernel(lam_ref, q_ref, k_ref, vt_ref, sg_ref, o_ref, *scratch, t, lambda_init, bounded):
    nq = q_ref.shape[1] // t
    seg = lax.broadcasted_iota(jnp.int32, (t, LANES), 1) // DIFF_HALF
    lam_p = lam_ref[...]
    lam = (jnp.exp(jnp.sum(lam_p[0:1] * lam_p[1:2], axis=1, keepdims=True))
           - jnp.exp(jnp.sum(lam_p[2:3] * lam_p[3:4], axis=1, keepdims=True)) + lambda_init)

    streams = q_ref.shape[0]

    def make_q(qi):
        out = []
        for g in range(streams):
            q = q_ref[g, pl.ds(pl.multiple_of(qi * t, t), t), :]
            out.append(jnp.concatenate([jnp.where(seg == i, q, jnp.zeros_like(q)) for i in range(4)], axis=0))
        return out

    def emit(qi, res):
        a = [acc * (1.0 / l) for acc, l in res]
        for g in range(streams):
            heads = []
            for hh in range(2):
                o = a[4 * g + 2 * hh] - lam * a[4 * g + 2 * hh + 1]
                heads.append(o * lax.rsqrt(jnp.mean(o * o, axis=0, keepdims=True) + EPS))
            o = jnp.concatenate(heads, axis=0).T
            o_ref[g, pl.ds(pl.multiple_of(qi * t, t), t), :] = (
                o * sg_ref[...] * (1.0 - lambda_init)).astype(BF16)

    v_rows = [(HEAD_DIM * (i // 2), HEAD_DIM * (i // 2 + 1)) for i in range(4)]
    if bounded:
        _flash_bounded(make_q, 4 * streams, v_rows, k_ref, vt_ref, *scratch, nq, t, emit)
    else:
        _flash_head_pair(make_q, 4 * streams, v_rows, k_ref, vt_ref, *scratch, nq, t, emit)


def _moba_attn_kernel(q_ref, bias_ref, k_ref, vt_ref, ind_ref, o_ref, *scratch, t, bounded):
    hp = pl.program_id(1)
    nq = q_ref.shape[1] // t
    lane = lax.broadcasted_iota(jnp.int32, (t, LANES), 1)

    streams = q_ref.shape[0]

    def make_q(qi):
        rows = pl.ds(pl.multiple_of(qi * t, t), t)
        out = []
        for g in range(streams):
            q = q_ref[g, rows, :]
            bias = bias_ref[g, rows, :]
            strips = []
            for hh in range(2):
                qz = jnp.where(lane // HEAD_DIM == hh, q, jnp.zeros_like(q))
                bz = jnp.where(lane // GATE_SEG == 2 * hp + hh, bias, jnp.zeros_like(bias))
                strips.append(jnp.concatenate([qz, bz], axis=1))
            out.append(jnp.concatenate(strips, axis=0))
        return out

    def emit(qi, res):
        for g in range(streams):
            o = jnp.concatenate([acc * (1.0 / l) for acc, l in res[2 * g:2 * g + 2]], axis=0)
            o_ref[g, pl.ds(pl.multiple_of(qi * t, t), t), :] = o.T.astype(BF16)

    v_rows = [(0, HEAD_DIM), (HEAD_DIM, 2 * HEAD_DIM)]
    if bounded:
        _flash_bounded(make_q, 2 * streams, v_rows, k_ref, vt_ref, *scratch, nq, t, emit, k_aug_ref=ind_ref)
    else:
        _flash_head_pair(make_q, 2 * streams, v_rows, k_ref, vt_ref, *scratch, nq, t, emit, k_aug_ref=ind_ref)


def _attention(qk, vt, bias, ind, lam_params, subln_tile, lambda_init, bounded):
    B, S, _ = qk.shape
    t = MOBA_BLOCK
    n_pairs = SECTION // LANES
    assert (S // t) % 2 == 0
    params = pltpu.CompilerParams(
        dimension_semantics=("arbitrary", "arbitrary"),
        vmem_limit_bytes=VMEM_LIMIT_BYTES)
    out_shape = jax.ShapeDtypeStruct((B, S, SECTION), BF16)

    def streams(wanted):
        return wanted if B % wanted == 0 else 1

    def seq_spec(g, col):
        return pl.BlockSpec((g, S, LANES), lambda b, p: (b, 0, col(p)))

    def vt_spec(g, sec):
        return pl.BlockSpec((g, S // t, LANES, t), lambda b, p: (b, 0, sec * n_pairs + p, 0))

    def pipeline_scratch(g, n):
        if bounded:
            return [pltpu.VMEM((3, g, t, n * t), F32)]
        return [pltpu.VMEM((3, t, g * n * t), F32), pltpu.VMEM((3, g * n, t, t), BF16)]

    g = streams(BOUNDED_STREAMS if bounded else ATTN_STREAMS)
    diff = pl.pallas_call(
        functools.partial(_diff_attn_kernel, t=t, lambda_init=lambda_init, bounded=bounded),
        grid=(B // g, n_pairs),
        in_specs=[pl.BlockSpec((4, DIFF_HALF), lambda b, p: (0, 0)),
                  seq_spec(g, lambda p: p), seq_spec(g, lambda p: n_pairs + p), vt_spec(g, 0),
                  pl.BlockSpec((1, LANES), lambda b, p: (0, 0))],
        out_specs=seq_spec(g, lambda p: p), out_shape=out_shape, compiler_params=params,
        scratch_shapes=pipeline_scratch(g, 4),
        name="diff_attn",
    )(lam_params, qk, qk, vt, subln_tile)

    g = streams(BOUNDED_STREAMS if bounded else ATTN_STREAMS)
    moba = pl.pallas_call(
        functools.partial(_moba_attn_kernel, t=t, bounded=bounded),
        grid=(B // g, n_pairs),
        in_specs=[seq_spec(g, lambda p: 2 * n_pairs + p),
                  seq_spec(g, lambda p: 0),
                  seq_spec(g, lambda p: 3 * n_pairs + p), vt_spec(g, 1),
                  pl.BlockSpec((S, LANES), lambda b, p: (0, 0))],
        out_specs=seq_spec(g, lambda p: p), out_shape=out_shape, compiler_params=params,
        scratch_shapes=pipeline_scratch(g, 2),
        name="moba_attn",
    )(qk, bias, qk, vt, ind)
    return diff, moba


def _out_ffn_kernel(mixa_ref, mixb_ref, x_ref, wo_ref, g_ref, wi_ref, w2_ref, o_ref, *, d_ff, chunk):
    mix = jnp.concatenate([mixa_ref[0], mixb_ref[0]], axis=1)
    x1 = x_ref[0] + jnp.dot(mix, wo_ref[...].astype(BF16), preferred_element_type=F32)
    hn = _rms(x1, g_ref[...]).astype(BF16)
    acc = x1
    for c in range(d_ff // chunk):
        lo = c * chunk
        g = jnp.dot(hn, wi_ref[0, :, lo:lo + chunk].astype(BF16), preferred_element_type=F32)
        u = jnp.dot(hn, wi_ref[0, :, d_ff + lo:d_ff + lo + chunk].astype(BF16), preferred_element_type=F32)
        a = (g * jax.nn.sigmoid(g) * u).astype(BF16)
        acc = acc + jnp.dot(a, w2_ref[0, lo:lo + chunk, :].astype(BF16), preferred_element_type=F32)
    o_ref[0] = acc


def _out_ffn(mix_a, col_a, mix_b, col_b, x, w_out, norm_g, w_in, w2, layer):
    B, S, D = x.shape
    d_ff = w2.shape[1]
    tm = OUT_FFN_ROWS
    half = w_out.shape[0] // 2
    const = lambda b, t: (0, 0)
    resident = lambda shape: pl.BlockSpec(shape, const, pipeline_mode=pl.Buffered(1))
    slab = lambda a: pl.BlockSpec((1,) + a.shape[1:], lambda b, t: (layer, 0, 0), pipeline_mode=pl.Buffered(1))
    tile = pl.BlockSpec((1, tm, D), lambda b, t: (b, t, 0))
    mix_spec = lambda col: pl.BlockSpec((1, tm, half), lambda b, t: (b, t, col))
    return pl.pallas_call(
        functools.partial(_out_ffn_kernel, d_ff=d_ff, chunk=256),
        grid=(B, S // tm),
        in_specs=[mix_spec(col_a), mix_spec(col_b), tile, resident(w_out.shape), pl.BlockSpec((1, D), const),
                  slab(w_in), slab(w2)],
        out_specs=tile,
        out_shape=jax.ShapeDtypeStruct((B, S, D), F32),
        compiler_params=pltpu.CompilerParams(
            dimension_semantics=("arbitrary", "arbitrary"),
            vmem_limit_bytes=VMEM_LIMIT_BYTES),
        name="out_ffn",
    )(mix_a, mix_b, x, w_out, norm_g, w_in, w2)


def _odd_mix_kernel(x_ref, g_ref, w_ref, b_ref, lng_ref, lnb_ref, ws_ref, bs_ref,
                    cw_ref, cb_ref, clg_ref, clb_ref, o_ref, ext_ref, *, tm):
    ti = pl.program_id(1)
    h = _rms(x_ref[0], g_ref[...]).astype(BF16)

    def proj(lo, width):
        return jnp.dot(h, w_ref[:, lo:lo + width], preferred_element_type=F32) + b_ref[:, lo:lo + width]

    gu = jax.nn.gelu(proj(0, GMLP_WIDTH), approximate=True)
    gv = jax.nn.gelu(proj(GMLP_WIDTH, GMLP_WIDTH), approximate=True)
    gv = _layer_norm(gv, lng_ref[...], lnb_ref[...]).astype(BF16)
    T = GMLP_CHUNK
    tri = lax.broadcasted_iota(jnp.int32, (T, T), 1) <= lax.broadcasted_iota(jnp.int32, (T, T), 0)
    lane = lax.broadcasted_iota(jnp.int32, (T, LANES), 1)
    first = lane < (GMLP_WIDTH // GMLP_GROUPS)
    zero = jnp.zeros((T, LANES), BF16)
    for lb in range(GMLP_WIDTH // LANES):
        wpair = jnp.concatenate(
            [jnp.where(tri, ws_ref[2 * lb + i], 0.0).astype(BF16) for i in range(2)], axis=1)
        bs = bs_ref[:, lb * LANES:(lb + 1) * LANES]
        for c in range(tm // T):
            vc = gv[c * T:(c + 1) * T, lb * LANES:(lb + 1) * LANES]
            vpair = jnp.concatenate([jnp.where(first, vc, zero), jnp.where(first, zero, vc)], axis=0)
            sg = jnp.dot(wpair, vpair, preferred_element_type=F32) + bs
            o_ref[0, c * T:(c + 1) * T, lb * LANES:(lb + 1) * LANES] = (
                gu[c * T:(c + 1) * T, lb * LANES:(lb + 1) * LANES] * sg).astype(BF16)

    lo = 2 * GMLP_WIDTH
    cval = proj(lo, CONV_CH) * jax.nn.sigmoid(proj(lo + CONV_CH, CONV_CH))

    @pl.when(ti == 0)
    def _():
        ext_ref[0:CONV_HALO, :] = jnp.zeros((CONV_HALO, CONV_CH), F32)

    ext_ref[CONV_HALO:CONV_HALO + tm, :] = cval
    first_tap = CONV_HALO - (CONV_KERNEL - 1)
    ext = ext_ref[...]
    rows_total = ext.shape[0]
    conv = jnp.zeros((tm, CONV_CH), F32) + cb_ref[...]
    for r in range(8):
        rolled = ext if r == 0 else pltpu.roll(ext, rows_total - r, 0)
        for k in range(CONV_KERNEL):
            if (first_tap + k) % 8 == r:
                off = first_tap + k - r
                conv = conv + rolled[off:off + tm, :] * cw_ref[k:k + 1, :]
    ext_ref[0:CONV_HALO, :] = ext_ref[tm:tm + CONV_HALO, :]
    dn = _layer_norm(conv, clg_ref[...], clb_ref[...])
    o_ref[0, :, GMLP_WIDTH:GMLP_WIDTH + CONV_CH] = (dn * jax.nn.sigmoid(dn)).astype(BF16)


def _odd_mix(x, norm_g, w_in, b_in, ln_g, ln_b, w_s, bs_tile, conv_w, conv_b, cln_g, cln_b):
    B, S, D = x.shape
    tm = ODD_MIX_ROWS
    const2 = lambda b, t: (0, 0)
    full = lambda a: pl.BlockSpec(a.shape, (lambda b, t: (0,) * a.ndim))
    n_out = GMLP_WIDTH + CONV_CH
    span = tm + 8 * ((CONV_KERNEL - 1) // 8)
    args = (x, norm_g, w_in, b_in, ln_g, ln_b, w_s, bs_tile, conv_w, conv_b, cln_g, cln_b)
    return pl.pallas_call(
        functools.partial(_odd_mix_kernel, tm=tm),
        grid=(B, S // tm),
        in_specs=[pl.BlockSpec((1, tm, D), lambda b, t: (b, t, 0))] + [full(a) for a in args[1:]],
        out_specs=pl.BlockSpec((1, tm, n_out), lambda b, t: (b, t, 0)),
        out_shape=jax.ShapeDtypeStruct((B, S, n_out), BF16),
        scratch_shapes=[pltpu.VMEM((span + 8, CONV_CH), F32)],
        compiler_params=pltpu.CompilerParams(
            dimension_semantics=("arbitrary", "arbitrary"),
            vmem_limit_bytes=VMEM_LIMIT_BYTES),
        name="odd_mix",
    )(*args)


def _rope_tables(seq, dim, seg_per_block):
    inv = 1.0 / (ROPE_THETA ** (jnp.arange(0, dim, 2, dtype=F32) / dim))
    ang = jnp.arange(seq, dtype=F32)[:, None] * inv[None, :]
    cos, sin = jnp.cos(ang), jnp.sin(ang)
    cos = jnp.tile(jnp.concatenate([cos, cos], axis=1), (1, seg_per_block))
    sin = jnp.tile(jnp.concatenate([-sin, sin], axis=1), (1, seg_per_block))
    return cos, sin


def _block_diag_ones(seg):
    idx = jnp.arange(BD) // seg
    return (idx[:, None] == idx[None, :]).astype(BF16)


def kernel(x, attn_norm_g, ffn_norm_g, ffn_w_in, ffn_w_out, even_w_in, even_w_out,
           diff_q_norm_g, diff_k_norm_g, diff_lambda_q1, diff_lambda_k1,
           diff_lambda_q2, diff_lambda_k2, diff_subln_g, moba_q_norm_g, moba_k_norm_g,
           odd_w_in, odd_b_in, odd_w_out, gmlp_ln_g, gmlp_ln_b, gmlp_w_s, gmlp_b_s,
           conv_w, conv_b, conv_ln_g, conv_ln_b):
    B, S, D = x.shape
    assert S % max(EVEN_PROJ_ROWS, ODD_MIX_ROWS, OUT_FFN_ROWS) == 0 and S // MOBA_BLOCK <= GATE_SEG
    row = lambda v: v.reshape(1, -1).astype(F32)

    lambda_init = 0.8 - 0.6 * math.exp(-0.3 * 0)
    cosd, sind = _rope_tables(S, DIFF_HALF, LANES // DIFF_HALF)
    cosm, sinm = _rope_tables(S, HEAD_DIM, LANES // HEAD_DIM)
    gains = jnp.stack([
        jnp.tile(diff_q_norm_g[0], SECTION // DIFF_HALF),
        jnp.tile(diff_k_norm_g[0], SECTION // DIFF_HALF),
        jnp.tile(moba_q_norm_g[0], SECTION // HEAD_DIM),
        jnp.tile(moba_k_norm_g[0], SECTION // HEAD_DIM)]).astype(F32)
    qk, vt, bias = _even_proj(x, row(attn_norm_g[0]), even_w_in[0].astype(BF16),
                              _block_diag_ones(DIFF_HALF), _block_diag_ones(HEAD_DIM), gains,
                              cosd, sind, cosm, sinm)
    key_block = jnp.arange(S)[:, None] // MOBA_BLOCK
    ind = (key_block == (jnp.arange(LANES)[None, :] % GATE_SEG)).astype(BF16)
    lam_params = jnp.stack([diff_lambda_q1[0], diff_lambda_k1[0],
                            diff_lambda_q2[0], diff_lambda_k2[0]]).astype(F32)
    subln_tile = jnp.tile(diff_subln_g[0], LANES // HEAD_DIM).reshape(1, LANES).astype(F32)
    def score_bound(d, gq, gk):
        return (d * d ** -0.5 * LOG2E * (1.0 + 2.0 ** -7)
                * jnp.max(jnp.abs(gq.astype(F32))) * jnp.max(jnp.abs(gk.astype(F32))))

    bounded = jnp.logical_and(score_bound(DIFF_HALF, diff_q_norm_g[0], diff_k_norm_g[0]) <= SCORE_BOUND,
                              score_bound(HEAD_DIM, moba_q_norm_g[0], moba_k_norm_g[0]) <= SCORE_BOUND)
    attend = lambda flag: (lambda *a: _attention(*a, lambda_init, flag))
    diff, moba = lax.cond(bounded, attend(True), attend(False), qk, vt, bias, ind, lam_params, subln_tile)
    x = _out_ffn(diff, 0, moba, 0, x, even_w_out[0], row(ffn_norm_g[0]), ffn_w_in, ffn_w_out, 0)

    bs_tile = jnp.repeat(gmlp_b_s[0].T, GMLP_WIDTH // GMLP_GROUPS, axis=1).astype(F32)
    mix = _odd_mix(x, row(attn_norm_g[1]), odd_w_in[0].astype(BF16), row(odd_b_in[0]),
                   row(gmlp_ln_g[0]), row(gmlp_ln_b[0]), gmlp_w_s[0].astype(F32), bs_tile,
                   conv_w[0].astype(F32), row(conv_b[0]), row(conv_ln_g[0]), row(conv_ln_b[0]))
    x = _out_ffn(mix, 0, mix, 1, x, odd_w_out[0], row(ffn_norm_g[1]), ffn_w_in, ffn_w_out, 1)
    return x
```

```python
import functools
import math

import jax
import jax.numpy as jnp
from jax import lax
from jax.experimental import pallas as pl
from jax.experimental.pallas import tpu as pltpu

F32 = jnp.float32
BF16 = jnp.bfloat16

LANES = 128
VMEM_LIMIT_BYTES = 56 * 1024 * 1024

HEAD_DIM = 64
DIFF_HALF = 32
N_HEADS_MOBA = 8
MOBA_BLOCK = 256
MOBA_TOPK = 3
GMLP_CHUNK = 128
GMLP_GROUPS = 8
GMLP_WIDTH = 512
CONV_CH = 512
CONV_KERNEL = 31
ROPE_THETA = 10000.0
EPS = 1e-6

SECTION = 512
BD = 256
EVEN_PROJ_ROWS = 1024
OUT_FFN_ROWS = 512
ATTN_STREAMS = 2
BOUNDED_STREAMS = 4
SCORE_BOUND = 40.0
ONES_ROWS = 16
GATE_SEG = LANES // N_HEADS_MOBA
CONV_HALO = 32
MASKED_BIAS = -1e30
M_INIT = -1e29
LOG2E = math.log2(math.e)

_NT = (((1,), (1,)), ((), ()))


def _rms(x, g):
    return x * lax.rsqrt(jnp.mean(x * x, axis=-1, keepdims=True) + EPS) * g


def _layer_norm(x, g, b):
    mu = jnp.mean(x, axis=-1, keepdims=True)
    xc = x - mu
    return xc * lax.rsqrt(jnp.mean(xc * xc, axis=-1, keepdims=True) + EPS) * g + b


def _even_proj_kernel(x_ref, g_ref, w_ref, bdd_ref, bdm_ref, gain_ref,
                      cosd_ref, sind_ref, cosm_ref, sinm_ref,
                      qk_ref, vt_ref, bias_ref, kbt_ref, *, tm):
    blocks = tm // MOBA_BLOCK
    first_blk = pl.program_id(1) * blocks
    h = _rms(x_ref[0], g_ref[...]).astype(BF16)

    def qk_section(sec, bd_ref, seg, cos_ref, sin_ref):
        y = jnp.dot(h, w_ref[:, sec * SECTION:(sec + 1) * SECTION], preferred_element_type=F32)
        y2 = (y * y).astype(BF16)
        ss = jnp.concatenate(
            [jnp.dot(y2[:, c * BD:(c + 1) * BD], bd_ref[...], preferred_element_type=F32)
             for c in range(SECTION // BD)], axis=1)
        yn = y * lax.rsqrt(ss * (1.0 / seg) + EPS) * gain_ref[sec_row[sec]:sec_row[sec] + 1, :]
        half = seg // 2
        lower = (lax.broadcasted_iota(jnp.int32, (tm, LANES), 1) & (seg - 1)) < half
        cos = cos_ref[...]
        sin = sin_ref[...]
        out = []
        for c in range(SECTION // LANES):
            yc = yn[:, c * LANES:(c + 1) * LANES]
            rot = jnp.where(lower, pltpu.roll(yc, LANES - half, 1), pltpu.roll(yc, half, 1))
            out.append(yc * cos + rot * sin)
        return out

    def store(out_sec, chunks, scale):
        for c, yc in enumerate(chunks):
            lo = out_sec * SECTION + c * LANES
            qk_ref[0, :, lo:lo + LANES] = (yc * scale).astype(BF16)

    sec_row = {0: 0, 1: 1, 3: 2, 4: 3}
    store(0, qk_section(0, bdd_ref, DIFF_HALF, cosd_ref, sind_ref), DIFF_HALF ** -0.5 * LOG2E)
    store(1, qk_section(1, bdd_ref, DIFF_HALF, cosd_ref, sind_ref), 1.0)
    mq = qk_section(3, bdm_ref, HEAD_DIM, cosm_ref, sinm_ref)
    store(2, mq, HEAD_DIM ** -0.5 * LOG2E)
    mk = qk_section(4, bdm_ref, HEAD_DIM, cosm_ref, sinm_ref)
    store(3, mk, 1.0)
    for out_sec, sec in enumerate((2, 5)):
        lo = sec * SECTION
        vt = jnp.dot(h, w_ref[:, lo:lo + SECTION], preferred_element_type=F32).T.astype(BF16)
        for sb in range(blocks):
            vt_ref[0, sb, out_sec * SECTION:(out_sec + 1) * SECTION, :] = (
                vt[:, sb * MOBA_BLOCK:(sb + 1) * MOBA_BLOCK])

    @pl.when(first_blk == 0)
    def _():
        kbt_ref[...] = jnp.zeros_like(kbt_ref)

    mk_full = jnp.concatenate(mk, axis=1)
    head_of_lane = lax.broadcasted_iota(jnp.int32, (1, SECTION), 1) // HEAD_DIM
    for sb in range(blocks):
        kbar = jnp.mean(mk_full[sb * MOBA_BLOCK:(sb + 1) * MOBA_BLOCK], axis=0, keepdims=True)
        for hh in range(N_HEADS_MOBA):
            kbt_ref[pl.ds(hh * GATE_SEG + first_blk + sb, 1), :] = jnp.where(head_of_lane == hh, kbar, 0.0)

    def split(a):
        hi = a.astype(BF16)
        return hi, (a - hi.astype(F32)).astype(BF16)

    q_hi, q_lo = split(jnp.concatenate(mq, axis=1))
    k_hi, k_lo = split(kbt_ref[...])
    gate = lax.dot_general(jnp.concatenate([k_hi, k_hi, k_lo], axis=1),
                           jnp.concatenate([q_hi, q_lo, q_hi], axis=1), _NT,
                           preferred_element_type=F32)
    blk = lax.broadcasted_iota(jnp.int32, (GATE_SEG, tm), 0)
    blk_f = blk.astype(F32)
    neg_inf = jnp.float32(-jnp.inf)
    own = first_blk + lax.broadcasted_iota(jnp.int32, (GATE_SEG, tm), 1) // MOBA_BLOCK
    bias_rows = []
    for hh in range(N_HEADS_MOBA):
        avail = jnp.where(blk < own, gate[hh * GATE_SEG:(hh + 1) * GATE_SEG], neg_inf)
        chosen = blk == own
        for _ in range(MOBA_TOPK):
            best = jnp.max(avail, axis=0, keepdims=True)
            cand = jnp.logical_and(avail == best, avail > neg_inf)
            first = jnp.min(jnp.where(cand, blk_f, float(GATE_SEG)), axis=0, keepdims=True)
            pick = blk_f == first
            chosen = jnp.logical_or(chosen, pick)
            avail = jnp.where(pick, neg_inf, avail)
        bias_rows.append(jnp.where(chosen, 0.0, MASKED_BIAS))
    bias_ref[0] = jnp.concatenate(bias_rows, axis=0).T.astype(BF16)


def _even_proj(x, norm_g, w_in, bdd, bdm, gains, cosd, sind, cosm, sinm):
    B, S, D = x.shape
    tm = EVEN_PROJ_ROWS
    blocks = tm // MOBA_BLOCK
    n_out = w_in.shape[1]
    n_qk = 4 * SECTION
    n_v = 2 * SECTION
    const = lambda b, t: (0, 0)
    tab = pl.BlockSpec((tm, LANES), lambda b, t: (t, 0))
    return pl.pallas_call(
        functools.partial(_even_proj_kernel, tm=tm),
        grid=(B, S // tm),
        in_specs=[
            pl.BlockSpec((1, tm, D), lambda b, t: (b, t, 0)),
            pl.BlockSpec((1, D), const),
            pl.BlockSpec((D, n_out), const, pipeline_mode=pl.Buffered(1)),
            pl.BlockSpec((BD, BD), const),
            pl.BlockSpec((BD, BD), const),
            pl.BlockSpec((4, SECTION), const),
            tab, tab, tab, tab,
        ],
        out_specs=[
            pl.BlockSpec((1, tm, n_qk), lambda b, t: (b, t, 0)),
            pl.BlockSpec((1, blocks, n_v, MOBA_BLOCK), lambda b, t: (b, t, 0, 0)),
            pl.BlockSpec((1, tm, LANES), lambda b, t: (b, t, 0)),
        ],
        out_shape=[
            jax.ShapeDtypeStruct((B, S, n_qk), BF16),
            jax.ShapeDtypeStruct((B, S // MOBA_BLOCK, n_v, MOBA_BLOCK), BF16),
            jax.ShapeDtypeStruct((B, S, LANES), BF16),
        ],
        scratch_shapes=[pltpu.VMEM((LANES, SECTION), F32)],
        compiler_params=pltpu.CompilerParams(
            dimension_semantics=("arbitrary", "arbitrary"),
            vmem_limit_bytes=VMEM_LIMIT_BYTES),
        name="even_proj",
    )(x, norm_g, w_in, bdd, bdm, gains, cosd, sind, cosm, sinm)


def _flash_head_pair(make_q, n, v_rows, k_ref, vt_ref, s_ref, p_ref, nq, t, emit, k_aug_ref=None):
    streams = k_ref.shape[0]
    per_stream = n // streams

    def scores(q_all, j):
        start = pl.multiple_of(j * t, t)
        out = []
        for g in range(streams):
            k = k_ref[g, pl.ds(start, t), :]
            if k_aug_ref is not None:
                k = jnp.concatenate([k, k_aug_ref[pl.ds(start, t), :]], axis=1)
            out.append(lax.dot_general(k, q_all[g], _NT, preferred_element_type=F32))
        return jnp.concatenate(out, axis=1)

    def softmax(s_all, stats, diagonal):
        ps, new_stats, alphas = [], [], []
        for i in range(n):
            m, l = stats[i]
            s = s_all[:, i * t:(i + 1) * t]
            if diagonal:
                key = lax.broadcasted_iota(jnp.int32, (t, t), 0)
                qry = lax.broadcasted_iota(jnp.int32, (t, t), 1)
                s = jnp.where(key <= qry, s, -jnp.inf)
            m_new = jnp.maximum(m, jnp.max(s, axis=0, keepdims=True))
            p = jnp.exp2(s - m_new)
            alpha = jnp.exp2(m - m_new)
            ps.append(p.astype(BF16))
            new_stats.append((m_new, alpha * l + jnp.sum(p, axis=0, keepdims=True)))
            alphas.append(alpha)
        return tuple(ps), tuple(new_stats), alphas

    def weighted_values(j, ps):
        out = []
        for g in range(streams):
            vt = vt_ref[g, j]
            for li, (lo, hi) in enumerate(v_rows):
                out.append(jnp.dot(vt[lo:hi], ps[g * per_stream + li], preferred_element_type=F32))
        return out

    def step(q_all, j, cur, nxt, carry):
        stats, accs = carry
        s_ref[nxt] = scores(q_all, j + 1)
        p_prev = tuple(p_ref[nxt, i] for i in range(n))
        pv = weighted_values(jnp.maximum(j - 1, 0), p_prev)
        pv = [jnp.where(j > 0, x, 0.0) for x in pv]
        p_cur, stats, alphas = softmax(s_ref[cur], stats, False)
        for i in range(n):
            p_ref[cur, i] = p_cur[i]
        accs = tuple(a * (acc + x) for a, acc, x in zip(alphas, accs, pv))
        return stats, accs

    def query_tile(qi, first, second, odd, next_first):
        q_all = make_q(qi)
        init = (tuple((jnp.full((1, t), M_INIT, F32), jnp.zeros((1, t), F32)) for _ in range(n)),
                tuple(jnp.zeros((hi - lo, t), F32) for _ in range(streams) for lo, hi in v_rows))
        carry = lax.fori_loop(
            0, qi // 2,
            lambda i, c: step(q_all, 2 * i + 1, second, first, step(q_all, 2 * i, first, second, c)), init)
        if odd:
            carry = step(q_all, qi - 1, first, second, carry)
        diag, prev = (second, first) if odd else (first, second)
        stats, accs = carry
        s_ref[next_first] = scores(make_q(jnp.minimum(qi + 1, nq - 1)), 0)
        pv_prev = weighted_values(jnp.maximum(qi - 1, 0), tuple(p_ref[prev, i] for i in range(n)))
        pv_prev = [jnp.where(qi > 0, x, 0.0) for x in pv_prev]
        p_cur, stats, alphas = softmax(s_ref[diag], stats, True)
        pv_cur = weighted_values(qi, p_cur)
        emit(qi, [(a * (acc + x) + y, l)
                  for a, acc, x, y, (_, l) in zip(alphas, accs, pv_prev, pv_cur, stats)])

    p_ref[...] = jnp.zeros(p_ref.shape, BF16)
    s_ref[0] = scores(make_q(0), 0)

    def pair_of_tiles(u, _):
        query_tile(2 * u, 0, 1, False, 2)
        query_tile(2 * u + 1, 2, 1, True, 0)
        return 0

    lax.fori_loop(0, nq // 2, pair_of_tiles, 0)


def _flash_bounded(make_q, n, v_rows, k_ref, vt_ref, s_ref, nq, t, emit, k_aug_ref=None):
    streams = k_ref.shape[0]
    per_stream = n // streams
    ones = jnp.ones((ONES_ROWS, t), BF16)
    width = per_stream * t
    key = lax.broadcasted_iota(jnp.int32, (t, width), 0)
    qry = lax.broadcasted_iota(jnp.int32, (t, width), 1) & (t - 1)

    def scores(q_all, j, slot):
        start = pl.multiple_of(j * t, t)
        for g in range(streams):
            k = k_ref[g, pl.ds(start, t), :]
            if k_aug_ref is not None:
                k = jnp.concatenate([k, k_aug_ref[pl.ds(start, t), :]], axis=1)
            s_ref[slot, g] = lax.dot_general(k, q_all[g], _NT, preferred_element_type=F32)

    def accumulate(j, slot, accs, diagonal):
        out = []
        for g in range(streams):
            p = jnp.exp2(s_ref[slot, g])
            if diagonal:
                p = jnp.where(key <= qry, p, 0.0)
            p = p.astype(BF16)
            vt = vt_ref[g, j]
            for li, (lo, hi) in enumerate(v_rows):
                lhs = jnp.concatenate([vt[lo:hi], ones], axis=0)
                out.append(accs[g * per_stream + li]
                           + jnp.dot(lhs, p[:, li * t:(li + 1) * t], preferred_element_type=F32))
        return tuple(out)

    def step(q_all, j, cur, nxt, accs):
        scores(q_all, j + 1, nxt)
        return accumulate(j, cur, accs, False)

    rows = [hi - lo for _ in range(streams) for lo, hi in v_rows]

    def query_tile(u, first, second, odd, next_first):
        qi = 2 * u + odd
        q_all = make_q(qi)

        def four_steps(i, a):
            j = 4 * i
            a = step(q_all, j, first, second, a)
            a = step(q_all, j + 1, second, first, a)
            a = step(q_all, j + 2, first, second, a)
            return step(q_all, j + 3, second, first, a)

        init = tuple(jnp.zeros((r + ONES_ROWS, t), F32) for r in rows)
        accs = lax.fori_loop(0, qi // 4, four_steps, init)

        def tail(accs, two_more):
            j = 4 * (qi // 4)
            if two_more:
                accs = step(q_all, j, first, second, accs)
                accs = step(q_all, j + 1, second, first, accs)
                j = j + 2
            if odd:
                accs = step(q_all, j, first, second, accs)
            scores(make_q(jnp.minimum(qi + 1, nq - 1)), 0, next_first)
            accs = accumulate(qi, second if odd else first, accs, True)
            emit(qi, [(a[:r], a[r:r + 1]) for a, r in zip(accs, rows)])

        pl.when((u & 1) == 0)(lambda: tail(accs, False))
        pl.when((u & 1) == 1)(lambda: tail(accs, True))

    scores(make_q(0), 0, 0)

    def pair_of_tiles(u, _):
        query_tile(u, 0, 1, 0, 2)
        query_tile(u, 2, 1, 1, 0)
        return 0

    lax.fori_loop(0, nq // 2, pair_of_tiles, 0)


def _diff_attn_kernel(lam_ref, q_ref, k_ref, vt_ref, sg_ref, o_ref, *scratch, t, lambda_init, bounded):
    nq = q_ref.shape[1] // t
    seg = lax.broadcasted_iota(jnp.int32, (t, LANES), 1) // DIFF_HALF
    lam_p = lam_ref[...]
    lam = (jnp.exp(jnp.sum(lam_p[0:1] * lam_p[1:2], axis=1, keepdims=True))
           - jnp.exp(jnp.sum(lam_p[2:3] * lam_p[3:4], axis=1, keepdims=True)) + lambda_init)

    streams = q_ref.shape[0]

    def make_q(qi):
        out = []
        for g in range(streams):
            q = q_ref[g, pl.ds(pl.multiple_of(qi * t, t), t), :]
            out.append(jnp.concatenate([jnp.where(seg == i, q, jnp.zeros_like(q)) for i in range(4)], axis=0))
        return out

    def emit(qi, res):
        a = [acc * (1.0 / l) for acc, l in res]
        for g in range(streams):
            heads = []
            for hh in range(2):
                o = a[4 * g + 2 * hh] - lam * a[4 * g + 2 * hh + 1]
                heads.append(o * lax.rsqrt(jnp.mean(o * o, axis=0, keepdims=True) + EPS))
            o = jnp.concatenate(heads, axis=0).T
            o_ref[g, pl.ds(pl.multiple_of(qi * t, t), t), :] = (
                o * sg_ref[...] * (1.0 - lambda_init)).astype(BF16)

    v_rows = [(HEAD_DIM * (i // 2), HEAD_DIM * (i // 2 + 1)) for i in range(4)]
    if bounded:
        _flash_bounded(make_q, 4 * streams, v_rows, k_ref, vt_ref, *scratch, nq, t, emit)
    else:
        _flash_head_pair(make_q, 4 * streams, v_rows, k_ref, vt_ref, *scratch, nq, t, emit)


def _moba_attn_kernel(q_ref, bias_ref, k_ref, vt_ref, ind_ref, o_ref, *scratch, t, bounded):
    hp = pl.program_id(1)
    nq = q_ref.shape[1] // t
    lane = lax.broadcasted_iota(jnp.int32, (t, LANES), 1)

    streams = q_ref.shape[0]

    def make_q(qi):
        rows = pl.ds(pl.multiple_of(qi * t, t), t)
        out = []
        for g in range(streams):
            q = q_ref[g, rows, :]
            bias = bias_ref[g, rows, :]
            strips = []
            for hh in range(2):
                qz = jnp.where(lane // HEAD_DIM == hh, q, jnp.zeros_like(q))
                bz = jnp.where(lane // GATE_SEG == 2 * hp + hh, bias, jnp.zeros_like(bias))
                strips.append(jnp.concatenate([qz, bz], axis=1))
            out.append(jnp.concatenate(strips, axis=0))
        return out

    def emit(qi, res):
        for g in range(streams):
            o = jnp.concatenate([acc * (1.0 / l) for acc, l in res[2 * g:2 * g + 2]], axis=0)
            o_ref[g, pl.ds(pl.multiple_of(qi * t, t), t), :] = o.T.astype(BF16)

    v_rows = [(0, HEAD_DIM), (HEAD_DIM, 2 * HEAD_DIM)]
    if bounded:
        _flash_bounded(make_q, 2 * streams, v_rows, k_ref, vt_ref, *scratch, nq, t, emit, k_aug_ref=ind_ref)
    else:
        _flash_head_pair(make_q, 2 * streams, v_rows, k_ref, vt_ref, *scratch, nq, t, emit, k_aug_ref=ind_ref)


def _attention(qk, vt, bias, ind, lam_params, subln_tile, lambda_init, bounded):
    B, S, _ = qk.shape
    t = MOBA_BLOCK
    n_pairs = SECTION // LANES
    assert (S // t) % 2 == 0
    params = pltpu.CompilerParams(
        dimension_semantics=("arbitrary", "arbitrary"),
        vmem_limit_bytes=VMEM_LIMIT_BYTES)
    out_shape = jax.ShapeDtypeStruct((B, S, SECTION), BF16)

    def streams(wanted):
        return wanted if B % wanted == 0 else 1

    def seq_spec(g, col):
        return pl.BlockSpec((g, S, LANES), lambda b, p: (b, 0, col(p)))

    def vt_spec(g, sec):
        return pl.BlockSpec((g, S // t, LANES, t), lambda b, p: (b, 0, sec * n_pairs + p, 0))

    def pipeline_scratch(g, n):
        if bounded:
            return [pltpu.VMEM((3, g, t, n * t), F32)]
        return [pltpu.VMEM((3, t, g * n * t), F32), pltpu.VMEM((3, g * n, t, t), BF16)]

    g = streams(BOUNDED_STREAMS if bounded else ATTN_STREAMS)
    diff = pl.pallas_call(
        functools.partial(_diff_attn_kernel, t=t, lambda_init=lambda_init, bounded=bounded),
        grid=(B // g, n_pairs),
        in_specs=[pl.BlockSpec((4, DIFF_HALF), lambda b, p: (0, 0)),
                  seq_spec(g, lambda p: p), seq_spec(g, lambda p: n_pairs + p), vt_spec(g, 0),
                  pl.BlockSpec((1, LANES), lambda b, p: (0, 0))],
        out_specs=seq_spec(g, lambda p: p), out_shape=out_shape, compiler_params=params,
        scratch_shapes=pipeline_scratch(g, 4),
        name="diff_attn",
    )(lam_params, qk, qk, vt, subln_tile)

    g = streams(BOUNDED_STREAMS if bounded else ATTN_STREAMS)
    moba = pl.pallas_call(
        functools.partial(_moba_attn_kernel, t=t, bounded=bounded),
        grid=(B // g, n_pairs),
        in_specs=[seq_spec(g, lambda p: 2 * n_pairs + p),
                  seq_spec(g, lambda p: 0),
                  seq_spec(g, lambda p: 3 * n_pairs + p), vt_spec(g, 1),
                  pl.BlockSpec((S, LANES), lambda b, p: (0, 0))],
        out_specs=seq_spec(g, lambda p: p), out_shape=out_shape, compiler_params=params,
        scratch_shapes=pipeline_scratch(g, 2),
        name="moba_attn",
    )(qk, bias, qk, vt, ind)
    return diff, moba


def _out_ffn_kernel(mixa_ref, mixb_ref, x_ref, wo_ref, g_ref, wi_ref, w2_ref, o_ref, *, d_ff, chunk):
    mix = jnp.concatenate([mixa_ref[0], mixb_ref[0]], axis=1)
    x1 = x_ref[0] + jnp.dot(mix, wo_ref[...].astype(BF16), preferred_element_type=F32)
    hn = _rms(x1, g_ref[...]).astype(BF16)
    acc = x1
    for c in range(d_ff // chunk):
        lo = c * chunk
        g = jnp.dot(hn, wi_ref[0, :, lo:lo + chunk].astype(BF16), preferred_element_type=F32)
        u = jnp.dot(hn, wi_ref[0, :, d_ff + lo:d_ff + lo + chunk].astype(BF16), preferred_element_type=F32)
        a = (g * jax.nn.sigmoid(g) * u).astype(BF16)
        acc = acc + jnp.dot(a, w2_ref[0, lo:lo + chunk, :].astype(BF16), preferred_element_type=F32)
    o_ref[0] = acc


def _out_ffn(mix_a, col_a, mix_b, col_b, x, w_out, norm_g, w_in, w2, layer):
    B, S, D = x.shape
    d_ff = w2.shape[1]
    tm = OUT_FFN_ROWS
    half = w_out.shape[0] // 2
    const = lambda b, t: (0, 0)
    resident = lambda shape: pl.BlockSpec(shape, const, pipeline_mode=pl.Buffered(1))
    slab = lambda a: pl.BlockSpec((1,) + a.shape[1:], lambda b, t: (layer, 0, 0), pipeline_mode=pl.Buffered(1))
    tile = pl.BlockSpec((1, tm, D), lambda b, t: (b, t, 0))
    mix_spec = lambda col: pl.BlockSpec((1, tm, half), lambda b, t: (b, t, col))
    return pl.pallas_call(
        functools.partial(_out_ffn_kernel, d_ff=d_ff, chunk=256),
        grid=(B, S // tm),
        in_specs=[mix_spec(col_a), mix_spec(col_b), tile, resident(w_out.shape), pl.BlockSpec((1, D), const),
                  slab(w_in), slab(w2)],
        out_specs=tile,
        out_shape=jax.ShapeDtypeStruct((B, S, D), F32),
        compiler_params=pltpu.CompilerParams(
            dimension_semantics=("arbitrary", "arbitrary"),
            vmem_limit_bytes=VMEM_LIMIT_BYTES),
        name="out_ffn",
    )(mix_a, mix_b, x, w_out, norm_g, w_in, w2)


def _odd_mix_stages(x_ref, g_ref, w_ref, b_ref, lng_ref, lnb_ref, ws_ref, bs_ref,
                    cw_ref, cb_ref, clg_ref, clb_ref, o_ref, ext_ref, *, tm, ti):
    h = _rms(x_ref[0], g_ref[...]).astype(BF16)

    def proj(lo, width):
        return jnp.dot(h, w_ref[:, lo:lo + width], preferred_element_type=F32) + b_ref[:, lo:lo + width]

    z_u = proj(0, GMLP_WIDTH)
    z_v = proj(GMLP_WIDTH, GMLP_WIDTH)
    z_a = proj(2 * GMLP_WIDTH, CONV_CH)
    z_g = proj(2 * GMLP_WIDTH + CONV_CH, CONV_CH)
    yield

    gu = jax.nn.gelu(z_u, approximate=True)
    gv = jax.nn.gelu(z_v, approximate=True)
    gv = _layer_norm(gv, lng_ref[...], lnb_ref[...]).astype(BF16)
    T = GMLP_CHUNK
    tri = lax.broadcasted_iota(jnp.int32, (T, T), 1) <= lax.broadcasted_iota(jnp.int32, (T, T), 0)
    lane = lax.broadcasted_iota(jnp.int32, (T, LANES), 1)
    first = lane < (GMLP_WIDTH // GMLP_GROUPS)
    zero = jnp.zeros((T, LANES), BF16)
    for lb in range(GMLP_WIDTH // LANES):
        wpair = jnp.concatenate(
            [jnp.where(tri, ws_ref[2 * lb + i], 0.0).astype(BF16) for i in range(2)], axis=1)
        bs = bs_ref[:, lb * LANES:(lb + 1) * LANES]
        for c in range(tm // T):
            vc = gv[c * T:(c + 1) * T, lb * LANES:(lb + 1) * LANES]
            vpair = jnp.concatenate([jnp.where(first, vc, zero), jnp.where(first, zero, vc)], axis=0)
            sg = jnp.dot(wpair, vpair, preferred_element_type=F32) + bs
            o_ref[0, c * T:(c + 1) * T, lb * LANES:(lb + 1) * LANES] = (
                gu[c * T:(c + 1) * T, lb * LANES:(lb + 1) * LANES] * sg).astype(BF16)

    cval = z_a * jax.nn.sigmoid(z_g)

    ext_ref[0:CONV_HALO, :] = jnp.where(ti == 0, 0.0, ext_ref[0:CONV_HALO, :])
    ext_ref[CONV_HALO:CONV_HALO + tm, :] = cval
    first_tap = CONV_HALO - (CONV_KERNEL - 1)
    ext = ext_ref[...]
    rows_total = ext.shape[0]
    conv = jnp.zeros((tm, CONV_CH), F32) + cb_ref[...]
    for r in range(8):
        rolled = ext if r == 0 else pltpu.roll(ext, rows_total - r, 0)
        for k in range(CONV_KERNEL):
            if (first_tap + k) % 8 == r:
                off = first_tap + k - r
                conv = conv + rolled[off:off + tm, :] * cw_ref[k:k + 1, :]
    ext_ref[0:CONV_HALO, :] = ext_ref[tm:tm + CONV_HALO, :]
    dn = _layer_norm(conv, clg_ref[...], clb_ref[...])
    o_ref[0, :, GMLP_WIDTH:GMLP_WIDTH + CONV_CH] = (dn * jax.nn.sigmoid(dn)).astype(BF16)


def _odd_ffn_kernel(xa_ref, xb_ref, g_ref, w_ref, b_ref, lng_ref, lnb_ref, ws_ref, bs_ref,
                    cw_ref, cb_ref, clg_ref, clb_ref, wo_ref, g2_ref, wi_ref, w2_ref,
                    o_ref, ext_ref, mix_ref, *, tm, tiles_per_seq, n_tiles, d_ff, chunk):
    s = pl.program_id(0)

    @pl.when(s == 0)
    def _():
        mix_ref[...] = jnp.zeros_like(mix_ref)
        ext_ref[0:CONV_HALO, :] = jnp.zeros((CONV_HALO, CONV_CH), F32)

    tile = jnp.minimum(s, n_tiles - 1)
    mixers = _odd_mix_stages(xb_ref, g_ref, w_ref, b_ref, lng_ref, lnb_ref, ws_ref, bs_ref,
                             cw_ref, cb_ref, clg_ref, clb_ref, mix_ref, ext_ref,
                             tm=tm, ti=tile % tiles_per_seq)
    mix_prev = mix_ref[0]
    next(mixers)

    x1 = xa_ref[0] + jnp.dot(mix_prev, wo_ref[...], preferred_element_type=F32)
    hn = _rms(x1, g2_ref[...]).astype(BF16)
    acc = x1
    for c in range(d_ff // chunk):
        lo = c * chunk
        gate = jnp.dot(hn, wi_ref[:, lo:lo + chunk], preferred_element_type=F32)
        up = jnp.dot(hn, wi_ref[:, d_ff + lo:d_ff + lo + chunk], preferred_element_type=F32)
        a = (gate * jax.nn.sigmoid(gate) * up).astype(BF16)
        acc = acc + jnp.dot(a, w2_ref[lo:lo + chunk, :], preferred_element_type=F32)
    o_ref[0] = acc

    for _ in mixers:
        pass


def _odd_ffn(x, norm_g, w_in, b_in, ln_g, ln_b, w_s, bs_tile, conv_w, conv_b, cln_g, cln_b,
             w_out, norm2_g, ffn_in, ffn_out):
    B, S, D = x.shape
    tm = OUT_FFN_ROWS
    nt = S // tm
    n_tiles = B * nt
    d_ff = ffn_out.shape[0]
    prev = lambda s: jnp.maximum(s - 1, 0)
    cur = lambda s: jnp.minimum(s, n_tiles - 1)
    full = lambda a: pl.BlockSpec(a.shape, (lambda s: (0,) * a.ndim), pipeline_mode=pl.Buffered(1))
    consts = (norm_g, w_in, b_in, ln_g, ln_b, w_s, bs_tile, conv_w, conv_b, cln_g, cln_b,
              w_out, norm2_g, ffn_in, ffn_out)
    return pl.pallas_call(
        functools.partial(_odd_ffn_kernel, tm=tm, tiles_per_seq=nt, n_tiles=n_tiles, d_ff=d_ff, chunk=256),
        grid=(n_tiles + 1,),
        in_specs=[pl.BlockSpec((1, tm, D), lambda s: (prev(s) // nt, prev(s) % nt, 0)),
                  pl.BlockSpec((1, tm, D), lambda s: (cur(s) // nt, cur(s) % nt, 0))]
                 + [full(a) for a in consts],
        out_specs=pl.BlockSpec((1, tm, D), lambda s: (prev(s) // nt, prev(s) % nt, 0)),
        out_shape=jax.ShapeDtypeStruct((B, S, D), F32),
        scratch_shapes=[pltpu.VMEM((tm + CONV_HALO, CONV_CH), F32),
                        pltpu.VMEM((1, tm, GMLP_WIDTH + CONV_CH), BF16)],
        compiler_params=pltpu.CompilerParams(
            dimension_semantics=("arbitrary",),
            vmem_limit_bytes=VMEM_LIMIT_BYTES),
        name="odd_ffn",
    )(x, x, *consts)


def _rope_tables(seq, dim, seg_per_block):
    inv = 1.0 / (ROPE_THETA ** (jnp.arange(0, dim, 2, dtype=F32) / dim))
    ang = jnp.arange(seq, dtype=F32)[:, None] * inv[None, :]
    cos, sin = jnp.cos(ang), jnp.sin(ang)
    cos = jnp.tile(jnp.concatenate([cos, cos], axis=1), (1, seg_per_block))
    sin = jnp.tile(jnp.concatenate([-sin, sin], axis=1), (1, seg_per_block))
    return cos, sin


def _block_diag_ones(seg):
    idx = jnp.arange(BD) // seg
    return (idx[:, None] == idx[None, :]).astype(BF16)


def kernel(x, attn_norm_g, ffn_norm_g, ffn_w_in, ffn_w_out, even_w_in, even_w_out,
           diff_q_norm_g, diff_k_norm_g, diff_lambda_q1, diff_lambda_k1,
           diff_lambda_q2, diff_lambda_k2, diff_subln_g, moba_q_norm_g, moba_k_norm_g,
           odd_w_in, odd_b_in, odd_w_out, gmlp_ln_g, gmlp_ln_b, gmlp_w_s, gmlp_b_s,
           conv_w, conv_b, conv_ln_g, conv_ln_b):
    B, S, D = x.shape
    assert S % max(EVEN_PROJ_ROWS, OUT_FFN_ROWS) == 0 and S // MOBA_BLOCK <= GATE_SEG
    row = lambda v: v.reshape(1, -1).astype(F32)

    lambda_init = 0.8 - 0.6 * math.exp(-0.3 * 0)
    cosd, sind = _rope_tables(S, DIFF_HALF, LANES // DIFF_HALF)
    cosm, sinm = _rope_tables(S, HEAD_DIM, LANES // HEAD_DIM)
    gains = jnp.stack([
        jnp.tile(diff_q_norm_g[0], SECTION // DIFF_HALF),
        jnp.tile(diff_k_norm_g[0], SECTION // DIFF_HALF),
        jnp.tile(moba_q_norm_g[0], SECTION // HEAD_DIM),
        jnp.tile(moba_k_norm_g[0], SECTION // HEAD_DIM)]).astype(F32)
    qk, vt, bias = _even_proj(x, row(attn_norm_g[0]), even_w_in[0].astype(BF16),
                              _block_diag_ones(DIFF_HALF), _block_diag_ones(HEAD_DIM), gains,
                              cosd, sind, cosm, sinm)
    key_block = jnp.arange(S)[:, None] // MOBA_BLOCK
    ind = (key_block == (jnp.arange(LANES)[None, :] % GATE_SEG)).astype(BF16)
    lam_params = jnp.stack([diff_lambda_q1[0], diff_lambda_k1[0],
                            diff_lambda_q2[0], diff_lambda_k2[0]]).astype(F32)
    subln_tile = jnp.tile(diff_subln_g[0], LANES // HEAD_DIM).reshape(1, LANES).astype(F32)
    def score_bound(d, gq, gk):
        return (d * d ** -0.5 * LOG2E * (1.0 + 2.0 ** -7)
                * jnp.max(jnp.abs(gq.astype(F32))) * jnp.max(jnp.abs(gk.astype(F32))))

    bounded = jnp.logical_and(score_bound(DIFF_HALF, diff_q_norm_g[0], diff_k_norm_g[0]) <= SCORE_BOUND,
                              score_bound(HEAD_DIM, moba_q_norm_g[0], moba_k_norm_g[0]) <= SCORE_BOUND)
    attend = lambda flag: (lambda *a: _attention(*a, lambda_init, flag))
    diff, moba = lax.cond(bounded, attend(True), attend(False), qk, vt, bias, ind, lam_params, subln_tile)
    x = _out_ffn(diff, 0, moba, 0, x, even_w_out[0], row(ffn_norm_g[0]), ffn_w_in, ffn_w_out, 0)

    bs_tile = jnp.repeat(gmlp_b_s[0].T, GMLP_WIDTH // GMLP_GROUPS, axis=1).astype(F32)
    return _odd_ffn(x, row(attn_norm_g[1]), odd_w_in[0].astype(BF16), row(odd_b_in[0]),
                    row(gmlp_ln_g[0]), row(gmlp_ln_b[0]), gmlp_w_s[0].astype(F32), bs_tile,
                    conv_w[0].astype(F32), row(conv_b[0]), row(conv_ln_g[0]), row(conv_ln_b[0]),
                    odd_w_out[0].astype(BF16), row(ffn_norm_g[1]),
                    ffn_w_in[1].astype(BF16), ffn_w_out[1].astype(BF16))
```

```python
import functools
import math

import jax
import jax.numpy as jnp
from jax import lax
from jax.experimental import pallas as pl
from jax.experimental.pallas import tpu as pltpu

F32 = jnp.float32
BF16 = jnp.bfloat16

LANES = 128
VMEM_LIMIT_BYTES = 56 * 1024 * 1024

HEAD_DIM = 64
DIFF_HALF = 32
N_HEADS_MOBA = 8
MOBA_BLOCK = 256
MOBA_TOPK = 3
GMLP_CHUNK = 128
GMLP_GROUPS = 8
GMLP_WIDTH = 512
CONV_CH = 512
CONV_KERNEL = 31
ROPE_THETA = 10000.0
EPS = 1e-6

SECTION = 512
BD = 256
EVEN_PROJ_ROWS = 1024
OUT_FFN_ROWS = 512
ATTN_STREAMS = 2
BOUNDED_STREAMS = 4
SCORE_BOUND = 40.0
ONES_ROWS = 16
GATE_SEG = LANES // N_HEADS_MOBA
CONV_HALO = 32
MASKED_BIAS = -1e30
M_INIT = -1e29
LOG2E = math.log2(math.e)

_NT = (((1,), (1,)), ((), ()))


def _rms(x, g):
    return x * lax.rsqrt(jnp.mean(x * x, axis=-1, keepdims=True) + EPS) * g


def _layer_norm(x, g, b):
    mu = jnp.mean(x, axis=-1, keepdims=True)
    xc = x - mu
    return xc * lax.rsqrt(jnp.mean(xc * xc, axis=-1, keepdims=True) + EPS) * g + b


def _even_proj_kernel(x_ref, g_ref, w_ref, bdd_ref, bdm_ref, gain_ref,
                      cosd_ref, sind_ref, cosm_ref, sinm_ref,
                      qk_ref, vt_ref, bias_ref, kbt_ref, *, tm):
    blocks = tm // MOBA_BLOCK
    first_blk = pl.program_id(1) * blocks

    @pl.when(first_blk == 0)
    def _():
        kbt_ref[...] = jnp.zeros_like(kbt_ref)

    h = _rms(x_ref[0], g_ref[...]).astype(BF16)

    def qk_section(sec, bd_ref, seg, cos_ref, sin_ref):
        y = jnp.dot(h, w_ref[:, sec * SECTION:(sec + 1) * SECTION], preferred_element_type=F32)
        y2 = (y * y).astype(BF16)
        ss = jnp.concatenate(
            [jnp.dot(y2[:, c * BD:(c + 1) * BD], bd_ref[...], preferred_element_type=F32)
             for c in range(SECTION // BD)], axis=1)
        yn = y * lax.rsqrt(ss * (1.0 / seg) + EPS) * gain_ref[sec_row[sec]:sec_row[sec] + 1, :]
        half = seg // 2
        lower = (lax.broadcasted_iota(jnp.int32, (tm, LANES), 1) & (seg - 1)) < half
        cos = cos_ref[...]
        sin = sin_ref[...]
        out = []
        for c in range(SECTION // LANES):
            yc = yn[:, c * LANES:(c + 1) * LANES]
            rot = jnp.where(lower, pltpu.roll(yc, LANES - half, 1), pltpu.roll(yc, half, 1))
            out.append(yc * cos + rot * sin)
        return out

    def store(out_sec, chunks, scale):
        for c, yc in enumerate(chunks):
            lo = out_sec * SECTION + c * LANES
            qk_ref[0, :, lo:lo + LANES] = (yc * scale).astype(BF16)

    sec_row = {0: 0, 1: 1, 3: 2, 4: 3}
    store(0, qk_section(0, bdd_ref, DIFF_HALF, cosd_ref, sind_ref), DIFF_HALF ** -0.5 * LOG2E)
    store(1, qk_section(1, bdd_ref, DIFF_HALF, cosd_ref, sind_ref), 1.0)
    mq = qk_section(3, bdm_ref, HEAD_DIM, cosm_ref, sinm_ref)
    store(2, mq, HEAD_DIM ** -0.5 * LOG2E)
    mk = qk_section(4, bdm_ref, HEAD_DIM, cosm_ref, sinm_ref)
    store(3, mk, 1.0)
    for out_sec, sec in enumerate((2, 5)):
        lo = sec * SECTION
        vt = jnp.dot(h, w_ref[:, lo:lo + SECTION], preferred_element_type=F32).T.astype(BF16)
        for sb in range(blocks):
            vt_ref[0, sb, out_sec * SECTION:(out_sec + 1) * SECTION, :] = (
                vt[:, sb * MOBA_BLOCK:(sb + 1) * MOBA_BLOCK])

    mk_full = jnp.concatenate(mk, axis=1)
    head_of_lane = lax.broadcasted_iota(jnp.int32, (1, SECTION), 1) // HEAD_DIM
    for sb in range(blocks):
        kbar = jnp.mean(mk_full[sb * MOBA_BLOCK:(sb + 1) * MOBA_BLOCK], axis=0, keepdims=True)
        for hh in range(N_HEADS_MOBA):
            kbt_ref[pl.ds(hh * GATE_SEG + first_blk + sb, 1), :] = jnp.where(head_of_lane == hh, kbar, 0.0)

    def split(a):
        hi = a.astype(BF16)
        return hi, (a - hi.astype(F32)).astype(BF16)

    q_hi, q_lo = split(jnp.concatenate(mq, axis=1))
    k_hi, k_lo = split(kbt_ref[...])
    gate = lax.dot_general(jnp.concatenate([k_hi, k_hi, k_lo], axis=1),
                           jnp.concatenate([q_hi, q_lo, q_hi], axis=1), _NT,
                           preferred_element_type=F32)
    blk = lax.broadcasted_iota(jnp.int32, (GATE_SEG, tm), 0)
    blk_f = blk.astype(F32)
    neg_inf = jnp.float32(-jnp.inf)
    own = first_blk + lax.broadcasted_iota(jnp.int32, (GATE_SEG, tm), 1) // MOBA_BLOCK
    bias_rows = []
    for hh in range(N_HEADS_MOBA):
        avail = jnp.where(blk < own, gate[hh * GATE_SEG:(hh + 1) * GATE_SEG], neg_inf)
        chosen = blk == own
        for _ in range(MOBA_TOPK):
            best = jnp.max(avail, axis=0, keepdims=True)
            cand = jnp.logical_and(avail == best, avail > neg_inf)
            first = jnp.min(jnp.where(cand, blk_f, float(GATE_SEG)), axis=0, keepdims=True)
            pick = blk_f == first
            chosen = jnp.logical_or(chosen, pick)
            avail = jnp.where(pick, neg_inf, avail)
        bias_rows.append(jnp.where(chosen, 0.0, MASKED_BIAS))
    bias_ref[0] = jnp.concatenate(bias_rows, axis=0).T.astype(BF16)


def _even_proj(x, norm_g, w_in, bdd, bdm, gains, cosd, sind, cosm, sinm):
    B, S, D = x.shape
    tm = EVEN_PROJ_ROWS
    blocks = tm // MOBA_BLOCK
    n_out = w_in.shape[1]
    n_qk = 4 * SECTION
    n_v = 2 * SECTION
    const = lambda b, t: (0, 0)
    tab = pl.BlockSpec((tm, LANES), lambda b, t: (t, 0))
    return pl.pallas_call(
        functools.partial(_even_proj_kernel, tm=tm),
        grid=(B, S // tm),
        in_specs=[
            pl.BlockSpec((1, tm, D), lambda b, t: (b, t, 0)),
            pl.BlockSpec((1, D), const),
            pl.BlockSpec((D, n_out), const, pipeline_mode=pl.Buffered(1)),
            pl.BlockSpec((BD, BD), const),
            pl.BlockSpec((BD, BD), const),
            pl.BlockSpec((4, SECTION), const),
            tab, tab, tab, tab,
        ],
        out_specs=[
            pl.BlockSpec((1, tm, n_qk), lambda b, t: (b, t, 0)),
            pl.BlockSpec((1, blocks, n_v, MOBA_BLOCK), lambda b, t: (b, t, 0, 0)),
            pl.BlockSpec((1, tm, LANES), lambda b, t: (b, t, 0)),
        ],
        out_shape=[
            jax.ShapeDtypeStruct((B, S, n_qk), BF16),
            jax.ShapeDtypeStruct((B, S // MOBA_BLOCK, n_v, MOBA_BLOCK), BF16),
            jax.ShapeDtypeStruct((B, S, LANES), BF16),
        ],
        scratch_shapes=[pltpu.VMEM((LANES, SECTION), F32)],
        compiler_params=pltpu.CompilerParams(
            dimension_semantics=("arbitrary", "arbitrary"),
            vmem_limit_bytes=VMEM_LIMIT_BYTES),
        name="even_proj",
    )(x, norm_g, w_in, bdd, bdm, gains, cosd, sind, cosm, sinm)


def _flash_head_pair(make_q, n, v_rows, k_ref, vt_ref, s_ref, p_ref, nq, t, emit, k_aug_ref=None):
    streams = k_ref.shape[0]
    per_stream = n // streams

    def scores(q_all, j):
        start = pl.multiple_of(j * t, t)
        out = []
        for g in range(streams):
            k = k_ref[g, pl.ds(start, t), :]
            if k_aug_ref is not None:
                k = jnp.concatenate([k, k_aug_ref[pl.ds(start, t), :]], axis=1)
            out.append(lax.dot_general(k, q_all[g], _NT, preferred_element_type=F32))
        return jnp.concatenate(out, axis=1)

    def softmax(s_all, stats, diagonal):
        ps, new_stats, alphas = [], [], []
        for i in range(n):
            m, l = stats[i]
            s = s_all[:, i * t:(i + 1) * t]
            if diagonal:
                key = lax.broadcasted_iota(jnp.int32, (t, t), 0)
                qry = lax.broadcasted_iota(jnp.int32, (t, t), 1)
                s = jnp.where(key <= qry, s, -jnp.inf)
            m_new = jnp.maximum(m, jnp.max(s, axis=0, keepdims=True))
            p = jnp.exp2(s - m_new)
            alpha = jnp.exp2(m - m_new)
            ps.append(p.astype(BF16))
            new_stats.append((m_new, alpha * l + jnp.sum(p, axis=0, keepdims=True)))
            alphas.append(alpha)
        return tuple(ps), tuple(new_stats), alphas

    def weighted_values(j, ps):
        out = []
        for g in range(streams):
            vt = vt_ref[g, j]
            for li, (lo, hi) in enumerate(v_rows):
                out.append(jnp.dot(vt[lo:hi], ps[g * per_stream + li], preferred_element_type=F32))
        return out

    def step(q_all, j, cur, nxt, carry):
        stats, accs = carry
        s_ref[nxt] = scores(q_all, j + 1)
        p_prev = tuple(p_ref[nxt, i] for i in range(n))
        pv = weighted_values(jnp.maximum(j - 1, 0), p_prev)
        pv = [jnp.where(j > 0, x, 0.0) for x in pv]
        p_cur, stats, alphas = softmax(s_ref[cur], stats, False)
        for i in range(n):
            p_ref[cur, i] = p_cur[i]
        accs = tuple(a * (acc + x) for a, acc, x in zip(alphas, accs, pv))
        return stats, accs

    def query_tile(qi, first, second, odd, next_first):
        q_all = make_q(qi)
        init = (tuple((jnp.full((1, t), M_INIT, F32), jnp.zeros((1, t), F32)) for _ in range(n)),
                tuple(jnp.zeros((hi - lo, t), F32) for _ in range(streams) for lo, hi in v_rows))
        carry = lax.fori_loop(
            0, qi // 2,
            lambda i, c: step(q_all, 2 * i + 1, second, first, step(q_all, 2 * i, first, second, c)), init)
        if odd:
            carry = step(q_all, qi - 1, first, second, carry)
        diag, prev = (second, first) if odd else (first, second)
        stats, accs = carry
        s_ref[next_first] = scores(make_q(jnp.minimum(qi + 1, nq - 1)), 0)
        pv_prev = weighted_values(jnp.maximum(qi - 1, 0), tuple(p_ref[prev, i] for i in range(n)))
        pv_prev = [jnp.where(qi > 0, x, 0.0) for x in pv_prev]
        p_cur, stats, alphas = softmax(s_ref[diag], stats, True)
        pv_cur = weighted_values(qi, p_cur)
        emit(qi, [(a * (acc + x) + y, l)
                  for a, acc, x, y, (_, l) in zip(alphas, accs, pv_prev, pv_cur, stats)])

    p_ref[...] = jnp.zeros(p_ref.shape, BF16)
    s_ref[0] = scores(make_q(0), 0)

    def pair_of_tiles(u, _):
        query_tile(2 * u, 0, 1, False, 2)
        query_tile(2 * u + 1, 2, 1, True, 0)
        return 0

    lax.fori_loop(0, nq // 2, pair_of_tiles, 0)


def _flash_bounded(make_q, n, v_rows, k_ref, vt_ref, s_ref, nq, t, emit, k_aug_ref=None):
    streams = k_ref.shape[0]
    per_stream = n // streams
    ones = jnp.ones((ONES_ROWS, t), BF16)
    width = per_stream * t
    key = lax.broadcasted_iota(jnp.int32, (t, width), 0)
    qry = lax.broadcasted_iota(jnp.int32, (t, width), 1) & (t - 1)

    def scores(q_all, j, slot):
        start = pl.multiple_of(j * t, t)
        for g in range(streams):
            k = k_ref[g, pl.ds(start, t), :]
            if k_aug_ref is not None:
                k = jnp.concatenate([k, k_aug_ref[pl.ds(start, t), :]], axis=1)
            s_ref[slot, g] = lax.dot_general(k, q_all[g], _NT, preferred_element_type=F32)

    def accumulate(j, slot, accs, diagonal):
        out = []
        for g in range(streams):
            p = jnp.exp2(s_ref[slot, g])
            if diagonal:
                p = jnp.where(key <= qry, p, 0.0)
            p = p.astype(BF16)
            vt = vt_ref[g, j]
            for li, (lo, hi) in enumerate(v_rows):
                lhs = jnp.concatenate([vt[lo:hi], ones], axis=0)
                out.append(accs[g * per_stream + li]
                           + jnp.dot(lhs, p[:, li * t:(li + 1) * t], preferred_element_type=F32))
        return tuple(out)

    def step(q_all, j, cur, nxt, accs):
        scores(q_all, j + 1, nxt)
        return accumulate(j, cur, accs, False)

    rows = [hi - lo for _ in range(streams) for lo, hi in v_rows]

    def query_tile(u, first, second, odd, next_first):
        qi = 2 * u + odd
        q_all = make_q(qi)

        def four_steps(i, a):
            j = 4 * i
            a = step(q_all, j, first, second, a)
            a = step(q_all, j + 1, second, first, a)
            a = step(q_all, j + 2, first, second, a)
            return step(q_all, j + 3, second, first, a)

        init = tuple(jnp.zeros((r + ONES_ROWS, t), F32) for r in rows)
        accs = lax.fori_loop(0, qi // 4, four_steps, init)

        def tail(accs, two_more):
            j = 4 * (qi // 4)
            if two_more:
                accs = step(q_all, j, first, second, accs)
                accs = step(q_all, j + 1, second, first, accs)
                j = j + 2
            if odd:
                accs = step(q_all, j, first, second, accs)
            scores(make_q(jnp.minimum(qi + 1, nq - 1)), 0, next_first)
            accs = accumulate(qi, second if odd else first, accs, True)
            emit(qi, [(a[:r], a[r:r + 1]) for a, r in zip(accs, rows)])

        pl.when((u & 1) == 0)(lambda: tail(accs, False))
        pl.when((u & 1) == 1)(lambda: tail(accs, True))

    scores(make_q(0), 0, 0)

    def pair_of_tiles(u, _):
        query_tile(u, 0, 1, 0, 2)
        query_tile(u, 2, 1, 1, 0)
        return 0

    lax.fori_loop(0, nq // 2, pair_of_tiles, 0)


def _diff_attn_kernel(lam_ref, q_ref, k_ref, vt_ref, sg_ref, o_ref, *scratch, t, lambda_init, bounded):
    nq = q_ref.shape[1] // t
    seg = lax.broadcasted_iota(jnp.int32, (t, LANES), 1) // DIFF_HALF
    lam_p = lam_ref[...]
    lam = (jnp.exp(jnp.sum(lam_p[0:1] * lam_p[1:2], axis=1, keepdims=True))
           - jnp.exp(jnp.sum(lam_p[2:3] * lam_p[3:4], axis=1, keepdims=True)) + lambda_init)

    streams = q_ref.shape[0]

    def make_q(qi):
        out = []
        for g in range(streams):
            q = q_ref[g, pl.ds(pl.multiple_of(qi * t, t), t), :]
            out.append(jnp.concatenate([jnp.where(seg == i, q, jnp.zeros_like(q)) for i in range(4)], axis=0))
        return out

    def emit(qi, res):
        a = [acc * (1.0 / l) for acc, l in res]
        for g in range(streams):
            heads = []
            for hh in range(2):
                o = a[4 * g + 2 * hh] - lam * a[4 * g + 2 * hh + 1]
                heads.append(o * lax.rsqrt(jnp.mean(o * o, axis=0, keepdims=True) + EPS))
            o = jnp.concatenate(heads, axis=0).T
            o_ref[g, pl.ds(pl.multiple_of(qi * t, t), t), :] = (
                o * sg_ref[...] * (1.0 - lambda_init)).astype(BF16)

    v_rows = [(HEAD_DIM * (i // 2), HEAD_DIM * (i // 2 + 1)) for i in range(4)]
    if bounded:
        _flash_bounded(make_q, 4 * streams, v_rows, k_ref, vt_ref, *scratch, nq, t, emit)
    else:
        _flash_head_pair(make_q, 4 * streams, v_rows, k_ref, vt_ref, *scratch, nq, t, emit)


def _moba_attn_kernel(q_ref, bias_ref, k_ref, vt_ref, ind_ref, o_ref, *scratch, t, bounded):
    hp = pl.program_id(1)
    nq = q_ref.shape[1] // t
    lane = lax.broadcasted_iota(jnp.int32, (t, LANES), 1)

    streams = q_ref.shape[0]

    def make_q(qi):
        rows = pl.ds(pl.multiple_of(qi * t, t), t)
        out = []
        for g in range(streams):
            q = q_ref[g, rows, :]
            bias = bias_ref[g, rows, :]
            strips = []
            for hh in range(2):
                qz = jnp.where(lane // HEAD_DIM == hh, q, jnp.zeros_like(q))
                bz = jnp.where(lane // GATE_SEG == 2 * hp + hh, bias, jnp.zeros_like(bias))
                strips.append(jnp.concatenate([qz, bz], axis=1))
            out.append(jnp.concatenate(strips, axis=0))
        return out

    def emit(qi, res):
        for g in range(streams):
            o = jnp.concatenate([acc * (1.0 / l) for acc, l in res[2 * g:2 * g + 2]], axis=0)
            o_ref[g, pl.ds(pl.multiple_of(qi * t, t), t), :] = o.T.astype(BF16)

    v_rows = [(0, HEAD_DIM), (HEAD_DIM, 2 * HEAD_DIM)]
    if bounded:
        _flash_bounded(make_q, 2 * streams, v_rows, k_ref, vt_ref, *scratch, nq, t, emit, k_aug_ref=ind_ref)
    else:
        _flash_head_pair(make_q, 2 * streams, v_rows, k_ref, vt_ref, *scratch, nq, t, emit, k_aug_ref=ind_ref)


def _attention(qk, vt, bias, ind, lam_params, subln_tile, lambda_init, bounded):
    B, S, _ = qk.shape
    t = MOBA_BLOCK
    n_pairs = SECTION // LANES
    assert (S // t) % 2 == 0
    params = pltpu.CompilerParams(
        dimension_semantics=("arbitrary", "arbitrary"),
        vmem_limit_bytes=VMEM_LIMIT_BYTES)
    out_shape = jax.ShapeDtypeStruct((B, S, SECTION), BF16)

    def streams(wanted):
        return wanted if B % wanted == 0 else 1

    def seq_spec(g, col):
        return pl.BlockSpec((g, S, LANES), lambda b, p: (b, 0, col(p)))

    def vt_spec(g, sec):
        return pl.BlockSpec((g, S // t, LANES, t), lambda b, p: (b, 0, sec * n_pairs + p, 0))

    def pipeline_scratch(g, n):
        if bounded:
            return [pltpu.VMEM((3, g, t, n * t), F32)]
        return [pltpu.VMEM((3, t, g * n * t), F32), pltpu.VMEM((3, g * n, t, t), BF16)]

    g = streams(BOUNDED_STREAMS if bounded else ATTN_STREAMS)
    diff = pl.pallas_call(
        functools.partial(_diff_attn_kernel, t=t, lambda_init=lambda_init, bounded=bounded),
        grid=(B // g, n_pairs),
        in_specs=[pl.BlockSpec((4, DIFF_HALF), lambda b, p: (0, 0)),
                  seq_spec(g, lambda p: p), seq_spec(g, lambda p: n_pairs + p), vt_spec(g, 0),
                  pl.BlockSpec((1, LANES), lambda b, p: (0, 0))],
        out_specs=seq_spec(g, lambda p: p), out_shape=out_shape, compiler_params=params,
        scratch_shapes=pipeline_scratch(g, 4),
        name="diff_attn",
    )(lam_params, qk, qk, vt, subln_tile)

    g = streams(BOUNDED_STREAMS if bounded else ATTN_STREAMS)
    moba = pl.pallas_call(
        functools.partial(_moba_attn_kernel, t=t, bounded=bounded),
        grid=(B // g, n_pairs),
        in_specs=[seq_spec(g, lambda p: 2 * n_pairs + p),
                  seq_spec(g, lambda p: 0),
                  seq_spec(g, lambda p: 3 * n_pairs + p), vt_spec(g, 1),
                  pl.BlockSpec((S, LANES), lambda b, p: (0, 0))],
        out_specs=seq_spec(g, lambda p: p), out_shape=out_shape, compiler_params=params,
        scratch_shapes=pipeline_scratch(g, 2),
        name="moba_attn",
    )(qk, bias, qk, vt, ind)
    return diff, moba


def _out_ffn_kernel(mixa_ref, mixb_ref, x_ref, wo_ref, g_ref, wi_ref, w2_ref, o_ref, *, d_ff, chunk):
    mix = jnp.concatenate([mixa_ref[0], mixb_ref[0]], axis=1)
    x1 = x_ref[0] + jnp.dot(mix, wo_ref[...].astype(BF16), preferred_element_type=F32)
    hn = _rms(x1, g_ref[...]).astype(BF16)
    acc = x1
    for c in range(d_ff // chunk):
        lo = c * chunk
        g = jnp.dot(hn, wi_ref[0, :, lo:lo + chunk].astype(BF16), preferred_element_type=F32)
        u = jnp.dot(hn, wi_ref[0, :, d_ff + lo:d_ff + lo + chunk].astype(BF16), preferred_element_type=F32)
        a = (g * jax.nn.sigmoid(g) * u).astype(BF16)
        acc = acc + jnp.dot(a, w2_ref[0, lo:lo + chunk, :].astype(BF16), preferred_element_type=F32)
    o_ref[0] = acc


def _out_ffn(mix_a, col_a, mix_b, col_b, x, w_out, norm_g, w_in, w2, layer):
    B, S, D = x.shape
    d_ff = w2.shape[1]
    tm = OUT_FFN_ROWS
    half = w_out.shape[0] // 2
    const = lambda b, t: (0, 0)
    resident = lambda shape: pl.BlockSpec(shape, const, pipeline_mode=pl.Buffered(1))
    slab = lambda a: pl.BlockSpec((1,) + a.shape[1:], lambda b, t: (layer, 0, 0), pipeline_mode=pl.Buffered(1))
    tile = pl.BlockSpec((1, tm, D), lambda b, t: (b, t, 0))
    mix_spec = lambda col: pl.BlockSpec((1, tm, half), lambda b, t: (b, t, col))
    return pl.pallas_call(
        functools.partial(_out_ffn_kernel, d_ff=d_ff, chunk=256),
        grid=(B, S // tm),
        in_specs=[mix_spec(col_a), mix_spec(col_b), tile, resident(w_out.shape), pl.BlockSpec((1, D), const),
                  slab(w_in), slab(w2)],
        out_specs=tile,
        out_shape=jax.ShapeDtypeStruct((B, S, D), F32),
        compiler_params=pltpu.CompilerParams(
            dimension_semantics=("arbitrary", "arbitrary"),
            vmem_limit_bytes=VMEM_LIMIT_BYTES),
        name="out_ffn",
    )(mix_a, mix_b, x, w_out, norm_g, w_in, w2)


def _odd_mix_stages(x_ref, g_ref, w_ref, b_ref, lng_ref, lnb_ref, ws_ref, bs_ref,
                    cw_ref, cb_ref, clg_ref, clb_ref, o_ref, ext_ref, *, tm, ti):
    h = _rms(x_ref[0], g_ref[...]).astype(BF16)

    def proj(lo, width):
        return jnp.dot(h, w_ref[:, lo:lo + width], preferred_element_type=F32) + b_ref[:, lo:lo + width]

    z_u = proj(0, GMLP_WIDTH)
    z_v = proj(GMLP_WIDTH, GMLP_WIDTH)
    z_a = proj(2 * GMLP_WIDTH, CONV_CH)
    z_g = proj(2 * GMLP_WIDTH + CONV_CH, CONV_CH)
    yield

    gu = jax.nn.gelu(z_u, approximate=True)
    gv = jax.nn.gelu(z_v, approximate=True)
    gv = _layer_norm(gv, lng_ref[...], lnb_ref[...]).astype(BF16)
    T = GMLP_CHUNK
    tri = lax.broadcasted_iota(jnp.int32, (T, T), 1) <= lax.broadcasted_iota(jnp.int32, (T, T), 0)
    lane = lax.broadcasted_iota(jnp.int32, (T, LANES), 1)
    first = lane < (GMLP_WIDTH // GMLP_GROUPS)
    zero = jnp.zeros((T, LANES), BF16)
    for lb in range(GMLP_WIDTH // LANES):
        wpair = jnp.concatenate(
            [jnp.where(tri, ws_ref[2 * lb + i], 0.0).astype(BF16) for i in range(2)], axis=1)
        bs = bs_ref[:, lb * LANES:(lb + 1) * LANES]
        for c in range(tm // T):
            vc = gv[c * T:(c + 1) * T, lb * LANES:(lb + 1) * LANES]
            vpair = jnp.concatenate([jnp.where(first, vc, zero), jnp.where(first, zero, vc)], axis=0)
            sg = jnp.dot(wpair, vpair, preferred_element_type=F32) + bs
            o_ref[0, c * T:(c + 1) * T, lb * LANES:(lb + 1) * LANES] = (
                gu[c * T:(c + 1) * T, lb * LANES:(lb + 1) * LANES] * sg).astype(BF16)

    cval = z_a * jax.nn.sigmoid(z_g)

    ext_ref[0:CONV_HALO, :] = jnp.where(ti == 0, 0.0, ext_ref[0:CONV_HALO, :])
    ext_ref[CONV_HALO:CONV_HALO + tm, :] = cval
    first_tap = CONV_HALO - (CONV_KERNEL - 1)
    ext = ext_ref[...]
    rows_total = ext.shape[0]
    conv = jnp.zeros((tm, CONV_CH), F32) + cb_ref[...]
    for r in range(8):
        rolled = ext if r == 0 else pltpu.roll(ext, rows_total - r, 0)
        for k in range(CONV_KERNEL):
            if (first_tap + k) % 8 == r:
                off = first_tap + k - r
                conv = conv + rolled[off:off + tm, :] * cw_ref[k:k + 1, :]
    ext_ref[0:CONV_HALO, :] = ext_ref[tm:tm + CONV_HALO, :]
    dn = _layer_norm(conv, clg_ref[...], clb_ref[...])
    o_ref[0, :, GMLP_WIDTH:GMLP_WIDTH + CONV_CH] = (dn * jax.nn.sigmoid(dn)).astype(BF16)


def _odd_ffn_kernel(xa_ref, xb_ref, g_ref, w_ref, b_ref, lng_ref, lnb_ref, ws_ref, bs_ref,
                    cw_ref, cb_ref, clg_ref, clb_ref, wo_ref, g2_ref, wi_ref, w2_ref,
                    o_ref, ext_ref, mix_ref, *, tm, tiles_per_seq, n_tiles, d_ff, chunk):
    s = pl.program_id(0)

    @pl.when(s == 0)
    def _():
        mix_ref[...] = jnp.zeros_like(mix_ref)
        ext_ref[0:CONV_HALO, :] = jnp.zeros((CONV_HALO, CONV_CH), F32)

    tile = jnp.minimum(s, n_tiles - 1)
    mixers = _odd_mix_stages(xb_ref, g_ref, w_ref, b_ref, lng_ref, lnb_ref, ws_ref, bs_ref,
                             cw_ref, cb_ref, clg_ref, clb_ref, mix_ref, ext_ref,
                             tm=tm, ti=tile % tiles_per_seq)
    mix_prev = mix_ref[0]
    next(mixers)

    x1 = xa_ref[0] + jnp.dot(mix_prev, wo_ref[...], preferred_element_type=F32)
    hn = _rms(x1, g2_ref[...]).astype(BF16)
    acc = x1
    for c in range(d_ff // chunk):
        lo = c * chunk
        gate = jnp.dot(hn, wi_ref[:, lo:lo + chunk], preferred_element_type=F32)
        up = jnp.dot(hn, wi_ref[:, d_ff + lo:d_ff + lo + chunk], preferred_element_type=F32)
        a = (gate * jax.nn.sigmoid(gate) * up).astype(BF16)
        acc = acc + jnp.dot(a, w2_ref[lo:lo + chunk, :], preferred_element_type=F32)
    o_ref[0] = acc

    for _ in mixers:
        pass


def _odd_ffn(x, norm_g, w_in, b_in, ln_g, ln_b, w_s, bs_tile, conv_w, conv_b, cln_g, cln_b,
             w_out, norm2_g, ffn_in, ffn_out):
    B, S, D = x.shape
    tm = OUT_FFN_ROWS
    nt = S // tm
    n_tiles = B * nt
    d_ff = ffn_out.shape[0]
    prev = lambda s: jnp.maximum(s - 1, 0)
    cur = lambda s: jnp.minimum(s, n_tiles - 1)
    full = lambda a: pl.BlockSpec(a.shape, (lambda s: (0,) * a.ndim), pipeline_mode=pl.Buffered(1))
    consts = (norm_g, w_in, b_in, ln_g, ln_b, w_s, bs_tile, conv_w, conv_b, cln_g, cln_b,
              w_out, norm2_g, ffn_in, ffn_out)
    return pl.pallas_call(
        functools.partial(_odd_ffn_kernel, tm=tm, tiles_per_seq=nt, n_tiles=n_tiles, d_ff=d_ff, chunk=256),
        grid=(n_tiles + 1,),
        in_specs=[pl.BlockSpec((1, tm, D), lambda s: (prev(s) // nt, prev(s) % nt, 0)),
                  pl.BlockSpec((1, tm, D), lambda s: (cur(s) // nt, cur(s) % nt, 0))]
                 + [full(a) for a in consts],
        out_specs=pl.BlockSpec((1, tm, D), lambda s: (prev(s) // nt, prev(s) % nt, 0)),
        out_shape=jax.ShapeDtypeStruct((B, S, D), F32),
        scratch_shapes=[pltpu.VMEM((tm + CONV_HALO, CONV_CH), F32),
                        pltpu.VMEM((1, tm, GMLP_WIDTH + CONV_CH), BF16)],
        compiler_params=pltpu.CompilerParams(
            dimension_semantics=("arbitrary",),
            vmem_limit_bytes=VMEM_LIMIT_BYTES),
        name="odd_ffn",
    )(x, x, *consts)


def _rope_tables(seq, dim, seg_per_block):
    inv = 1.0 / (ROPE_THETA ** (jnp.arange(0, dim, 2, dtype=F32) / dim))
    ang = jnp.arange(seq, dtype=F32)[:, None] * inv[None, :]
    cos, sin = jnp.cos(ang), jnp.sin(ang)
    cos = jnp.tile(jnp.concatenate([cos, cos], axis=1), (1, seg_per_block))
    sin = jnp.tile(jnp.concatenate([-sin, sin], axis=1), (1, seg_per_block))
    return cos, sin


def _block_diag_ones(seg):
    idx = jnp.arange(BD) // seg
    return (idx[:, None] == idx[None, :]).astype(BF16)


def kernel(x, attn_norm_g, ffn_norm_g, ffn_w_in, ffn_w_out, even_w_in, even_w_out,
           diff_q_norm_g, diff_k_norm_g, diff_lambda_q1, diff_lambda_k1,
           diff_lambda_q2, diff_lambda_k2, diff_subln_g, moba_q_norm_g, moba_k_norm_g,
           odd_w_in, odd_b_in, odd_w_out, gmlp_ln_g, gmlp_ln_b, gmlp_w_s, gmlp_b_s,
           conv_w, conv_b, conv_ln_g, conv_ln_b):
    B, S, D = x.shape
    assert S % max(EVEN_PROJ_ROWS, OUT_FFN_ROWS) == 0 and S // MOBA_BLOCK <= GATE_SEG
    row = lambda v: v.reshape(1, -1).astype(F32)

    lambda_init = 0.8 - 0.6 * math.exp(-0.3 * 0)
    cosd, sind = _rope_tables(S, DIFF_HALF, LANES // DIFF_HALF)
    cosm, sinm = _rope_tables(S, HEAD_DIM, LANES // HEAD_DIM)
    gains = jnp.stack([
        jnp.tile(diff_q_norm_g[0], SECTION // DIFF_HALF),
        jnp.tile(diff_k_norm_g[0], SECTION // DIFF_HALF),
        jnp.tile(moba_q_norm_g[0], SECTION // HEAD_DIM),
        jnp.tile(moba_k_norm_g[0], SECTION // HEAD_DIM)]).astype(F32)
    qk, vt, bias = _even_proj(x, row(attn_norm_g[0]), even_w_in[0].astype(BF16),
                              _block_diag_ones(DIFF_HALF), _block_diag_ones(HEAD_DIM), gains,
                              cosd, sind, cosm, sinm)
    key_block = jnp.arange(S)[:, None] // MOBA_BLOCK
    ind = (key_block == (jnp.arange(LANES)[None, :] % GATE_SEG)).astype(BF16)
    lam_params = jnp.stack([diff_lambda_q1[0], diff_lambda_k1[0],
                            diff_lambda_q2[0], diff_lambda_k2[0]]).astype(F32)
    subln_tile = jnp.tile(diff_subln_g[0], LANES // HEAD_DIM).reshape(1, LANES).astype(F32)
    def score_bound(d, gq, gk):
        return (d * d ** -0.5 * LOG2E * (1.0 + 2.0 ** -7)
                * jnp.max(jnp.abs(gq.astype(F32))) * jnp.max(jnp.abs(gk.astype(F32))))

    bounded = jnp.logical_and(score_bound(DIFF_HALF, diff_q_norm_g[0], diff_k_norm_g[0]) <= SCORE_BOUND,
                              score_bound(HEAD_DIM, moba_q_norm_g[0], moba_k_norm_g[0]) <= SCORE_BOUND)
    attend = lambda flag: (lambda *a: _attention(*a, lambda_init, flag))
    diff, moba = lax.cond(bounded, attend(True), attend(False), qk, vt, bias, ind, lam_params, subln_tile)
    x = _out_ffn(diff, 0, moba, 0, x, even_w_out[0], row(ffn_norm_g[0]), ffn_w_in, ffn_w_out, 0)

    bs_tile = jnp.repeat(gmlp_b_s[0].T, GMLP_WIDTH // GMLP_GROUPS, axis=1).astype(F32)
    return _odd_ffn(x, row(attn_norm_g[1]), odd_w_in[0].astype(BF16), row(odd_b_in[0]),
                    row(gmlp_ln_g[0]), row(gmlp_ln_b[0]), gmlp_w_s[0].astype(F32), bs_tile,
                    conv_w[0].astype(F32), row(conv_b[0]), row(conv_ln_g[0]), row(conv_ln_b[0]),
                    odd_w_out[0].astype(BF16), row(ffn_norm_g[1]),
                    ffn_w_in[1].astype(BF16), ffn_w_out[1].astype(BF16))
```
